```python
import math
import jax
import jax.numpy as jnp
from jax import lax
import numpy as np


D_MODEL = 2048
BATCH = 4
SEQ = 2048
DEPTH = 1

ML_HEADS = 4
ML_QK_DIM = 128
ML_V_DIM = 256
CONV_WIDTH = 4
CHUNK = 64
MLA_HEADS = 8
MLA_NOPE_DIM = 128
MLA_ROPE_DIM = 64
MLA_V_DIM = 128
Q_LORA_RANK = 512
KV_LORA_RANK = 256
ROPE_THETA = 10000.0
Q_BLOCK = 128
D_FF = 5632
FFN_RES_WEIGHT = 0.5
N_SUBLAYERS = 3
ALPHA = (2.0 * DEPTH) ** 0.25
BETA = (8.0 * DEPTH) ** -0.25
LN_EPS = 1e-5
IN_SIZES = (ML_HEADS * ML_QK_DIM, ML_HEADS * ML_QK_DIM, ML_HEADS * ML_V_DIM, ML_HEADS * ML_V_DIM, ML_HEADS, ML_HEADS, Q_LORA_RANK, KV_LORA_RANK, MLA_ROPE_DIM)
D_IN = sum(IN_SIZES)
MIX_WIDTH = ML_HEADS * ML_V_DIM + MLA_HEADS * MLA_V_DIM

kernel_name = 'hybrid_mlstm_mla_macaron_deepnorm'


def split_last(x, sizes):
    outs = []
    off = 0
    for s in sizes:
        outs.append(x[..., off:off + s])
        off += s
    return outs


def layer_norm(x, g, b):
    xf = x.astype(jnp.float32)
    mu = jnp.mean(xf, axis=-1, keepdims=True)
    var = jnp.mean(jnp.square(xf - mu), axis=-1, keepdims=True)
    return ((xf - mu) * lax.rsqrt(var + LN_EPS) * g + b).astype(x.dtype)


def rms_norm(x, g):
    xf = x.astype(jnp.float32)
    return (xf * lax.rsqrt(jnp.mean(jnp.square(xf), axis=-1, keepdims=True) + LN_EPS) * g).astype(x.dtype)


def modulate(x, shift, scale):
    return x * (1.0 + scale[:, None, :]) + shift[:, None, :]


def swiglu(u, w1, w3, w2):
    return (jax.nn.silu(u @ w1) * (u @ w3)) @ w2


def causal_conv(x, w, b):
    seq = x.shape[1]
    xp = jnp.pad(x, ((0, 0), (CONV_WIDTH - 1, 0), (0, 0)))
    return sum(xp[:, j:j + seq] * w[j] for j in range(CONV_WIDTH)) + b


def rope(x):
    seq = x.shape[1]
    half = x.shape[-1] // 2
    inv = ROPE_THETA ** (-jnp.arange(half, dtype=jnp.float32) / half)
    ang = jnp.arange(seq, dtype=jnp.float32)[:, None] * inv[None, :]
    cos = jnp.cos(ang)[None, :, None, :]
    sin = jnp.sin(ang)[None, :, None, :]
    x1 = x[..., :half].astype(jnp.float32)
    x2 = x[..., half:].astype(jnp.float32)
    return jnp.concatenate([x1 * cos - x2 * sin, x1 * sin + x2 * cos], axis=-1).astype(x.dtype)


def mlstm_chunkwise(q, k, v, i_pre, f_pre):
    bsz, nh, seq, dk = q.shape
    dv = v.shape[-1]
    nc = seq // CHUNK
    qc = q.reshape(bsz, nh, nc, CHUNK, dk)
    kc = k.reshape(bsz, nh, nc, CHUNK, dk)
    vc = v.reshape(bsz, nh, nc, CHUNK, dv)
    logi = i_pre.reshape(bsz, nh, nc, CHUNK)
    bcum = jnp.cumsum(jax.nn.log_sigmoid(f_pre).reshape(bsz, nh, nc, CHUNK), axis=-1)
    g = bcum[..., -1]
    a = g[..., None] - bcum + logi
    m_loc = jnp.max(a, axis=-1)
    wgt = jnp.exp(a - m_loc[..., None])
    c_loc = jnp.einsum('bhcld,bhcle->bhcde', kc * wgt[..., None], vc)
    n_loc = jnp.einsum('bhcl,bhcld->bhcd', wgt, kc)

    def step(carry, inp):
        c_st, n_st, m_st = carry
        g_c, m_l, c_l, n_l = inp
        m_new = jnp.maximum(g_c + m_st, m_l)
        s_old = jnp.exp(g_c + m_st - m_new)
        s_loc = jnp.exp(m_l - m_new)
        c_new = s_old[..., None, None] * c_st + s_loc[..., None, None] * c_l
        n_new = s_old[..., None] * n_st + s_loc[..., None] * n_l
        return (c_new, n_new, m_new), (c_st, n_st, m_st)

    init = (jnp.zeros((bsz, nh, dk, dv), jnp.float32), jnp.zeros((bsz, nh, dk), jnp.float32), jnp.zeros((bsz, nh), jnp.float32))
    xs = (jnp.moveaxis(g, 2, 0), jnp.moveaxis(m_loc, 2, 0), jnp.moveaxis(c_loc, 2, 0), jnp.moveaxis(n_loc, 2, 0))
    _, (c_in, n_in, m_in) = lax.scan(step, init, xs)
    c_in = jnp.moveaxis(c_in, 0, 2)
    n_in = jnp.moveaxis(n_in, 0, 2)
    m_in = jnp.moveaxis(m_in, 0, 2)

    causal = jnp.tril(jnp.ones((CHUNK, CHUNK), dtype=bool))
    log_d = jnp.where(causal, bcum[..., :, None] - bcum[..., None, :] + logi[..., None, :], -jnp.inf)
    log_inter = bcum + m_in[..., None]
    m_t = jnp.maximum(log_inter, jnp.max(log_d, axis=-1))
    s_inter = jnp.exp(log_inter - m_t)
    w_intra = jnp.exp(log_d - m_t[..., None]) * jnp.einsum('bhctd,bhcjd->bhctj', qc, kc)
    num = jnp.einsum('bhctj,bhcje->bhcte', w_intra, vc) + s_inter[..., None] * jnp.einsum('bhctd,bhcde->bhcte', qc, c_in)
    den = jnp.sum(w_intra, axis=-1) + s_inter * jnp.einsum('bhctd,bhcd->bhct', qc, n_in)
    h = num / jnp.maximum(jnp.abs(den), jnp.exp(-m_t))[..., None]
    return h.reshape(bsz, nh, seq, dv)


def blocked_causal_attention(q, k, v):
    bsz, nh, seq, dq = q.shape
    nb = seq // Q_BLOCK
    scale = 1.0 / math.sqrt(dq)
    qb = q.reshape(bsz, nh, nb, Q_BLOCK, dq).transpose(2, 0, 1, 3, 4)
    kpos = jnp.arange(seq)

    def one_block(args):
        q_blk, start = args
        s = jnp.einsum('bhqd,bhkd->bhqk', q_blk, k).astype(jnp.float32) * scale
        qpos = start + jnp.arange(Q_BLOCK)
        s = jnp.where(kpos[None, :] <= qpos[:, None], s, -jnp.inf)
        p = jax.nn.softmax(s, axis=-1).astype(v.dtype)
        return jnp.einsum('bhqk,bhkd->bhqd', p, v)

    o = lax.map(one_block, (qb, jnp.arange(nb) * Q_BLOCK))
    return o.transpose(1, 2, 0, 3, 4).reshape(bsz, nh, seq, v.shape[-1])


def hybrid_mixer(u, w_in, conv_w, conv_b, b_igate, b_fgate, ml_norm_g, q_norm_g, w_uq, kv_norm_g, w_ukv, w_out):
    bsz, seq, _ = u.shape
    f32 = jnp.float32
    ml_q, ml_k, ml_v, ml_o, ml_i, ml_f, c_q, c_kv, k_r = split_last(u @ w_in, IN_SIZES)

    qk = jax.nn.silu(causal_conv(jnp.concatenate([ml_q, ml_k], axis=-1), conv_w, conv_b))
    qk = qk.astype(f32).reshape(bsz, seq, 2, ML_HEADS, ML_QK_DIM).transpose(2, 0, 3, 1, 4)
    q_m = qk[0]
    k_m = qk[1] * (ML_QK_DIM ** -0.5)
    v_m = ml_v.astype(f32).reshape(bsz, seq, ML_HEADS, ML_V_DIM).transpose(0, 2, 1, 3)
    i_pre = (ml_i + b_igate).astype(f32).transpose(0, 2, 1)
    f_pre = (ml_f + b_fgate).astype(f32).transpose(0, 2, 1)
    h = mlstm_chunkwise(q_m, k_m, v_m, i_pre, f_pre)
    mu = jnp.mean(h, axis=-1, keepdims=True)
    var = jnp.mean(jnp.square(h - mu), axis=-1, keepdims=True)
    h = (h - mu) * lax.rsqrt(var + LN_EPS) * ml_norm_g.astype(f32).reshape(ML_HEADS, 1, ML_V_DIM)
    h = h.transpose(0, 2, 1, 3).reshape(bsz, seq, ML_HEADS * ML_V_DIM)
    h_ml = (h * jax.nn.sigmoid(ml_o.astype(f32))).astype(u.dtype)

    q_l = (rms_norm(c_q, q_norm_g) @ w_uq).reshape(bsz, seq, MLA_HEADS, MLA_NOPE_DIM + MLA_ROPE_DIM)
    q_a = jnp.concatenate([q_l[..., :MLA_NOPE_DIM], rope(q_l[..., MLA_NOPE_DIM:])], axis=-1)
    kv = (rms_norm(c_kv, kv_norm_g) @ w_ukv).reshape(bsz, seq, MLA_HEADS, MLA_NOPE_DIM + MLA_V_DIM)
    k_rope = jnp.broadcast_to(rope(k_r[:, :, None, :]), (bsz, seq, MLA_HEADS, MLA_ROPE_DIM))
    k_a = jnp.concatenate([kv[..., :MLA_NOPE_DIM], k_rope], axis=-1)
    v_a = kv[..., MLA_NOPE_DIM:]
    o = blocked_causal_attention(q_a.transpose(0, 2, 1, 3), k_a.transpose(0, 2, 1, 3), v_a.transpose(0, 2, 1, 3))
    h_mla = o.transpose(0, 2, 1, 3).reshape(bsz, seq, MLA_HEADS * MLA_V_DIM)

    return jnp.concatenate([h_ml, h_mla], axis=-1) @ w_out


def setup_inputs(seed: int = 0) -> dict:
    key = jax.random.key(seed)
    ks = jax.random.split(key, 32)
    L = DEPTH
    D = D_MODEL

    def nrm(k, shape, scale):
        return jax.random.normal(k, shape, jnp.float32) * scale

    def gain(k, n):
        return 1.0 + nrm(k, (L, n), 0.01)

    def bias(k, n):
        return nrm(k, (L, n), 0.01)

    return {
        'x': nrm(ks[0], (BATCH, SEQ, D), 1.0),
        'c': nrm(ks[1], (BATCH, D), 1.0),
        'w_ada': nrm(ks[2], (L, D, N_SUBLAYERS * 3 * D), 0.1 * D ** -0.5),
        'b_ada': bias(ks[3], N_SUBLAYERS * 3 * D),
        'ffn1_w1': nrm(ks[4], (L, D, D_FF), D ** -0.5),
        'ffn1_w3': nrm(ks[5], (L, D, D_FF), D ** -0.5),
        'ffn1_w2': nrm(ks[6], (L, D_FF, D), BETA * D_FF ** -0.5),
        'ln1_g': gain(ks[7], D),
        'ln1_b': bias(ks[8], D),
        'w_in': nrm(ks[9], (L, D, D_IN), D ** -0.5),
        'conv_w': nrm(ks[10], (L, CONV_WIDTH, 2 * ML_HEADS * ML_QK_DIM), CONV_WIDTH ** -0.5),
        'conv_b': bias(ks[11], 2 * ML_HEADS * ML_QK_DIM),
        'b_igate': nrm(ks[12], (L, ML_HEADS), 0.1),
        'b_fgate': jnp.linspace(3.0, 6.0, ML_HEADS, dtype=jnp.float32)[None, :] + nrm(ks[13], (L, ML_HEADS), 0.1),
        'ml_norm_g': gain(ks[14], ML_HEADS * ML_V_DIM),
        'q_norm_g': gain(ks[15], Q_LORA_RANK),
        'w_uq': nrm(ks[16], (L, Q_LORA_RANK, MLA_HEADS * (MLA_NOPE_DIM + MLA_ROPE_DIM)), Q_LORA_RANK ** -0.5),
        'kv_norm_g': gain(ks[17], KV_LORA_RANK),
        'w_ukv': nrm(ks[18], (L, KV_LORA_RANK, MLA_HEADS * (MLA_NOPE_DIM + MLA_V_DIM)), KV_LORA_RANK ** -0.5),
        'w_out': nrm(ks[19], (L, MIX_WIDTH, D), BETA * MIX_WIDTH ** -0.5),
        'ln2_g': gain(ks[20], D),
        'ln2_b': bias(ks[21], D),
        'ffn2_w1': nrm(ks[22], (L, D, D_FF), D ** -0.5),
        'ffn2_w3': nrm(ks[23], (L, D, D_FF), D ** -0.5),
        'ffn2_w2': nrm(ks[24], (L, D_FF, D), BETA * D_FF ** -0.5),
        'ln3_g': gain(ks[25], D),
        'ln3_b': bias(ks[26], D),
    }


def reference(x, c, w_ada, b_ada, ffn1_w1, ffn1_w3, ffn1_w2, ln1_g, ln1_b, w_in, conv_w, conv_b, b_igate, b_fgate, ml_norm_g, q_norm_g, w_uq, kv_norm_g, w_ukv, w_out, ln2_g, ln2_b, ffn2_w1, ffn2_w3, ffn2_w2, ln3_g, ln3_b):
    bsz = x.shape[0]
    for l in range(DEPTH):
        mod = (jax.nn.silu(c) @ w_ada[l] + b_ada[l]).reshape(bsz, N_SUBLAYERS, 3, D_MODEL)
        shift, scale, gate = mod[:, :, 0], mod[:, :, 1], mod[:, :, 2]
        y = swiglu(modulate(x, shift[:, 0], scale[:, 0]), ffn1_w1[l], ffn1_w3[l], ffn1_w2[l])
        x = layer_norm(ALPHA * x + FFN_RES_WEIGHT * (1.0 + gate[:, 0])[:, None, :] * y, ln1_g[l], ln1_b[l])
        y = hybrid_mixer(modulate(x, shift[:, 1], scale[:, 1]), w_in[l], conv_w[l], conv_b[l], b_igate[l], b_fgate[l], ml_norm_g[l], q_norm_g[l], w_uq[l], kv_norm_g[l], w_ukv[l], w_out[l])
        x = layer_norm(ALPHA * x + (1.0 + gate[:, 1])[:, None, :] * y, ln2_g[l], ln2_b[l])
        y = swiglu(modulate(x, shift[:, 2], scale[:, 2]), ffn2_w1[l], ffn2_w3[l], ffn2_w2[l])
        x = layer_norm(ALPHA * x + FFN_RES_WEIGHT * (1.0 + gate[:, 2])[:, None, :] * y, ln3_g[l], ln3_b[l])
    return x
```

```python
import functools
import math

import jax
import jax.numpy as jnp
from jax import lax
from jax.experimental import pallas as pl
from jax.experimental.pallas import tpu as pltpu

F32 = jnp.float32
BF16 = jnp.bfloat16

ML_HEADS = 4
ML_QK_DIM = 128
ML_V_DIM = 256
CONV_WIDTH = 4
MLA_HEADS = 8
MLA_NOPE_DIM = 128
MLA_ROPE_DIM = 64
MLA_V_DIM = 128
Q_LORA_RANK = 512
KV_LORA_RANK = 256
ROPE_THETA = 10000.0
FFN_RES_WEIGHT = 0.5
N_SUBLAYERS = 3
LN_EPS = 1e-5

LANES = 128
SUBLANES = 8
VMEM_LIMIT_BYTES = 56 * 1024 * 1024

ML_QK_COLS = 2 * ML_HEADS * ML_QK_DIM
ML_V_COLS = ML_HEADS * ML_V_DIM
PA_COLS = ML_QK_COLS + 2 * ML_V_COLS
PB_CQ = 0
PB_CKV = PB_CQ + Q_LORA_RANK
PB_KR = PB_CKV + KV_LORA_RANK
PB_GATES = PB_KR + LANES
PB_COLS = PB_GATES + LANES
MLA_QK_GROUP = 2 * LANES
NEG_BIG = -1e30


def _params(semantics):
    return pltpu.CompilerParams(dimension_semantics=semantics, vmem_limit_bytes=VMEM_LIMIT_BYTES)


def _silu(v):
    return v * jax.nn.sigmoid(v)


def _layer_norm_rows(z, g, b):
    mu = jnp.mean(z, axis=-1, keepdims=True)
    zc = z - mu
    var = jnp.mean(zc * zc, axis=-1, keepdims=True)
    return zc * lax.rsqrt(var + LN_EPS) * g + b


def _adaln_kernel(c_ref, w_ref, b_ref, o_ref):
    sc = _silu(c_ref[...]).astype(BF16)
    o_ref[...] = jnp.dot(sc, w_ref[...].astype(BF16), preferred_element_type=F32) + b_ref[...]


def _adaln(c, w, b, *, tn=1024):
    bsz, d = c.shape
    n = w.shape[1]
    rows = -(-bsz // SUBLANES) * SUBLANES
    c_pad = jnp.pad(c, ((0, rows - bsz), (0, 0)))
    out = pl.pallas_call(
        _adaln_kernel,
        grid=(n // tn,),
        in_specs=[
            pl.BlockSpec((rows, d), lambda j: (0, 0)),
            pl.BlockSpec((d, tn), lambda j: (0, j)),
            pl.BlockSpec((1, tn), lambda j: (0, j)),
        ],
        out_specs=pl.BlockSpec((rows, tn), lambda j: (0, j)),
        out_shape=jax.ShapeDtypeStruct((rows, n), F32),
        compiler_params=_params(("arbitrary",)),
        name="adaln",
    )(c_pad, w, b.reshape(1, n))
    return out[:bsz]


def _ffn_kernel(x_ref, mod_ref, w1_ref, w3_ref, w2_ref, g_ref, b_ref, o_ref, u_ref, *,
                sub, tiles_per_batch, alpha, row_chunk):
    i = pl.program_id(0)
    j = pl.program_id(1)
    nj = pl.num_programs(1)
    bidx = i // tiles_per_batch
    tm = x_ref.shape[0]
    n_chunks = tm // row_chunk

    @pl.when(j == 0)
    def _():
        shift = mod_ref[bidx, 3 * sub:3 * sub + 1, :]
        scale1 = 1.0 + mod_ref[bidx, 3 * sub + 1:3 * sub + 2, :]

        def body(r, carry):
            rows = pl.ds(pl.multiple_of(r * row_chunk, row_chunk), row_chunk)
            u_ref[rows, :] = (x_ref[rows, :] * scale1 + shift).astype(BF16)
            o_ref[rows, :] = jnp.zeros((row_chunk, o_ref.shape[1]), F32)
            return carry

        lax.fori_loop(0, n_chunks, body, 0)

    u = u_ref[...]
    a = jnp.dot(u, w1_ref[...].astype(BF16), preferred_element_type=F32)
    b = jnp.dot(u, w3_ref[...].astype(BF16), preferred_element_type=F32)
    h = (_silu(a) * b).astype(BF16)
    o_ref[...] += jnp.dot(h, w2_ref[...].astype(BF16), preferred_element_type=F32)

    @pl.when(j == nj - 1)
    def _():
        gate = FFN_RES_WEIGHT * (1.0 + mod_ref[bidx, 3 * sub + 2:3 * sub + 3, :])
        g = g_ref[...]
        b_ln = b_ref[...]

        def body(r, carry):
            rows = pl.ds(pl.multiple_of(r * row_chunk, row_chunk), row_chunk)
            z = alpha * x_ref[rows, :] + gate * o_ref[rows, :]
            o_ref[rows, :] = _layer_norm_rows(z, g, b_ln)
            return carry

        lax.fori_loop(0, n_chunks, body, 0)


def _ffn(x2d, mod, w1, w3, w2, ln_g, ln_b, *, sub, seq, alpha, tm=1024, tf=256, row_chunk=64):
    n_tok, d = x2d.shape
    f = w1.shape[1]
    tm = min(tm, seq)
    kern = functools.partial(_ffn_kernel, sub=sub, tiles_per_batch=seq // tm, alpha=alpha,
                             row_chunk=row_chunk)
    return pl.pallas_call(
        kern,
        grid=(n_tok // tm, f // tf),
        in_specs=[
            pl.BlockSpec((tm, d), lambda i, j: (i, 0)),
            pl.BlockSpec(mod.shape, lambda i, j: (0, 0, 0)),
            pl.BlockSpec((d, tf), lambda i, j: (0, j)),
            pl.BlockSpec((d, tf), lambda i, j: (0, j)),
            pl.BlockSpec((tf, d), lambda i, j: (j, 0)),
            pl.BlockSpec((1, d), lambda i, j: (0, 0)),
            pl.BlockSpec((1, d), lambda i, j: (0, 0)),
        ],
        out_specs=pl.BlockSpec((tm, d), lambda i, j: (i, 0)),
        out_shape=jax.ShapeDtypeStruct((n_tok, d), F32),
        scratch_shapes=[pltpu.VMEM((tm, d), BF16)],
        compiler_params=_params(("arbitrary", "arbitrary")),
        name=f"ffn{sub}",
    )(x2d, mod, w1, w3, w2, ln_g.reshape(1, d), ln_b.reshape(1, d))


def _inproj_kernel(x_ref, mod_ref, w_ref, oa_ref, ob_ref, u_ref, *, sub, tiles_per_batch, n_a):
    i = pl.program_id(0)
    j = pl.program_id(1)
    bidx = i // tiles_per_batch

    @pl.when(j == 0)
    def _():
        shift = mod_ref[bidx, 3 * sub:3 * sub + 1, :]
        scale1 = 1.0 + mod_ref[bidx, 3 * sub + 1:3 * sub + 2, :]
        u_ref[...] = (x_ref[...] * scale1 + shift).astype(BF16)

    r = jnp.dot(u_ref[...], w_ref[...], preferred_element_type=F32)

    @pl.when(j < n_a)
    def _():
        oa_ref[...] = r.astype(BF16)

    @pl.when(j >= n_a)
    def _():
        ob_ref[...] = r


def _inproj(x2d, mod, w_all, *, sub, seq, tm=512, tn=512):
    n_tok, d = x2d.shape
    n_a = PA_COLS // tn
    n_b = PB_COLS // tn
    kern = functools.partial(_inproj_kernel, sub=sub, tiles_per_batch=seq // tm, n_a=n_a)
    return pl.pallas_call(
        kern,
        grid=(n_tok // tm, n_a + n_b),
        in_specs=[
            pl.BlockSpec((tm, d), lambda i, j: (i, 0)),
            pl.BlockSpec(mod.shape, lambda i, j: (0, 0, 0)),
            pl.BlockSpec((d, tn), lambda i, j: (0, j)),
        ],
        out_specs=[
            pl.BlockSpec((tm, tn), lambda i, j: (i, jnp.minimum(j, n_a - 1))),
            pl.BlockSpec((tm, tn), lambda i, j: (i, jnp.maximum(j - n_a, 0))),
        ],
        out_shape=[
            jax.ShapeDtypeStruct((n_tok, PA_COLS), BF16),
            jax.ShapeDtypeStruct((n_tok, PB_COLS), F32),
        ],
        scratch_shapes=[pltpu.VMEM((tm, d), BF16)],
        compiler_params=_params(("arbitrary", "arbitrary")),
        name="inproj",
    )(x2d, mod, w_all)


def _inproj_weights(w_in):
    d = w_in.shape[0]
    sizes = (ML_HEADS * ML_QK_DIM, ML_HEADS * ML_QK_DIM, ML_V_COLS, ML_V_COLS, ML_HEADS, ML_HEADS,
             Q_LORA_RANK, KV_LORA_RANK, MLA_ROPE_DIM)
    offs = [0]
    for s in sizes:
        offs.append(offs[-1] + s)
    piece = lambda k: w_in[:, offs[k]:offs[k + 1]]
    zeros = lambda n: jnp.zeros((d, n), w_in.dtype)
    cols = [piece(0), piece(1), piece(2), piece(3),
            piece(6), piece(7), piece(8), zeros(LANES - MLA_ROPE_DIM),
            piece(4), piece(5), zeros(LANES - 2 * ML_HEADS)]
    return jnp.concatenate(cols, axis=1).astype(BF16)


def _lane_scan(v, op, chunk):
    lane = lax.broadcasted_iota(jnp.int32, v.shape, 1)
    ident = 0.0 if op is jnp.add else NEG_BIG
    s = 1
    while s < chunk:
        shifted = pltpu.roll(v, s, axis=1)
        v = op(v, jnp.where(lane >= s, shifted, ident))
        s *= 2
    return v


def _mlstm_kernel(qk_ref, v_ref, og_ref, gates_ref, cw_ref, cb_ref, gb_ref, ng_ref, o_ref,
                  c_ref, m_ref, halo_ref, *, chunk):
    t = pl.program_id(1)
    hq = ML_HEADS * ML_QK_DIM

    @pl.when(t == 0)
    def _():
        c_ref[...] = jnp.zeros_like(c_ref)
        m_ref[...] = jnp.zeros_like(m_ref)
        halo_ref[...] = jnp.zeros_like(halo_ref)

    qk_pre = qk_ref[0].astype(F32)
    ext = jnp.concatenate([halo_ref[...], qk_pre], axis=0)
    halo_ref[...] = qk_pre[chunk - SUBLANES:, :]
    conv = cb_ref[...]
    for tap in range(CONV_WIDTH):
        lo = SUBLANES - (CONV_WIDTH - 1) + tap
        conv = conv + cw_ref[tap:tap + 1, :] * ext[lo:lo + chunk, :]
    qk = _silu(conv)

    z = gates_ref[0].T[0:SUBLANES, :] + gb_ref[:, 0:1]
    logi = pltpu.roll(z, ML_HEADS, axis=0)
    logf = jnp.minimum(z, 0.0) - jnp.log1p(jnp.exp(-jnp.abs(z)))
    bcum = _lane_scan(logf, jnp.add, chunk)
    r = logi - bcum
    cm = _lane_scan(r, jnp.maximum, chunk)
    m_in = m_ref[:, 0:1]
    m_all = jnp.maximum(m_in, cm)
    m_last = m_all[:, chunk - 1:chunk]
    m_ref[...] = jnp.broadcast_to(bcum[:, chunk - 1:chunk] + m_last, m_ref.shape)
    s_inter = jnp.exp(m_in - m_all)
    e_neg_m = jnp.exp(-(bcum + m_all))
    w_last = jnp.exp(r - m_last)
    s_old = jnp.exp(m_in - m_last)

    stack = jnp.concatenate(
        [m_all, s_inter, e_neg_m, w_last, jnp.zeros((LANES - 4 * SUBLANES, chunk), F32)], axis=0)
    cols = stack.T

    row_t = lax.broadcasted_iota(jnp.int32, (chunk, chunk), 0)
    col_j = lax.broadcasted_iota(jnp.int32, (chunk, chunk), 1)
    causal = col_j <= row_t
    ones_col = (lax.broadcasted_iota(jnp.int32, (chunk, LANES), 1) == 0).astype(BF16)

    for h in range(ML_HEADS):
        row = ML_HEADS + h
        q_h = qk[:, h * ML_QK_DIM:(h + 1) * ML_QK_DIM].astype(BF16)
        k_f = qk[:, hq + h * ML_QK_DIM:hq + (h + 1) * ML_QK_DIM] * (ML_QK_DIM ** -0.5)
        v_aug = jnp.concatenate([v_ref[0, :, h * ML_V_DIM:(h + 1) * ML_V_DIM], ones_col], axis=1)
        m_col = cols[:, row:row + 1]
        s_col = cols[:, SUBLANES + row:SUBLANES + row + 1]
        e_col = cols[:, 2 * SUBLANES + row:2 * SUBLANES + row + 1]
        w_col = cols[:, 3 * SUBLANES + row:3 * SUBLANES + row + 1]

        decay = jnp.exp(jnp.where(causal, r[row:row + 1, :] - m_col, NEG_BIG))
        scores = lax.dot_general(q_h, k_f.astype(BF16), (((1,), (1,)), ((), ())),
                                 preferred_element_type=F32)
        p = (decay * scores).astype(BF16)
        c_old = c_ref[h]
        tot = (jnp.dot(p, v_aug, preferred_element_type=F32)
               + s_col * jnp.dot(q_h, c_old.astype(BF16), preferred_element_type=F32))
        num = tot[:, :ML_V_DIM]
        den = tot[:, ML_V_DIM:ML_V_DIM + 1]
        hid = num / jnp.maximum(jnp.abs(den), e_col)

        k_w = (k_f * w_col).astype(BF16)
        c_ref[h] = s_old[row:row + 1, :] * c_old + lax.dot_general(
            k_w, v_aug, (((0,), (0,)), ((), ())), preferred_element_type=F32)

        mu = jnp.mean(hid, axis=-1, keepdims=True)
        hc = hid - mu
        var = jnp.mean(hc * hc, axis=-1, keepdims=True)
        hn = hc * lax.rsqrt(var + LN_EPS) * ng_ref[:, h * ML_V_DIM:(h + 1) * ML_V_DIM]
        gate = jax.nn.sigmoid(og_ref[0, :, h * ML_V_DIM:(h + 1) * ML_V_DIM].astype(F32))
        o_ref[0, :, h * ML_V_DIM:(h + 1) * ML_V_DIM] = (hn * gate).astype(o_ref.dtype)


def _mlstm(pa, pb, conv_w, conv_b, b_igate, b_fgate, norm_g, *, chunk=128):
    bsz, seq, _ = pa.shape
    gate_bias = jnp.broadcast_to(jnp.concatenate([b_igate, b_fgate])[:, None], (SUBLANES, LANES))
    kern = functools.partial(_mlstm_kernel, chunk=chunk)
    return pl.pallas_call(
        kern,
        grid=(bsz, seq // chunk),
        in_specs=[
            pl.BlockSpec((1, chunk, ML_QK_COLS), lambda b, t: (b, t, 0)),
            pl.BlockSpec((1, chunk, ML_V_COLS), lambda b, t: (b, t, ML_QK_COLS // ML_V_COLS)),
            pl.BlockSpec((1, chunk, ML_V_COLS), lambda b, t: (b, t, ML_QK_COLS // ML_V_COLS + 1)),
            pl.BlockSpec((1, chunk, LANES), lambda b, t: (b, t, PB_GATES // LANES)),
            pl.BlockSpec((CONV_WIDTH, ML_QK_COLS), lambda b, t: (0, 0)),
            pl.BlockSpec((1, ML_QK_COLS), lambda b, t: (0, 0)),
            pl.BlockSpec((SUBLANES, LANES), lambda b, t: (0, 0)),
            pl.BlockSpec((1, ML_V_COLS), lambda b, t: (0, 0)),
        ],
        out_specs=pl.BlockSpec((1, chunk, ML_V_COLS), lambda b, t: (b, t, 0)),
        out_shape=jax.ShapeDtypeStruct((bsz, seq, ML_V_COLS), BF16),
        scratch_shapes=[
            pltpu.VMEM((ML_HEADS, ML_QK_DIM, ML_V_DIM + LANES), F32),
            pltpu.VMEM((SUBLANES, LANES), F32),
            pltpu.VMEM((SUBLANES, ML_QK_COLS), F32),
        ],
        compiler_params=_params(("arbitrary", "arbitrary")),
        name="mlstm",
    )(pa, pa, pa, pb, conv_w, conv_b.reshape(1, -1), gate_bias, norm_g.reshape(1, -1))


def _rope_lanes(x, cos, sin_lo, sin_hi):
    half = MLA_ROPE_DIM // 2
    return x * cos + pltpu.roll(x, LANES - half, axis=1) * sin_lo + pltpu.roll(x, half, axis=1) * sin_hi


def _rms_rows(x, g):
    return x * lax.rsqrt(jnp.mean(x * x, axis=-1, keepdims=True) + LN_EPS) * g


def _mla_prep_kernel(cq_ref, ckv_ref, kr_ref, qg_ref, kvg_ref, wq_ref, wkv_ref, cos_ref, sl_ref, sh_ref,
                     q_ref, k_ref, v_ref, *, q_scale):
    cos = cos_ref[...]
    sin_lo = sl_ref[...]
    sin_hi = sh_ref[...]
    q_lat = _rms_rows(cq_ref[...], qg_ref[...]).astype(BF16)
    kv_lat = _rms_rows(ckv_ref[...], kvg_ref[...]).astype(BF16)
    k_rope = _rope_lanes(kr_ref[...], cos, sin_lo, sin_hi).astype(k_ref.dtype)
    for h in range(MLA_HEADS):
        lo = h * MLA_QK_GROUP
        q_h = jnp.dot(q_lat, wq_ref[:, lo:lo + MLA_QK_GROUP], preferred_element_type=F32) * q_scale
        q_ref[:, lo:lo + LANES] = q_h[:, :LANES].astype(q_ref.dtype)
        q_ref[:, lo + LANES:lo + 2 * LANES] = _rope_lanes(q_h[:, LANES:], cos, sin_lo, sin_hi).astype(q_ref.dtype)
        kv_h = jnp.dot(kv_lat, wkv_ref[:, lo:lo + MLA_QK_GROUP], preferred_element_type=F32)
        k_ref[:, lo:lo + LANES] = kv_h[:, :LANES].astype(k_ref.dtype)
        k_ref[:, lo + LANES:lo + 2 * LANES] = k_rope
        v_ref[:, h * MLA_V_DIM:(h + 1) * MLA_V_DIM] = kv_h[:, LANES:].astype(v_ref.dtype)


def _rope_tables(seq):
    half = MLA_ROPE_DIM // 2
    inv = ROPE_THETA ** (-jnp.arange(half, dtype=F32) / half)
    ang = jnp.arange(seq, dtype=F32)[:, None] * inv[None, :]
    cos, sin = jnp.cos(ang), jnp.sin(ang)
    zero = jnp.zeros_like(cos)
    cos_t = jnp.concatenate([cos, cos, zero, zero], axis=1)
    sin_lo = jnp.concatenate([-sin, zero, zero, zero], axis=1)
    sin_hi = jnp.concatenate([zero, sin, zero, zero], axis=1)
    return cos_t, sin_lo, sin_hi


def _mla_prep(pb2d, q_norm_g, kv_norm_g, w_uq, w_ukv, *, seq, tm=512):
    n_tok = pb2d.shape[0]
    dq = MLA_NOPE_DIM + MLA_ROPE_DIM
    wq = w_uq.reshape(Q_LORA_RANK, MLA_HEADS, dq)
    wq = jnp.pad(wq, ((0, 0), (0, 0), (0, MLA_QK_GROUP - dq))).reshape(Q_LORA_RANK, MLA_HEADS * MLA_QK_GROUP)
    cos_t, sin_lo, sin_hi = _rope_tables(seq)
    tiles_per_seq = seq // tm
    pos = lambda i: (i % tiles_per_seq, 0)
    kern = functools.partial(_mla_prep_kernel, q_scale=1.0 / math.sqrt(dq))
    qk_cols = MLA_HEADS * MLA_QK_GROUP
    return pl.pallas_call(
        kern,
        grid=(n_tok // tm,),
        in_specs=[
            pl.BlockSpec((tm, Q_LORA_RANK), lambda i: (i, PB_CQ // Q_LORA_RANK)),
            pl.BlockSpec((tm, KV_LORA_RANK), lambda i: (i, PB_CKV // KV_LORA_RANK)),
            pl.BlockSpec((tm, LANES), lambda i: (i, PB_KR // LANES)),
            pl.BlockSpec((1, Q_LORA_RANK), lambda i: (0, 0)),
            pl.BlockSpec((1, KV_LORA_RANK), lambda i: (0, 0)),
            pl.BlockSpec((Q_LORA_RANK, qk_cols), lambda i: (0, 0)),
            pl.BlockSpec((KV_LORA_RANK, qk_cols), lambda i: (0, 0)),
            pl.BlockSpec((tm, LANES), pos),
            pl.BlockSpec((tm, LANES), pos),
            pl.BlockSpec((tm, LANES), pos),
        ],
        out_specs=[
            pl.BlockSpec((tm, qk_cols), lambda i: (i, 0)),
            pl.BlockSpec((tm, qk_cols), lambda i: (i, 0)),
            pl.BlockSpec((tm, MLA_HEADS * MLA_V_DIM), lambda i: (i, 0)),
        ],
        out_shape=[
            jax.ShapeDtypeStruct((n_tok, qk_cols), BF16),
            jax.ShapeDtypeStruct((n_tok, qk_cols), BF16),
            jax.ShapeDtypeStruct((n_tok, MLA_HEADS * MLA_V_DIM), BF16),
        ],
        compiler_params=_params(("arbitrary",)),
        name="mla_prep",
    )(pb2d, pb2d, pb2d, q_norm_g.reshape(1, -1), kv_norm_g.reshape(1, -1),
      wq.astype(BF16), w_ukv.astype(BF16), cos_t, sin_lo, sin_hi)


def _flash_kernel(q_ref, k_ref, v_ref, o_ref, m_ref, l_ref, acc_ref, *, tq, tk):
    qi = pl.program_id(2)
    kj = pl.program_id(3)

    @pl.when(kj == 0)
    def _():
        m_ref[...] = jnp.full_like(m_ref, NEG_BIG)
        l_ref[...] = jnp.zeros_like(l_ref)
        acc_ref[...] = jnp.zeros_like(acc_ref)

    @pl.when(kj * tk < (qi + 1) * tq)
    def _():
        s = lax.dot_general(q_ref[0], k_ref[0], (((1,), (1,)), ((), ())), preferred_element_type=F32)
        qpos = qi * tq + lax.broadcasted_iota(jnp.int32, (tq, tk), 0)
        kpos = kj * tk + lax.broadcasted_iota(jnp.int32, (tq, tk), 1)
        s = jnp.where(kpos <= qpos, s, NEG_BIG)
        m_prev = m_ref[:, 0:1]
        m_new = jnp.maximum(m_prev, jnp.max(s, axis=-1, keepdims=True))
        p = jnp.exp(s - m_new)
        alpha = jnp.exp(m_prev - m_new)
        l_ref[...] = jnp.broadcast_to(alpha * l_ref[:, 0:1] + jnp.sum(p, axis=-1, keepdims=True), l_ref.shape)
        m_ref[...] = jnp.broadcast_to(m_new, m_ref.shape)
        acc_ref[...] = alpha * acc_ref[...] + jnp.dot(p.astype(BF16), v_ref[0], preferred_element_type=F32)

    @pl.when(kj == ((qi + 1) * tq - 1) // tk)
    def _():
        o_ref[0] = (acc_ref[...] / l_ref[:, 0:1]).astype(o_ref.dtype)


def _flash(q, k, v, *, tq=512, tk=512):
    bsz, seq, _ = q.shape
    last_k = lambda qi: ((qi + 1) * tq - 1) // tk
    kern = functools.partial(_flash_kernel, tq=tq, tk=tk)
    return pl.pallas_call(
        kern,
        grid=(bsz, MLA_HEADS, seq // tq, seq // tk),
        in_specs=[
            pl.BlockSpec((1, tq, MLA_QK_GROUP), lambda b, h, qi, kj: (b, qi, h)),
            pl.BlockSpec((1, tk, MLA_QK_GROUP), lambda b, h, qi, kj: (b, jnp.minimum(kj, last_k(qi)), h)),
            pl.BlockSpec((1, tk, MLA_V_DIM), lambda b, h, qi, kj: (b, jnp.minimum(kj, last_k(qi)), h)),
        ],
        out_specs=pl.BlockSpec((1, tq, MLA_V_DIM), lambda b, h, qi, kj: (b, qi, h)),
        out_shape=jax.ShapeDtypeStruct((bsz, seq, MLA_HEADS * MLA_V_DIM), BF16),
        scratch_shapes=[
            pltpu.VMEM((tq, LANES), F32),
            pltpu.VMEM((tq, LANES), F32),
            pltpu.VMEM((tq, MLA_V_DIM), F32),
        ],
        compiler_params=_params(("arbitrary", "arbitrary", "arbitrary", "arbitrary")),
        name="flash",
    )(q, k, v)


def _outproj_kernel(x_ref, hml_ref, hmla_ref, mod_ref, w_ref, g_ref, b_ref, o_ref, *,
                    sub, tiles_per_batch, alpha):
    bidx = pl.program_id(0) // tiles_per_batch
    k_ml = hml_ref.shape[1]
    y = (jnp.dot(hml_ref[...], w_ref[0:k_ml, :], preferred_element_type=F32)
         + jnp.dot(hmla_ref[...], w_ref[k_ml:, :], preferred_element_type=F32))
    gate = 1.0 + mod_ref[bidx, 3 * sub + 2:3 * sub + 3, :]
    z = alpha * x_ref[...] + gate * y
    o_ref[...] = _layer_norm_rows(z, g_ref[...], b_ref[...])


def _outproj(x2d, hml, hmla, mod, w_out, ln_g, ln_b, *, sub, seq, alpha, tm=256):
    n_tok, d = x2d.shape
    kern = functools.partial(_outproj_kernel, sub=sub, tiles_per_batch=seq // tm, alpha=alpha)
    return pl.pallas_call(
        kern,
        grid=(n_tok // tm,),
        in_specs=[
            pl.BlockSpec((tm, d), lambda i: (i, 0)),
            pl.BlockSpec((tm, hml.shape[1]), lambda i: (i, 0)),
            pl.BlockSpec((tm, hmla.shape[1]), lambda i: (i, 0)),
            pl.BlockSpec(mod.shape, lambda i: (0, 0, 0)),
            pl.BlockSpec(w_out.shape, lambda i: (0, 0)),
            pl.BlockSpec((1, d), lambda i: (0, 0)),
            pl.BlockSpec((1, d), lambda i: (0, 0)),
        ],
        out_specs=pl.BlockSpec((tm, d), lambda i: (i, 0)),
        out_shape=jax.ShapeDtypeStruct((n_tok, d), F32),
        compiler_params=_params(("arbitrary",)),
        name="outproj",
    )(x2d, hml, hmla, mod, w_out, ln_g.reshape(1, d), ln_b.reshape(1, d))


def kernel(x, c, w_ada, b_ada, ffn1_w1, ffn1_w3, ffn1_w2, ln1_g, ln1_b, w_in, conv_w, conv_b, b_igate, b_fgate, ml_norm_g, q_norm_g, w_uq, kv_norm_g, w_ukv, w_out, ln2_g, ln2_b, ffn2_w1, ffn2_w3, ffn2_w2, ln3_g, ln3_b):
    bsz, seq, d = x.shape
    depth = w_ada.shape[0]
    alpha = (2.0 * depth) ** 0.25
    h2d = x.reshape(bsz * seq, d)
    for l in range(depth):
        mod = _adaln(c, w_ada[l], b_ada[l]).reshape(bsz, N_SUBLAYERS * 3, d)
        h2d = _ffn(h2d, mod, ffn1_w1[l], ffn1_w3[l], ffn1_w2[l], ln1_g[l], ln1_b[l],
                   sub=0, seq=seq, alpha=alpha)
        pa, pb = _inproj(h2d, mod, _inproj_weights(w_in[l]), sub=1, seq=seq)
        hml = _mlstm(pa.reshape(bsz, seq, PA_COLS), pb.reshape(bsz, seq, PB_COLS), conv_w[l], conv_b[l],
                     b_igate[l], b_fgate[l], ml_norm_g[l])
        q, k, v = _mla_prep(pb, q_norm_g[l], kv_norm_g[l], w_uq[l], w_ukv[l], seq=seq)
        hmla = _flash(q.reshape(bsz, seq, -1), k.reshape(bsz, seq, -1), v.reshape(bsz, seq, -1))
        h2d = _outproj(h2d, hml.reshape(bsz * seq, -1), hmla.reshape(bsz * seq, -1), mod,
                       w_out[l].astype(BF16), ln2_g[l], ln2_b[l], sub=1, seq=seq, alpha=alpha)
        h2d = _ffn(h2d, mod, ffn2_w1[l], ffn2_w3[l], ffn2_w2[l], ln3_g[l], ln3_b[l],
                   sub=2, seq=seq, alpha=alpha)
    return h2d.reshape(bsz, seq, d)
```

```python
import functools
import math

import jax
import jax.numpy as jnp
from jax import lax
from jax.experimental import pallas as pl
from jax.experimental.pallas import tpu as pltpu

F32 = jnp.float32
BF16 = jnp.bfloat16

ML_HEADS = 4
ML_QK_DIM = 128
ML_V_DIM = 256
CONV_WIDTH = 4
MLA_HEADS = 8
MLA_NOPE_DIM = 128
MLA_ROPE_DIM = 64
MLA_V_DIM = 128
Q_LORA_RANK = 512
KV_LORA_RANK = 256
ROPE_THETA = 10000.0
FFN_RES_WEIGHT = 0.5
N_SUBLAYERS = 3
LN_EPS = 1e-5

LANES = 128
SUBLANES = 8
VMEM_LIMIT_BYTES = 56 * 1024 * 1024

ML_QK_COLS = 2 * ML_HEADS * ML_QK_DIM
ML_V_COLS = ML_HEADS * ML_V_DIM
PA_COLS = ML_QK_COLS + 2 * ML_V_COLS
PB_CQ = 0
PB_CKV = PB_CQ + Q_LORA_RANK
PB_KR = PB_CKV + KV_LORA_RANK
PB_GATES = PB_KR + LANES
PB_COLS = PB_GATES + LANES
MLA_QK_GROUP = 2 * LANES
NEG_BIG = -1e30


def _params(semantics):
    return pltpu.CompilerParams(dimension_semantics=semantics, vmem_limit_bytes=VMEM_LIMIT_BYTES)


def _silu(v):
    return v * jax.nn.sigmoid(v)


def _layer_norm_rows(z, g, b):
    mu = jnp.mean(z, axis=-1, keepdims=True)
    zc = z - mu
    var = jnp.mean(zc * zc, axis=-1, keepdims=True)
    return zc * lax.rsqrt(var + LN_EPS) * g + b


def _adaln_kernel(c_ref, w_ref, b_ref, o_ref):
    sc = _silu(c_ref[...]).astype(BF16)
    o_ref[...] = jnp.dot(sc, w_ref[...].astype(BF16), preferred_element_type=F32) + b_ref[...]


def _adaln(c, w, b, *, tn=1024):
    bsz, d = c.shape
    n = w.shape[1]
    rows = -(-bsz // SUBLANES) * SUBLANES
    c_pad = jnp.pad(c, ((0, rows - bsz), (0, 0)))
    out = pl.pallas_call(
        _adaln_kernel,
        grid=(n // tn,),
        in_specs=[
            pl.BlockSpec((rows, d), lambda j: (0, 0)),
            pl.BlockSpec((d, tn), lambda j: (0, j)),
            pl.BlockSpec((1, tn), lambda j: (0, j)),
        ],
        out_specs=pl.BlockSpec((rows, tn), lambda j: (0, j)),
        out_shape=jax.ShapeDtypeStruct((rows, n), F32),
        compiler_params=_params(("arbitrary",)),
        name="adaln",
    )(c_pad, w, b.reshape(1, n))
    return out[:bsz]


def _ffn_kernel(x_ref, mod_ref, w1_ref, w3_ref, w2_ref, g_ref, b_ref, o_ref, u_ref, *,
                sub, tiles_per_batch, alpha, row_chunk):
    i = pl.program_id(0)
    j = pl.program_id(1)
    nj = pl.num_programs(1)
    bidx = i // tiles_per_batch
    tm = x_ref.shape[0]
    n_chunks = tm // row_chunk

    @pl.when(j == 0)
    def _():
        shift = mod_ref[bidx, 3 * sub:3 * sub + 1, :]
        scale1 = 1.0 + mod_ref[bidx, 3 * sub + 1:3 * sub + 2, :]

        def body(r, carry):
            rows = pl.ds(pl.multiple_of(r * row_chunk, row_chunk), row_chunk)
            u_ref[rows, :] = (x_ref[rows, :] * scale1 + shift).astype(BF16)
            o_ref[rows, :] = jnp.zeros((row_chunk, o_ref.shape[1]), F32)
            return carry

        lax.fori_loop(0, n_chunks, body, 0)

    u = u_ref[...]
    a = jnp.dot(u, w1_ref[...].astype(BF16), preferred_element_type=F32)
    b = jnp.dot(u, w3_ref[...].astype(BF16), preferred_element_type=F32)
    h = (_silu(a) * b).astype(BF16)
    o_ref[...] += jnp.dot(h, w2_ref[...].astype(BF16), preferred_element_type=F32)

    @pl.when(j == nj - 1)
    def _():
        gate = FFN_RES_WEIGHT * (1.0 + mod_ref[bidx, 3 * sub + 2:3 * sub + 3, :])
        g = g_ref[...]
        b_ln = b_ref[...]

        def body(r, carry):
            rows = pl.ds(pl.multiple_of(r * row_chunk, row_chunk), row_chunk)
            z = alpha * x_ref[rows, :] + gate * o_ref[rows, :]
            o_ref[rows, :] = _layer_norm_rows(z, g, b_ln)
            return carry

        lax.fori_loop(0, n_chunks, body, 0)


def _ffn(x2d, mod, w1, w3, w2, ln_g, ln_b, *, sub, seq, alpha, tm=1024, tf=256, row_chunk=64):
    n_tok, d = x2d.shape
    f = w1.shape[1]
    tm = min(tm, seq)
    kern = functools.partial(_ffn_kernel, sub=sub, tiles_per_batch=seq // tm, alpha=alpha,
                             row_chunk=row_chunk)
    return pl.pallas_call(
        kern,
        grid=(n_tok // tm, f // tf),
        in_specs=[
            pl.BlockSpec((tm, d), lambda i, j: (i, 0)),
            pl.BlockSpec(mod.shape, lambda i, j: (0, 0, 0)),
            pl.BlockSpec((d, tf), lambda i, j: (0, j)),
            pl.BlockSpec((d, tf), lambda i, j: (0, j)),
            pl.BlockSpec((tf, d), lambda i, j: (j, 0)),
            pl.BlockSpec((1, d), lambda i, j: (0, 0)),
            pl.BlockSpec((1, d), lambda i, j: (0, 0)),
        ],
        out_specs=pl.BlockSpec((tm, d), lambda i, j: (i, 0)),
        out_shape=jax.ShapeDtypeStruct((n_tok, d), F32),
        scratch_shapes=[pltpu.VMEM((tm, d), BF16)],
        compiler_params=_params(("arbitrary", "arbitrary")),
        name=f"ffn{sub}",
    )(x2d, mod, w1, w3, w2, ln_g.reshape(1, d), ln_b.reshape(1, d))


def _inproj_kernel(x_ref, mod_ref, wa_ref, wb_ref, oa_ref, ob_ref, u_ref, *,
                   sub, tiles_per_batch, n_a, row_chunk):
    i = pl.program_id(0)
    j = pl.program_id(1)
    bidx = i // tiles_per_batch

    @pl.when(j == 0)
    def _():
        shift = mod_ref[bidx, 3 * sub:3 * sub + 1, :]
        scale1 = 1.0 + mod_ref[bidx, 3 * sub + 1:3 * sub + 2, :]

        def body(r, carry):
            rows = pl.ds(pl.multiple_of(r * row_chunk, row_chunk), row_chunk)
            u_ref[rows, :] = (x_ref[rows, :] * scale1 + shift).astype(BF16)
            return carry

        lax.fori_loop(0, x_ref.shape[0] // row_chunk, body, 0)

    @pl.when(j < n_a)
    def _():
        oa_ref[...] = jnp.dot(u_ref[...], wa_ref[...].astype(BF16),
                              preferred_element_type=F32).astype(BF16)

    @pl.when(j >= n_a)
    def _():
        ob_ref[...] = jnp.dot(u_ref[...], wb_ref[...], preferred_element_type=F32)


def _inproj(x2d, mod, w_in, w_tail, *, sub, seq, tm=1024, tn=512, row_chunk=64):
    n_tok, d = x2d.shape
    n_a = PA_COLS // tn
    n_b = PB_COLS // tn
    kern = functools.partial(_inproj_kernel, sub=sub, tiles_per_batch=seq // tm, n_a=n_a,
                             row_chunk=row_chunk)
    return pl.pallas_call(
        kern,
        grid=(n_tok // tm, n_a + n_b),
        in_specs=[
            pl.BlockSpec((tm, d), lambda i, j: (i, 0)),
            pl.BlockSpec(mod.shape, lambda i, j: (0, 0, 0)),
            pl.BlockSpec((d, tn), lambda i, j: (0, jnp.minimum(j, n_a - 1))),
            pl.BlockSpec((d, tn), lambda i, j: (0, jnp.maximum(j - n_a, 0))),
        ],
        out_specs=[
            pl.BlockSpec((tm, tn), lambda i, j: (i, jnp.minimum(j, n_a - 1))),
            pl.BlockSpec((tm, tn), lambda i, j: (i, jnp.maximum(j - n_a, 0))),
        ],
        out_shape=[
            jax.ShapeDtypeStruct((n_tok, PA_COLS), BF16),
            jax.ShapeDtypeStruct((n_tok, PB_COLS), F32),
        ],
        scratch_shapes=[pltpu.VMEM((tm, d), BF16)],
        compiler_params=_params(("arbitrary", "arbitrary")),
        name="inproj",
    )(x2d, mod, w_in, w_tail)


def _inproj_tail_weights(w_in):
    d = w_in.shape[0]
    sizes = (ML_HEADS, ML_HEADS, Q_LORA_RANK, KV_LORA_RANK, MLA_ROPE_DIM)
    offs = [PA_COLS]
    for s in sizes:
        offs.append(offs[-1] + s)
    piece = lambda k: w_in[:, offs[k]:offs[k + 1]]
    zeros = lambda n: jnp.zeros((d, n), w_in.dtype)
    cols = [piece(2), piece(3), piece(4), zeros(LANES - MLA_ROPE_DIM),
            piece(0), piece(1), zeros(LANES - 2 * ML_HEADS)]
    return jnp.concatenate(cols, axis=1).astype(BF16)


def _lane_scan(v, op, chunk):
    lane = lax.broadcasted_iota(jnp.int32, v.shape, 1)
    ident = 0.0 if op is jnp.add else NEG_BIG
    s = 1
    while s < chunk:
        shifted = pltpu.roll(v, s, axis=1)
        v = op(v, jnp.where(lane >= s, shifted, ident))
        s *= 2
    return v


def _mlstm_kernel(qk_ref, v_ref, og_ref, gates_ref, cw_ref, cb_ref, gb_ref, ng_ref, o_ref,
                  c_ref, m_ref, halo_ref, *, chunk):
    t = pl.program_id(1)

    @pl.when(t == 0)
    def _():
        c_ref[...] = jnp.zeros_like(c_ref)
        m_ref[...] = jnp.zeros_like(m_ref)
        halo_ref[...] = jnp.zeros_like(halo_ref)

    for bi in range(qk_ref.shape[0]):
        _mlstm_row(bi, qk_ref, v_ref, og_ref, gates_ref, cw_ref, cb_ref, gb_ref, ng_ref, o_ref,
                   c_ref, m_ref, halo_ref, chunk)


def _mlstm_row(bi, qk_ref, v_ref, og_ref, gates_ref, cw_ref, cb_ref, gb_ref, ng_ref, o_ref,
               c_ref, m_ref, halo_ref, chunk):
    hq = ML_HEADS * ML_QK_DIM

    qk_pre = qk_ref[bi].astype(F32)
    ext = jnp.concatenate([halo_ref[bi], qk_pre], axis=0)
    halo_ref[bi] = qk_pre[chunk - SUBLANES:, :]
    conv = cb_ref[...]
    for tap in range(CONV_WIDTH):
        lo = SUBLANES - (CONV_WIDTH - 1) + tap
        conv = conv + cw_ref[tap:tap + 1, :] * ext[lo:lo + chunk, :]
    qk = _silu(conv)

    z = gates_ref[bi].T[0:SUBLANES, :] + gb_ref[:, 0:1]
    logi = pltpu.roll(z, ML_HEADS, axis=0)
    logf = jnp.minimum(z, 0.0) - jnp.log1p(jnp.exp(-jnp.abs(z)))
    bcum = _lane_scan(logf, jnp.add, chunk)
    r = logi - bcum
    cm = _lane_scan(r, jnp.maximum, chunk)
    m_in = m_ref[bi, :, 0:1]
    m_all = jnp.maximum(m_in, cm)
    m_last = m_all[:, chunk - 1:chunk]
    m_ref[bi] = jnp.broadcast_to(bcum[:, chunk - 1:chunk] + m_last, (SUBLANES, LANES))
    s_inter = jnp.exp(m_in - m_all)
    e_neg_m = jnp.exp(-(bcum + m_all))
    w_last = jnp.exp(r - m_last)
    s_old = jnp.exp(m_in - m_last)

    stack = jnp.concatenate(
        [m_all, s_inter, e_neg_m, w_last, jnp.zeros((LANES - 4 * SUBLANES, chunk), F32)], axis=0)
    cols = stack.T

    row_t = lax.broadcasted_iota(jnp.int32, (chunk, chunk), 0)
    col_j = lax.broadcasted_iota(jnp.int32, (chunk, chunk), 1)
    causal = col_j <= row_t
    ones_col = (lax.broadcasted_iota(jnp.int32, (chunk, LANES), 1) == 0).astype(BF16)

    for h in range(ML_HEADS):
        row = ML_HEADS + h
        q_h = qk[:, h * ML_QK_DIM:(h + 1) * ML_QK_DIM].astype(BF16)
        k_f = qk[:, hq + h * ML_QK_DIM:hq + (h + 1) * ML_QK_DIM] * (ML_QK_DIM ** -0.5)
        v_aug = jnp.concatenate([v_ref[bi, :, h * ML_V_DIM:(h + 1) * ML_V_DIM], ones_col], axis=1)
        m_col = cols[:, row:row + 1]
        s_col = cols[:, SUBLANES + row:SUBLANES + row + 1]
        e_col = cols[:, 2 * SUBLANES + row:2 * SUBLANES + row + 1]
        w_col = cols[:, 3 * SUBLANES + row:3 * SUBLANES + row + 1]

        decay = jnp.exp(jnp.where(causal, r[row:row + 1, :] - m_col, NEG_BIG))
        scores = lax.dot_general(q_h, k_f.astype(BF16), (((1,), (1,)), ((), ())),
                                 preferred_element_type=F32)
        p = (decay * scores).astype(BF16)
        c_old = c_ref[bi, h]
        tot = (jnp.dot(p, v_aug, preferred_element_type=F32)
               + s_col * jnp.dot(q_h, c_old.astype(BF16), preferred_element_type=F32))
        num = tot[:, :ML_V_DIM]
        den = tot[:, ML_V_DIM:ML_V_DIM + 1]
        hid = num / jnp.maximum(jnp.abs(den), e_col)

        k_w = (k_f * w_col).astype(BF16)
        c_ref[bi, h] = s_old[row:row + 1, :] * c_old + lax.dot_general(
            k_w, v_aug, (((0,), (0,)), ((), ())), preferred_element_type=F32)

        mu = jnp.mean(hid, axis=-1, keepdims=True)
        hc = hid - mu
        var = jnp.mean(hc * hc, axis=-1, keepdims=True)
        hn = hc * lax.rsqrt(var + LN_EPS) * ng_ref[:, h * ML_V_DIM:(h + 1) * ML_V_DIM]
        gate = jax.nn.sigmoid(og_ref[bi, :, h * ML_V_DIM:(h + 1) * ML_V_DIM].astype(F32))
        o_ref[bi, :, h * ML_V_DIM:(h + 1) * ML_V_DIM] = (hn * gate).astype(o_ref.dtype)


def _mlstm(pa, pb, conv_w, conv_b, b_igate, b_fgate, norm_g, *, chunk=128, rows_per_step=2):
    bsz, seq, _ = pa.shape
    rb = rows_per_step
    gate_bias = jnp.broadcast_to(jnp.concatenate([b_igate, b_fgate])[:, None], (SUBLANES, LANES))
    kern = functools.partial(_mlstm_kernel, chunk=chunk)
    return pl.pallas_call(
        kern,
        grid=(bsz // rb, seq // chunk),
        in_specs=[
            pl.BlockSpec((rb, chunk, ML_QK_COLS), lambda b, t: (b, t, 0)),
            pl.BlockSpec((rb, chunk, ML_V_COLS), lambda b, t: (b, t, ML_QK_COLS // ML_V_COLS)),
            pl.BlockSpec((rb, chunk, ML_V_COLS), lambda b, t: (b, t, ML_QK_COLS // ML_V_COLS + 1)),
            pl.BlockSpec((rb, chunk, LANES), lambda b, t: (b, t, PB_GATES // LANES)),
            pl.BlockSpec((CONV_WIDTH, ML_QK_COLS), lambda b, t: (0, 0)),
            pl.BlockSpec((1, ML_QK_COLS), lambda b, t: (0, 0)),
            pl.BlockSpec((SUBLANES, LANES), lambda b, t: (0, 0)),
            pl.BlockSpec((1, ML_V_COLS), lambda b, t: (0, 0)),
        ],
        out_specs=pl.BlockSpec((rb, chunk, ML_V_COLS), lambda b, t: (b, t, 0)),
        out_shape=jax.ShapeDtypeStruct((bsz, seq, ML_V_COLS), BF16),
        scratch_shapes=[
            pltpu.VMEM((rb, ML_HEADS, ML_QK_DIM, ML_V_DIM + LANES), F32),
            pltpu.VMEM((rb, SUBLANES, LANES), F32),
            pltpu.VMEM((rb, SUBLANES, ML_QK_COLS), F32),
        ],
        compiler_params=_params(("arbitrary", "arbitrary")),
        name="mlstm",
    )(pa, pa, pa, pb, conv_w, conv_b.reshape(1, -1), gate_bias, norm_g.reshape(1, -1))


def _rope_lanes(x, cos, sin_lo, sin_hi):
    half = MLA_ROPE_DIM // 2
    return x * cos + pltpu.roll(x, LANES - half, axis=1) * sin_lo + pltpu.roll(x, half, axis=1) * sin_hi


def _rms_rows(x, g):
    return x * lax.rsqrt(jnp.mean(x * x, axis=-1, keepdims=True) + LN_EPS) * g


def _mla_prep_kernel(cq_ref, ckv_ref, kr_ref, qg_ref, kvg_ref, wq_ref, wkv_ref, cos_ref, sl_ref, sh_ref,
                     q_ref, k_ref, v_ref, *, q_scale):
    cos = cos_ref[...]
    sin_lo = sl_ref[...]
    sin_hi = sh_ref[...]
    q_lat = _rms_rows(cq_ref[...], qg_ref[...]).astype(BF16)
    kv_lat = _rms_rows(ckv_ref[...], kvg_ref[...]).astype(BF16)
    k_rope = _rope_lanes(kr_ref[...], cos, sin_lo, sin_hi).astype(k_ref.dtype)
    for h in range(MLA_HEADS):
        lo = h * MLA_QK_GROUP
        q_h = jnp.dot(q_lat, wq_ref[:, lo:lo + MLA_QK_GROUP], preferred_element_type=F32) * q_scale
        q_ref[:, lo:lo + LANES] = q_h[:, :LANES].astype(q_ref.dtype)
        q_ref[:, lo + LANES:lo + 2 * LANES] = _rope_lanes(q_h[:, LANES:], cos, sin_lo, sin_hi).astype(q_ref.dtype)
        kv_h = jnp.dot(kv_lat, wkv_ref[:, lo:lo + MLA_QK_GROUP], preferred_element_type=F32)
        k_ref[:, lo:lo + LANES] = kv_h[:, :LANES].astype(k_ref.dtype)
        k_ref[:, lo + LANES:lo + 2 * LANES] = k_rope
        v_ref[:, h * MLA_V_DIM:(h + 1) * MLA_V_DIM] = kv_h[:, LANES:].astype(v_ref.dtype)


def _rope_tables(seq):
    half = MLA_ROPE_DIM // 2
    inv = ROPE_THETA ** (-jnp.arange(half, dtype=F32) / half)
    ang = jnp.arange(seq, dtype=F32)[:, None] * inv[None, :]
    cos, sin = jnp.cos(ang), jnp.sin(ang)
    zero = jnp.zeros_like(cos)
    cos_t = jnp.concatenate([cos, cos, zero, zero], axis=1)
    sin_lo = jnp.concatenate([-sin, zero, zero, zero], axis=1)
    sin_hi = jnp.concatenate([zero, sin, zero, zero], axis=1)
    return cos_t, sin_lo, sin_hi


def _mla_prep(pb2d, q_norm_g, kv_norm_g, w_uq, w_ukv, *, seq, tm=512):
    n_tok = pb2d.shape[0]
    dq = MLA_NOPE_DIM + MLA_ROPE_DIM
    wq = w_uq.reshape(Q_LORA_RANK, MLA_HEADS, dq)
    wq = jnp.pad(wq, ((0, 0), (0, 0), (0, MLA_QK_GROUP - dq))).reshape(Q_LORA_RANK, MLA_HEADS * MLA_QK_GROUP)
    cos_t, sin_lo, sin_hi = _rope_tables(seq)
    tiles_per_seq = seq // tm
    pos = lambda i: (i % tiles_per_seq, 0)
    kern = functools.partial(_mla_prep_kernel, q_scale=math.log2(math.e) / math.sqrt(dq))
    qk_cols = MLA_HEADS * MLA_QK_GROUP
    return pl.pallas_call(
        kern,
        grid=(n_tok // tm,),
        in_specs=[
            pl.BlockSpec((tm, Q_LORA_RANK), lambda i: (i, PB_CQ // Q_LORA_RANK)),
            pl.BlockSpec((tm, KV_LORA_RANK), lambda i: (i, PB_CKV // KV_LORA_RANK)),
            pl.BlockSpec((tm, LANES), lambda i: (i, PB_KR // LANES)),
            pl.BlockSpec((1, Q_LORA_RANK), lambda i: (0, 0)),
            pl.BlockSpec((1, KV_LORA_RANK), lambda i: (0, 0)),
            pl.BlockSpec((Q_LORA_RANK, qk_cols), lambda i: (0, 0)),
            pl.BlockSpec((KV_LORA_RANK, qk_cols), lambda i: (0, 0)),
            pl.BlockSpec((tm, LANES), pos),
            pl.BlockSpec((tm, LANES), pos),
            pl.BlockSpec((tm, LANES), pos),
        ],
        out_specs=[
            pl.BlockSpec((tm, qk_cols), lambda i: (i, 0)),
            pl.BlockSpec((tm, qk_cols), lambda i: (i, 0)),
            pl.BlockSpec((tm, MLA_HEADS * MLA_V_DIM), lambda i: (i, 0)),
        ],
        out_shape=[
            jax.ShapeDtypeStruct((n_tok, qk_cols), BF16),
            jax.ShapeDtypeStruct((n_tok, qk_cols), BF16),
            jax.ShapeDtypeStruct((n_tok, MLA_HEADS * MLA_V_DIM), BF16),
        ],
        compiler_params=_params(("arbitrary",)),
        name="mla_prep",
    )(pb2d, pb2d, pb2d, q_norm_g.reshape(1, -1), kv_norm_g.reshape(1, -1),
      wq.astype(BF16), w_ukv.astype(BF16), cos_t, sin_lo, sin_hi)


def _flash_kernel(q_ref, k_ref, v_ref, o_ref, m_ref, acc_ref, *, tq, heads):
    qi = pl.program_id(2)
    m_ref[...] = jnp.full_like(m_ref, NEG_BIG)
    acc_ref[...] = jnp.zeros_like(acc_ref)
    ones_col = (lax.broadcasted_iota(jnp.int32, (tq, LANES), 1) == 0).astype(BF16)
    on_or_below_diag = (lax.broadcasted_iota(jnp.int32, (tq, tq), 1)
                        <= lax.broadcasted_iota(jnp.int32, (tq, tq), 0))

    def key_block(j, masked):
        rows = pl.ds(pl.multiple_of(j * tq, tq), tq)
        for h in range(heads):
            q = q_ref[0, :, h * MLA_QK_GROUP:(h + 1) * MLA_QK_GROUP]
            k = k_ref[0, rows, h * MLA_QK_GROUP:(h + 1) * MLA_QK_GROUP]
            s = lax.dot_general(q, k, (((1,), (1,)), ((), ())), preferred_element_type=F32)
            if masked:
                s = jnp.where(on_or_below_diag, s, NEG_BIG)
            m_prev = m_ref[h]
            m_new = jnp.maximum(m_prev, jnp.max(s, axis=-1, keepdims=True))
            p = jnp.exp2(s - m_new[:, 0:1]).astype(BF16)
            alpha = jnp.exp2(m_prev - m_new)
            v_aug = jnp.concatenate([v_ref[0, rows, h * MLA_V_DIM:(h + 1) * MLA_V_DIM], ones_col], axis=1)
            pv = jnp.dot(p, v_aug, preferred_element_type=F32)
            acc_ref[h] = jnp.concatenate([alpha, alpha], axis=1) * acc_ref[h] + pv
            m_ref[h] = m_new

    def body(j, carry):
        key_block(j, False)
        return carry

    lax.fori_loop(0, qi, body, 0)
    key_block(qi, True)
    for h in range(heads):
        acc = acc_ref[h]
        o_ref[0, :, h * MLA_V_DIM:(h + 1) * MLA_V_DIM] = (
            acc[:, :MLA_V_DIM] / acc[:, MLA_V_DIM:MLA_V_DIM + 1]).astype(o_ref.dtype)


def _flash(q, k, v, *, tq=512, heads=4):
    bsz, seq, _ = q.shape
    kern = functools.partial(_flash_kernel, tq=tq, heads=heads)
    return pl.pallas_call(
        kern,
        grid=(bsz, MLA_HEADS // heads, seq // tq),
        in_specs=[
            pl.BlockSpec((1, tq, heads * MLA_QK_GROUP), lambda b, g, qi: (b, qi, g)),
            pl.BlockSpec((1, seq, heads * MLA_QK_GROUP), lambda b, g, qi: (b, 0, g)),
            pl.BlockSpec((1, seq, heads * MLA_V_DIM), lambda b, g, qi: (b, 0, g)),
        ],
        out_specs=pl.BlockSpec((1, tq, heads * MLA_V_DIM), lambda b, g, qi: (b, qi, g)),
        out_shape=jax.ShapeDtypeStruct((bsz, seq, MLA_HEADS * MLA_V_DIM), BF16),
        scratch_shapes=[
            pltpu.VMEM((heads, tq, LANES), F32),
            pltpu.VMEM((heads, tq, MLA_V_DIM + LANES), F32),
        ],
        compiler_params=_params(("arbitrary", "arbitrary", "arbitrary")),
        name="flash",
    )(q, k, v)


def _outproj_kernel(x_ref, hml_ref, hmla_ref, mod_ref, w_ref, g_ref, b_ref, o_ref, wb_ref, *,
                    sub, tiles_per_batch, alpha, row_chunk):
    i = pl.program_id(0)
    bidx = i // tiles_per_batch
    k_ml = hml_ref.shape[1]

    @pl.when(i == 0)
    def _():
        def body(r, carry):
            rows = pl.ds(pl.multiple_of(r * row_chunk, row_chunk), row_chunk)
            wb_ref[rows, :] = w_ref[rows, :].astype(BF16)
            return carry

        lax.fori_loop(0, w_ref.shape[0] // row_chunk, body, 0)

    y = (jnp.dot(hml_ref[...], wb_ref[0:k_ml, :], preferred_element_type=F32)
         + jnp.dot(hmla_ref[...], wb_ref[k_ml:, :], preferred_element_type=F32))
    gate = 1.0 + mod_ref[bidx, 3 * sub + 2:3 * sub + 3, :]
    z = alpha * x_ref[...] + gate * y
    o_ref[...] = _layer_norm_rows(z, g_ref[...], b_ref[...])


def _outproj(x2d, hml, hmla, mod, w_out, ln_g, ln_b, *, sub, seq, alpha, tm=512, row_chunk=128):
    n_tok, d = x2d.shape
    kern = functools.partial(_outproj_kernel, sub=sub, tiles_per_batch=seq // tm, alpha=alpha,
                             row_chunk=row_chunk)
    return pl.pallas_call(
        kern,
        grid=(n_tok // tm,),
        in_specs=[
            pl.BlockSpec((tm, d), lambda i: (i, 0)),
            pl.BlockSpec((tm, hml.shape[1]), lambda i: (i, 0)),
            pl.BlockSpec((tm, hmla.shape[1]), lambda i: (i, 0)),
            pl.BlockSpec(mod.shape, lambda i: (0, 0, 0)),
            pl.BlockSpec(w_out.shape, lambda i: (0, 0), pipeline_mode=pl.Buffered(1)),
            pl.BlockSpec((1, d), lambda i: (0, 0)),
            pl.BlockSpec((1, d), lambda i: (0, 0)),
        ],
        out_specs=pl.BlockSpec((tm, d), lambda i: (i, 0)),
        out_shape=jax.ShapeDtypeStruct((n_tok, d), F32),
        scratch_shapes=[pltpu.VMEM(w_out.shape, BF16)],
        compiler_params=_params(("arbitrary",)),
        name="outproj",
    )(x2d, hml, hmla, mod, w_out, ln_g.reshape(1, d), ln_b.reshape(1, d))


def kernel(x, c, w_ada, b_ada, ffn1_w1, ffn1_w3, ffn1_w2, ln1_g, ln1_b, w_in, conv_w, conv_b, b_igate, b_fgate, ml_norm_g, q_norm_g, w_uq, kv_norm_g, w_ukv, w_out, ln2_g, ln2_b, ffn2_w1, ffn2_w3, ffn2_w2, ln3_g, ln3_b):
    bsz, seq, d = x.shape
    depth = w_ada.shape[0]
    alpha = (2.0 * depth) ** 0.25
    h2d = x.reshape(bsz * seq, d)
    for l in range(depth):
        mod = _adaln(c, w_ada[l], b_ada[l]).reshape(bsz, N_SUBLAYERS * 3, d)
        h2d = _ffn(h2d, mod, ffn1_w1[l], ffn1_w3[l], ffn1_w2[l], ln1_g[l], ln1_b[l],
                   sub=0, seq=seq, alpha=alpha)
        pa, pb = _inproj(h2d, mod, w_in[l], _inproj_tail_weights(w_in[l]), sub=1, seq=seq)
        hml = _mlstm(pa.reshape(bsz, seq, PA_COLS), pb.reshape(bsz, seq, PB_COLS), conv_w[l], conv_b[l],
                     b_igate[l], b_fgate[l], ml_norm_g[l])
        q, k, v = _mla_prep(pb, q_norm_g[l], kv_norm_g[l], w_uq[l], w_ukv[l], seq=seq)
        hmla = _flash(q.reshape(bsz, seq, -1), k.reshape(bsz, seq, -1), v.reshape(bsz, seq, -1))
        h2d = _outproj(h2d, hml.reshape(bsz * seq, -1), hmla.reshape(bsz * seq, -1), mod,
                       w_out[l], ln2_g[l], ln2_b[l], sub=1, seq=seq, alpha=alpha)
        h2d = _ffn(h2d, mod, ffn2_w1[l], ffn2_w3[l], ffn2_w2[l], ln3_g[l], ln3_b[l],
                   sub=2, seq=seq, alpha=alpha)
    return h2d.reshape(bsz, seq, d)
```

```python
import functools
import math

import jax
import jax.numpy as jnp
from jax import lax
from jax.experimental import pallas as pl
from jax.experimental.pallas import tpu as pltpu

F32 = jnp.float32
BF16 = jnp.bfloat16

ML_HEADS = 4
ML_QK_DIM = 128
ML_V_DIM = 256
CONV_WIDTH = 4
MLA_HEADS = 8
MLA_NOPE_DIM = 128
MLA_ROPE_DIM = 64
MLA_V_DIM = 128
Q_LORA_RANK = 512
KV_LORA_RANK = 256
ROPE_THETA = 10000.0
FFN_RES_WEIGHT = 0.5
N_SUBLAYERS = 3
LN_EPS = 1e-5

LANES = 128
SUBLANES = 8
VMEM_LIMIT_BYTES = 56 * 1024 * 1024

ML_QK_COLS = 2 * ML_HEADS * ML_QK_DIM
ML_V_COLS = ML_HEADS * ML_V_DIM
PA_COLS = ML_QK_COLS + 2 * ML_V_COLS
PB_GATES = 0
PB_CQ = PB_GATES + 2 * ML_HEADS
PB_CKV = PB_CQ + Q_LORA_RANK
PB_KR = PB_CKV + KV_LORA_RANK
PB_VALID = PB_KR + MLA_ROPE_DIM
PB_COLS = -(-PB_VALID // 512) * 512
D_IN = PA_COLS + PB_VALID
MLA_QK_GROUP = 2 * LANES
NEG_BIG = -1e30


def _params(semantics):
    return pltpu.CompilerParams(dimension_semantics=semantics, vmem_limit_bytes=VMEM_LIMIT_BYTES)


def _silu(v):
    return v * jax.nn.sigmoid(v)


def _layer_norm_rows(z, g, b):
    mu = jnp.mean(z, axis=-1, keepdims=True)
    zc = z - mu
    var = jnp.mean(zc * zc, axis=-1, keepdims=True)
    return zc * lax.rsqrt(var + LN_EPS) * g + b


def _adaln_kernel(c_ref, w_ref, b_ref, o_ref):
    sc = _silu(c_ref[...]).astype(BF16)
    o_ref[...] = jnp.dot(sc, w_ref[...].astype(BF16), preferred_element_type=F32) + b_ref[...]


def _adaln(c, w, b, *, tn=1024):
    bsz, d = c.shape
    n = w.shape[1]
    rows = -(-bsz // SUBLANES) * SUBLANES
    c_pad = jnp.pad(c, ((0, rows - bsz), (0, 0)))
    out = pl.pallas_call(
        _adaln_kernel,
        grid=(n // tn,),
        in_specs=[
            pl.BlockSpec((rows, d), lambda j: (0, 0)),
            pl.BlockSpec((d, tn), lambda j: (0, j)),
            pl.BlockSpec((1, tn), lambda j: (0, j)),
        ],
        out_specs=pl.BlockSpec((rows, tn), lambda j: (0, j)),
        out_shape=jax.ShapeDtypeStruct((rows, n), F32),
        compiler_params=_params(("arbitrary",)),
        name="adaln",
    )(c_pad, w, b.reshape(1, n))
    return out[:bsz]


def _ffn_kernel(x_ref, mod_ref, w1_ref, w3_ref, w2_ref, g_ref, b_ref, o_ref, *,
                sub, tiles_per_batch, alpha, row_chunk):
    i = pl.program_id(0)
    j = pl.program_id(1)
    nj = pl.num_programs(1)
    bidx = i // tiles_per_batch
    tm = x_ref.shape[0]
    n_chunks = tm // row_chunk

    @pl.when(j == 0)
    def _():
        def body(r, carry):
            rows = pl.ds(pl.multiple_of(r * row_chunk, row_chunk), row_chunk)
            o_ref[rows, :] = jnp.zeros((row_chunk, o_ref.shape[1]), F32)
            return carry

        lax.fori_loop(0, n_chunks, body, 0)

    shift = mod_ref[bidx, 3 * sub:3 * sub + 1, :]
    scale1 = 1.0 + mod_ref[bidx, 3 * sub + 1:3 * sub + 2, :]
    u = (x_ref[...] * scale1 + shift).astype(BF16)
    a = jnp.dot(u, w1_ref[...].astype(BF16), preferred_element_type=F32)
    b = jnp.dot(u, w3_ref[...].astype(BF16), preferred_element_type=F32)
    h = (_silu(a) * b).astype(BF16)
    o_ref[...] += jnp.dot(h, w2_ref[...].astype(BF16), preferred_element_type=F32)

    @pl.when(j == nj - 1)
    def _():
        gate = FFN_RES_WEIGHT * (1.0 + mod_ref[bidx, 3 * sub + 2:3 * sub + 3, :])
        g = g_ref[...]
        b_ln = b_ref[...]

        def body(r, carry):
            rows = pl.ds(pl.multiple_of(r * row_chunk, row_chunk), row_chunk)
            z = alpha * x_ref[rows, :] + gate * o_ref[rows, :]
            o_ref[rows, :] = _layer_norm_rows(z, g, b_ln)
            return carry

        lax.fori_loop(0, n_chunks, body, 0)


def _ffn(x2d, mod, w1, w3, w2, ln_g, ln_b, *, sub, seq, alpha, tm=1024, tf=256, row_chunk=64):
    n_tok, d = x2d.shape
    f = w1.shape[1]
    tm = min(tm, seq)
    kern = functools.partial(_ffn_kernel, sub=sub, tiles_per_batch=seq // tm, alpha=alpha,
                             row_chunk=row_chunk)
    return pl.pallas_call(
        kern,
        grid=(n_tok // tm, f // tf),
        in_specs=[
            pl.BlockSpec((tm, d), lambda i, j: (i, 0)),
            pl.BlockSpec(mod.shape, lambda i, j: (0, 0, 0)),
            pl.BlockSpec((d, tf), lambda i, j: (0, j)),
            pl.BlockSpec((d, tf), lambda i, j: (0, j)),
            pl.BlockSpec((tf, d), lambda i, j: (j, 0)),
            pl.BlockSpec((1, d), lambda i, j: (0, 0)),
            pl.BlockSpec((1, d), lambda i, j: (0, 0)),
        ],
        out_specs=pl.BlockSpec((tm, d), lambda i, j: (i, 0)),
        out_shape=jax.ShapeDtypeStruct((n_tok, d), F32),
        compiler_params=_params(("arbitrary", "arbitrary")),
        name=f"ffn{sub}",
    )(x2d, mod, w1, w3, w2, ln_g.reshape(1, d), ln_b.reshape(1, d))


def _inproj_kernel(x_ref, mod_ref, w_ref, oa_ref, ob_ref, wbf_ref, *,
                   sub, tiles_per_batch, n_a, n_valid):
    i = pl.program_id(0)
    j = pl.program_id(1)
    bidx = i // tiles_per_batch
    tn = w_ref.shape[0]

    @pl.when(i == 0)
    def _():
        out_col = j * tn + lax.broadcasted_iota(jnp.int32, w_ref.shape, 0)
        wbf_ref[j] = jnp.where(out_col < n_valid, w_ref[...], 0.0).astype(BF16)

    def project():
        shift = mod_ref[bidx, 3 * sub:3 * sub + 1, :]
        scale1 = 1.0 + mod_ref[bidx, 3 * sub + 1:3 * sub + 2, :]
        u = (x_ref[...] * scale1 + shift).astype(BF16)
        return lax.dot_general(u, wbf_ref[j], (((1,), (1,)), ((), ())), preferred_element_type=F32)

    @pl.when(j < n_a)
    def _():
        oa_ref[...] = project().astype(BF16)

    @pl.when(j >= n_a)
    def _():
        ob_ref[...] = project()


def _inproj(x2d, mod, w_in_t, *, sub, seq, tm=1024, tn=512):
    n_tok, d = x2d.shape
    n_a = PA_COLS // tn
    n_b = PB_COLS // tn
    n_blocks = n_a + n_b
    kern = functools.partial(_inproj_kernel, sub=sub, tiles_per_batch=seq // tm, n_a=n_a,
                             n_valid=w_in_t.shape[0])
    return pl.pallas_call(
        kern,
        grid=(n_tok // tm, n_blocks),
        in_specs=[
            pl.BlockSpec((tm, d), lambda i, j: (i, 0)),
            pl.BlockSpec(mod.shape, lambda i, j: (0, 0, 0)),
            pl.BlockSpec((tn, d), lambda i, j: (jnp.where(i == 0, j, n_blocks - 1), 0)),
        ],
        out_specs=[
            pl.BlockSpec((tm, tn), lambda i, j: (i, jnp.minimum(j, n_a - 1))),
            pl.BlockSpec((tm, tn), lambda i, j: (i, jnp.maximum(j - n_a, 0))),
        ],
        out_shape=[
            jax.ShapeDtypeStruct((n_tok, PA_COLS), BF16),
            jax.ShapeDtypeStruct((n_tok, PB_COLS), F32),
        ],
        scratch_shapes=[pltpu.VMEM((n_blocks, tn, d), BF16)],
        compiler_params=_params(("arbitrary", "arbitrary")),
        name="inproj",
    )(x2d, mod, w_in_t)


def _lane_scan(v, op, chunk):
    lane = lax.broadcasted_iota(jnp.int32, v.shape, 1) % chunk
    ident = 0.0 if op is jnp.add else NEG_BIG
    s = 1
    while s < chunk:
        shifted = pltpu.roll(v, s, axis=1)
        v = op(v, jnp.where(lane >= s, shifted, ident))
        s *= 2
    return v


def _mlstm_gate_scans(bi, gates_ref, gb_ref, scan_ref, chunk):
    seq = gates_ref.shape[1]
    for c in range(seq // LANES):
        cols = slice(c * LANES, (c + 1) * LANES)
        scan_ref[bi, 0, :, cols] = gates_ref[bi, cols, :].T[0:SUBLANES, :]
    z = scan_ref[bi, 0] + gb_ref[:, 0:1]
    logi = pltpu.roll(z, ML_HEADS, axis=0)
    logf = jnp.minimum(z, 0.0) - jnp.log1p(jnp.exp(-jnp.abs(z)))
    bcum = _lane_scan(logf, jnp.add, chunk)
    r = logi - bcum
    scan_ref[bi, 0] = bcum
    scan_ref[bi, 1] = r
    scan_ref[bi, 2] = _lane_scan(r, jnp.maximum, chunk)


def _mlstm_kernel(qk_ref, v_ref, og_ref, gates_ref, cw_ref, cb_ref, gb_ref, ng_ref, o_ref,
                  c_ref, m_ref, halo_ref, scan_ref, *, chunk):
    t = pl.program_id(1)

    @pl.when(t == 0)
    def _():
        c_ref[...] = jnp.zeros_like(c_ref)
        m_ref[...] = jnp.zeros_like(m_ref)
        halo_ref[...] = jnp.zeros_like(halo_ref)
        for bi in range(qk_ref.shape[0]):
            _mlstm_gate_scans(bi, gates_ref, gb_ref, scan_ref, chunk)

    for bi in range(qk_ref.shape[0]):
        _mlstm_row(bi, t, qk_ref, v_ref, og_ref, cw_ref, cb_ref, ng_ref, o_ref,
                   c_ref, m_ref, halo_ref, scan_ref, chunk)


def _mlstm_row(bi, t, qk_ref, v_ref, og_ref, cw_ref, cb_ref, ng_ref, o_ref,
               c_ref, m_ref, halo_ref, scan_ref, chunk):
    hq = ML_HEADS * ML_QK_DIM

    qk_pre = qk_ref[bi].astype(F32)
    ext = jnp.concatenate([halo_ref[bi], qk_pre], axis=0)
    halo_ref[bi] = qk_pre[chunk - SUBLANES:, :]
    conv = cb_ref[...]
    for tap in range(CONV_WIDTH):
        lo = SUBLANES - (CONV_WIDTH - 1) + tap
        conv = conv + cw_ref[tap:tap + 1, :] * ext[lo:lo + chunk, :]
    qk = _silu(conv)

    lanes = pl.ds(pl.multiple_of(t * chunk, chunk), chunk)
    bcum = scan_ref[bi, 0, :, lanes]
    r = scan_ref[bi, 1, :, lanes]
    cm = scan_ref[bi, 2, :, lanes]
    m_in = m_ref[bi, :, 0:1]
    m_all = jnp.maximum(m_in, cm)
    m_last = m_all[:, chunk - 1:chunk]
    m_ref[bi] = jnp.broadcast_to(bcum[:, chunk - 1:chunk] + m_last, (SUBLANES, LANES))
    s_inter = jnp.exp(m_in - m_all)
    e_neg_m = jnp.exp(-(bcum + m_all))
    w_last = jnp.exp(r - m_last)
    s_old = jnp.exp(m_in - m_last)

    stack = jnp.concatenate(
        [m_all, s_inter, e_neg_m, w_last, jnp.zeros((LANES - 4 * SUBLANES, chunk), F32)], axis=0)
    cols = stack.T

    row_t = lax.broadcasted_iota(jnp.int32, (chunk, chunk), 0)
    col_j = lax.broadcasted_iota(jnp.int32, (chunk, chunk), 1)
    causal = col_j <= row_t
    ones_col = (lax.broadcasted_iota(jnp.int32, (chunk, LANES), 1) == 0).astype(BF16)

    for h in range(ML_HEADS):
        row = ML_HEADS + h
        q_h = qk[:, h * ML_QK_DIM:(h + 1) * ML_QK_DIM].astype(BF16)
        k_f = qk[:, hq + h * ML_QK_DIM:hq + (h + 1) * ML_QK_DIM] * (ML_QK_DIM ** -0.5)
        v_aug = jnp.concatenate([v_ref[bi, :, h * ML_V_DIM:(h + 1) * ML_V_DIM], ones_col], axis=1)
        m_col = cols[:, row:row + 1]
        s_col = cols[:, SUBLANES + row:SUBLANES + row + 1]
        e_col = cols[:, 2 * SUBLANES + row:2 * SUBLANES + row + 1]
        w_col = cols[:, 3 * SUBLANES + row:3 * SUBLANES + row + 1]

        decay = jnp.exp(jnp.where(causal, r[row:row + 1, :] - m_col, NEG_BIG))
        scores = lax.dot_general(q_h, k_f.astype(BF16), (((1,), (1,)), ((), ())),
                                 preferred_element_type=F32)
        p = (decay * scores).astype(BF16)
        c_old = c_ref[bi, h]
        tot = (jnp.dot(p, v_aug, preferred_element_type=F32)
               + s_col * jnp.dot(q_h, c_old.astype(BF16), preferred_element_type=F32))
        num = tot[:, :ML_V_DIM]
        den = tot[:, ML_V_DIM:ML_V_DIM + 1]
        hid = num / jnp.maximum(jnp.abs(den), e_col)

        k_w = (k_f * w_col).astype(BF16)
        c_ref[bi, h] = s_old[row:row + 1, :] * c_old + lax.dot_general(
            k_w, v_aug, (((0,), (0,)), ((), ())), preferred_element_type=F32)

        mu = jnp.mean(hid, axis=-1, keepdims=True)
        hc = hid - mu
        var = jnp.mean(hc * hc, axis=-1, keepdims=True)
        hn = hc * lax.rsqrt(var + LN_EPS) * ng_ref[:, h * ML_V_DIM:(h + 1) * ML_V_DIM]
        gate = jax.nn.sigmoid(og_ref[bi, :, h * ML_V_DIM:(h + 1) * ML_V_DIM].astype(F32))
        o_ref[bi, :, h * ML_V_DIM:(h + 1) * ML_V_DIM] = (hn * gate).astype(o_ref.dtype)


def _mlstm(pa, pb, conv_w, conv_b, b_igate, b_fgate, norm_g, *, chunk=128, rows_per_step=2):
    bsz, seq, _ = pa.shape
    rb = rows_per_step
    gate_bias = jnp.broadcast_to(jnp.concatenate([b_igate, b_fgate])[:, None], (SUBLANES, LANES))
    kern = functools.partial(_mlstm_kernel, chunk=chunk)
    return pl.pallas_call(
        kern,
        grid=(bsz // rb, seq // chunk),
        in_specs=[
            pl.BlockSpec((rb, chunk, ML_QK_COLS), lambda b, t: (b, t, 0)),
            pl.BlockSpec((rb, chunk, ML_V_COLS), lambda b, t: (b, t, ML_QK_COLS // ML_V_COLS)),
            pl.BlockSpec((rb, chunk, ML_V_COLS), lambda b, t: (b, t, ML_QK_COLS // ML_V_COLS + 1)),
            pl.BlockSpec((rb, seq, LANES), lambda b, t: (b, 0, PB_GATES // LANES)),
            pl.BlockSpec((CONV_WIDTH, ML_QK_COLS), lambda b, t: (0, 0)),
            pl.BlockSpec((1, ML_QK_COLS), lambda b, t: (0, 0)),
            pl.BlockSpec((SUBLANES, LANES), lambda b, t: (0, 0)),
            pl.BlockSpec((1, ML_V_COLS), lambda b, t: (0, 0)),
        ],
        out_specs=pl.BlockSpec((rb, chunk, ML_V_COLS), lambda b, t: (b, t, 0)),
        out_shape=jax.ShapeDtypeStruct((bsz, seq, ML_V_COLS), BF16),
        scratch_shapes=[
            pltpu.VMEM((rb, ML_HEADS, ML_QK_DIM, ML_V_DIM + LANES), F32),
            pltpu.VMEM((rb, SUBLANES, LANES), F32),
            pltpu.VMEM((rb, SUBLANES, ML_QK_COLS), F32),
            pltpu.VMEM((rb, 3, SUBLANES, seq), F32),
        ],
        compiler_params=_params(("arbitrary", "arbitrary")),
        name="mlstm",
    )(pa, pa, pa, pb, conv_w, conv_b.reshape(1, -1), gate_bias, norm_g.reshape(1, -1))


def _rope_lanes(x, cos, sin_lo, sin_hi):
    half = MLA_ROPE_DIM // 2
    return x * cos + pltpu.roll(x, LANES - half, axis=1) * sin_lo + pltpu.roll(x, half, axis=1) * sin_hi


def _rms_window(x, ind, g, n):
    ssq = jnp.sum(jnp.square(x * ind), axis=-1, keepdims=True)
    return x * lax.rsqrt(ssq * (1.0 / n) + LN_EPS) * g


def _mla_prep_kernel(xa_ref, xb_ref, qi_ref, qg_ref, kvi_ref, kvg_ref, wq_ref, wkv_ref,
                     cos_ref, sl_ref, sh_ref, q_ref, k_ref, v_ref, *, q_scale, kv_cols, kr_group, kr_shift):
    cos = cos_ref[...]
    sin_lo = sl_ref[...]
    sin_hi = sh_ref[...]
    q_lat = _rms_window(xa_ref[...], qi_ref[...], qg_ref[...], Q_LORA_RANK).astype(BF16)
    kv_lat = _rms_window(xb_ref[:, :kv_cols], kvi_ref[...], kvg_ref[...], KV_LORA_RANK).astype(BF16)
    k_r = pltpu.roll(xb_ref[:, kr_group * LANES:(kr_group + 1) * LANES], LANES - kr_shift, axis=1)
    k_rope = _rope_lanes(k_r, cos, sin_lo, sin_hi).astype(k_ref.dtype)
    for h in range(MLA_HEADS):
        lo = h * MLA_QK_GROUP
        q_h = jnp.dot(q_lat, wq_ref[:, lo:lo + MLA_QK_GROUP], preferred_element_type=F32) * q_scale
        q_ref[:, lo:lo + LANES] = q_h[:, :LANES].astype(q_ref.dtype)
        q_ref[:, lo + LANES:lo + 2 * LANES] = _rope_lanes(q_h[:, LANES:], cos, sin_lo, sin_hi).astype(q_ref.dtype)
        kv_h = jnp.dot(kv_lat, wkv_ref[:, lo:lo + MLA_QK_GROUP], preferred_element_type=F32)
        k_ref[:, lo:lo + LANES] = kv_h[:, :LANES].astype(k_ref.dtype)
        k_ref[:, lo + LANES:lo + 2 * LANES] = k_rope
        v_ref[:, h * MLA_V_DIM:(h + 1) * MLA_V_DIM] = kv_h[:, LANES:].astype(v_ref.dtype)


def _rope_tables(seq):
    half = MLA_ROPE_DIM // 2
    inv = ROPE_THETA ** (-jnp.arange(half, dtype=F32) / half)
    ang = jnp.arange(seq, dtype=F32)[:, None] * inv[None, :]
    cos, sin = jnp.cos(ang), jnp.sin(ang)
    zero = jnp.zeros_like(cos)
    cos_t = jnp.concatenate([cos, cos, zero, zero], axis=1)
    sin_lo = jnp.concatenate([-sin, zero, zero, zero], axis=1)
    sin_hi = jnp.concatenate([zero, sin, zero, zero], axis=1)
    return cos_t, sin_lo, sin_hi


def _window_row(values, lo, width):
    return jnp.pad(values.astype(F32), (lo, width - lo - values.shape[0])).reshape(1, width)


def _mla_prep(pb2d, q_norm_g, kv_norm_g, w_uq, w_ukv, *, seq, tm=512):
    n_tok = pb2d.shape[0]
    dq = MLA_NOPE_DIM + MLA_ROPE_DIM
    xb_width = 512
    xb_start = (PB_CKV // xb_width) * xb_width
    assert PB_VALID <= xb_start + xb_width and PB_COLS % xb_width == 0
    xa_cols = -(-PB_CKV // LANES) * LANES
    ckv_lo = PB_CKV - xb_start
    kv_cols = -(-(ckv_lo + KV_LORA_RANK) // LANES) * LANES
    kr_lo = PB_KR - xb_start
    assert kr_lo % LANES + MLA_ROPE_DIM <= LANES
    wq = w_uq.reshape(Q_LORA_RANK, MLA_HEADS, dq)
    wq = jnp.pad(wq, ((PB_CQ, xa_cols - PB_CKV), (0, 0), (0, MLA_QK_GROUP - dq)))
    wq = wq.reshape(xa_cols, MLA_HEADS * MLA_QK_GROUP).astype(BF16)
    wkv = jnp.pad(w_ukv, ((ckv_lo, kv_cols - ckv_lo - KV_LORA_RANK), (0, 0))).astype(BF16)
    q_ind = _window_row(jnp.ones((Q_LORA_RANK,), F32), PB_CQ, xa_cols)
    q_gain = _window_row(q_norm_g, PB_CQ, xa_cols)
    kv_ind = _window_row(jnp.ones((KV_LORA_RANK,), F32), ckv_lo, kv_cols)
    kv_gain = _window_row(kv_norm_g, ckv_lo, kv_cols)
    cos_t, sin_lo, sin_hi = _rope_tables(seq)
    tiles_per_seq = seq // tm
    pos = lambda i: (i % tiles_per_seq, 0)
    kern = functools.partial(_mla_prep_kernel, q_scale=math.log2(math.e) / math.sqrt(dq),
                             kv_cols=kv_cols, kr_group=kr_lo // LANES, kr_shift=kr_lo % LANES)
    qk_cols = MLA_HEADS * MLA_QK_GROUP
    const = lambda i: (0, 0)
    return pl.pallas_call(
        kern,
        grid=(n_tok // tm,),
        in_specs=[
            pl.BlockSpec((tm, xa_cols), lambda i: (i, 0)),
            pl.BlockSpec((tm, xb_width), lambda i: (i, xb_start // xb_width)),
            pl.BlockSpec((1, xa_cols), const),
            pl.BlockSpec((1, xa_cols), const),
            pl.BlockSpec((1, kv_cols), const),
            pl.BlockSpec((1, kv_cols), const),
            pl.BlockSpec((xa_cols, qk_cols), const),
            pl.BlockSpec((kv_cols, qk_cols), const),
            pl.BlockSpec((tm, LANES), pos),
            pl.BlockSpec((tm, LANES), pos),
            pl.BlockSpec((tm, LANES), pos),
        ],
        out_specs=[
            pl.BlockSpec((tm, qk_cols), lambda i: (i, 0)),
            pl.BlockSpec((tm, qk_cols), lambda i: (i, 0)),
            pl.BlockSpec((tm, MLA_HEADS * MLA_V_DIM), lambda i: (i, 0)),
        ],
        out_shape=[
            jax.ShapeDtypeStruct((n_tok, qk_cols), BF16),
            jax.ShapeDtypeStruct((n_tok, qk_cols), BF16),
            jax.ShapeDtypeStruct((n_tok, MLA_HEADS * MLA_V_DIM), BF16),
        ],
        compiler_params=_params(("arbitrary",)),
        name="mla_prep",
    )(pb2d, pb2d, q_ind, q_gain, kv_ind, kv_gain, wq, wkv, cos_t, sin_lo, sin_hi)


def _flash_kernel(q_ref, k_ref, v_ref, o_ref, m_ref, acc_ref, *, tq, heads):
    qi = pl.program_id(2)
    m_ref[...] = jnp.full_like(m_ref, NEG_BIG)
    acc_ref[...] = jnp.zeros_like(acc_ref)
    ones_col = (lax.broadcasted_iota(jnp.int32, (tq, LANES), 1) == 0).astype(BF16)
    on_or_below_diag = (lax.broadcasted_iota(jnp.int32, (tq, tq), 1)
                        <= lax.broadcasted_iota(jnp.int32, (tq, tq), 0))

    def key_block(j, masked):
        rows = pl.ds(pl.multiple_of(j * tq, tq), tq)
        for h in range(heads):
            q = q_ref[0, :, h * MLA_QK_GROUP:(h + 1) * MLA_QK_GROUP]
            k = k_ref[0, rows, h * MLA_QK_GROUP:(h + 1) * MLA_QK_GROUP]
            s = lax.dot_general(q, k, (((1,), (1,)), ((), ())), preferred_element_type=F32)
            if masked:
                s = jnp.where(on_or_below_diag, s, NEG_BIG)
            m_prev = m_ref[h]
            m_new = jnp.maximum(m_prev, jnp.max(s, axis=-1, keepdims=True))
            p = jnp.exp2(s - m_new[:, 0:1]).astype(BF16)
            alpha = jnp.exp2(m_prev - m_new)
            v_aug = jnp.concatenate([v_ref[0, rows, h * MLA_V_DIM:(h + 1) * MLA_V_DIM], ones_col], axis=1)
            pv = jnp.dot(p, v_aug, preferred_element_type=F32)
            acc_ref[h] = jnp.concatenate([alpha, alpha], axis=1) * acc_ref[h] + pv
            m_ref[h] = m_new

    def body(j, carry):
        key_block(j, False)
        return carry

    lax.fori_loop(0, qi, body, 0)
    key_block(qi, True)
    for h in range(heads):
        acc = acc_ref[h]
        o_ref[0, :, h * MLA_V_DIM:(h + 1) * MLA_V_DIM] = (
            acc[:, :MLA_V_DIM] / acc[:, MLA_V_DIM:MLA_V_DIM + 1]).astype(o_ref.dtype)


def _flash(q, k, v, *, tq=512, heads=4):
    bsz, seq, _ = q.shape
    kern = functools.partial(_flash_kernel, tq=tq, heads=heads)
    return pl.pallas_call(
        kern,
        grid=(bsz, MLA_HEADS // heads, seq // tq),
        in_specs=[
            pl.BlockSpec((1, tq, heads * MLA_QK_GROUP), lambda b, g, qi: (b, qi, g)),
            pl.BlockSpec((1, seq, heads * MLA_QK_GROUP), lambda b, g, qi: (b, 0, g)),
            pl.BlockSpec((1, seq, heads * MLA_V_DIM), lambda b, g, qi: (b, 0, g)),
        ],
        out_specs=pl.BlockSpec((1, tq, heads * MLA_V_DIM), lambda b, g, qi: (b, qi, g)),
        out_shape=jax.ShapeDtypeStruct((bsz, seq, MLA_HEADS * MLA_V_DIM), BF16),
        scratch_shapes=[
            pltpu.VMEM((heads, tq, LANES), F32),
            pltpu.VMEM((heads, tq, MLA_V_DIM + LANES), F32),
        ],
        compiler_params=_params(("arbitrary", "arbitrary", "arbitrary")),
        name="flash",
    )(q, k, v)


def _outproj_kernel(x_ref, hml_ref, hmla_ref, mod_ref, w_ref, g_ref, b_ref, o_ref, wb_ref, *,
                    sub, tiles_per_batch, alpha, row_chunk):
    i = pl.program_id(0)
    bidx = i // tiles_per_batch
    k_ml = hml_ref.shape[1]

    @pl.when(i == 0)
    def _():
        def body(r, carry):
            rows = pl.ds(pl.multiple_of(r * row_chunk, row_chunk), row_chunk)
            wb_ref[rows, :] = w_ref[rows, :].astype(BF16)
            return carry

        lax.fori_loop(0, w_ref.shape[0] // row_chunk, body, 0)

    y = (jnp.dot(hml_ref[...], wb_ref[0:k_ml, :], preferred_element_type=F32)
         + jnp.dot(hmla_ref[...], wb_ref[k_ml:, :], preferred_element_type=F32))
    gate = 1.0 + mod_ref[bidx, 3 * sub + 2:3 * sub + 3, :]
    z = alpha * x_ref[...] + gate * y
    o_ref[...] = _layer_norm_rows(z, g_ref[...], b_ref[...])


def _outproj(x2d, hml, hmla, mod, w_out, ln_g, ln_b, *, sub, seq, alpha, tm=512, row_chunk=128):
    n_tok, d = x2d.shape
    kern = functools.partial(_outproj_kernel, sub=sub, tiles_per_batch=seq // tm, alpha=alpha,
                             row_chunk=row_chunk)
    return pl.pallas_call(
        kern,
        grid=(n_tok // tm,),
        in_specs=[
            pl.BlockSpec((tm, d), lambda i: (i, 0)),
            pl.BlockSpec((tm, hml.shape[1]), lambda i: (i, 0)),
            pl.BlockSpec((tm, hmla.shape[1]), lambda i: (i, 0)),
            pl.BlockSpec(mod.shape, lambda i: (0, 0, 0)),
            pl.BlockSpec(w_out.shape, lambda i: (0, 0), pipeline_mode=pl.Buffered(1)),
            pl.BlockSpec((1, d), lambda i: (0, 0)),
            pl.BlockSpec((1, d), lambda i: (0, 0)),
        ],
        out_specs=pl.BlockSpec((tm, d), lambda i: (i, 0)),
        out_shape=jax.ShapeDtypeStruct((n_tok, d), F32),
        scratch_shapes=[pltpu.VMEM(w_out.shape, BF16)],
        compiler_params=_params(("arbitrary",)),
        name="outproj",
    )(x2d, hml, hmla, mod, w_out, ln_g.reshape(1, d), ln_b.reshape(1, d))


def kernel(x, c, w_ada, b_ada, ffn1_w1, ffn1_w3, ffn1_w2, ln1_g, ln1_b, w_in, conv_w, conv_b, b_igate, b_fgate, ml_norm_g, q_norm_g, w_uq, kv_norm_g, w_ukv, w_out, ln2_g, ln2_b, ffn2_w1, ffn2_w3, ffn2_w2, ln3_g, ln3_b):
    bsz, seq, d = x.shape
    depth = w_ada.shape[0]
    alpha = (2.0 * depth) ** 0.25
    h2d = x.reshape(bsz * seq, d)
    for l in range(depth):
        mod = _adaln(c, w_ada[l], b_ada[l]).reshape(bsz, N_SUBLAYERS * 3, d)
        h2d = _ffn(h2d, mod, ffn1_w1[l], ffn1_w3[l], ffn1_w2[l], ln1_g[l], ln1_b[l],
                   sub=0, seq=seq, alpha=alpha)
        assert w_in.shape[2] == D_IN
        pa, pb = _inproj(h2d, mod, w_in[l].T, sub=1, seq=seq)
        hml = _mlstm(pa.reshape(bsz, seq, PA_COLS), pb.reshape(bsz, seq, PB_COLS), conv_w[l], conv_b[l],
                     b_igate[l], b_fgate[l], ml_norm_g[l])
        q, k, v = _mla_prep(pb, q_norm_g[l], kv_norm_g[l], w_uq[l], w_ukv[l], seq=seq)
        hmla = _flash(q.reshape(bsz, seq, -1), k.reshape(bsz, seq, -1), v.reshape(bsz, seq, -1))
        h2d = _outproj(h2d, hml.reshape(bsz * seq, -1), hmla.reshape(bsz * seq, -1), mod,
                       w_out[l], ln2_g[l], ln2_b[l], sub=1, seq=seq, alpha=alpha)
        h2d = _ffn(h2d, mod, ffn2_w1[l], ffn2_w3[l], ffn2_w2[l], ln3_g[l], ln3_b[l],
                   sub=2, seq=seq, alpha=alpha)
    return h2d.reshape(bsz, seq, d)
```

```python
import functools
import math

import jax
import jax.numpy as jnp
from jax import lax
from jax.experimental import pallas as pl
from jax.experimental.pallas import tpu as pltpu

F32 = jnp.float32
BF16 = jnp.bfloat16

ML_HEADS = 4
ML_QK_DIM = 128
ML_V_DIM = 256
CONV_WIDTH = 4
MLA_HEADS = 8
MLA_NOPE_DIM = 128
MLA_ROPE_DIM = 64
MLA_V_DIM = 128
Q_LORA_RANK = 512
KV_LORA_RANK = 256
ROPE_THETA = 10000.0
FFN_RES_WEIGHT = 0.5
N_SUBLAYERS = 3
LN_EPS = 1e-5

LANES = 128
SUBLANES = 8
VMEM_LIMIT_BYTES = 56 * 1024 * 1024

ML_QK_COLS = 2 * ML_HEADS * ML_QK_DIM
ML_V_COLS = ML_HEADS * ML_V_DIM
PA_COLS = ML_QK_COLS + 2 * ML_V_COLS
PB_GATES = 0
PB_CQ = PB_GATES + 2 * ML_HEADS
PB_CKV = PB_CQ + Q_LORA_RANK
PB_KR = PB_CKV + KV_LORA_RANK
PB_VALID = PB_KR + MLA_ROPE_DIM
PB_COLS = -(-PB_VALID // 512) * 512
D_IN = PA_COLS + PB_VALID
MLA_QK_GROUP = 2 * LANES
NEG_BIG = -1e30


def _params(semantics):
    return pltpu.CompilerParams(dimension_semantics=semantics, vmem_limit_bytes=VMEM_LIMIT_BYTES)


def _silu(v):
    return v * jax.nn.sigmoid(v)


def _layer_norm_rows(z, g, b):
    mu = jnp.mean(z, axis=-1, keepdims=True)
    zc = z - mu
    var = jnp.mean(zc * zc, axis=-1, keepdims=True)
    return zc * lax.rsqrt(var + LN_EPS) * g + b


def _adaln_kernel(c_ref, w_ref, b_ref, o_ref):
    sc = _silu(c_ref[...]).astype(BF16)
    o_ref[...] = jnp.dot(sc, w_ref[...].astype(BF16), preferred_element_type=F32) + b_ref[...]


def _adaln(c, w, b, *, tn=1024):
    bsz, d = c.shape
    n = w.shape[1]
    rows = -(-bsz // SUBLANES) * SUBLANES
    c_pad = jnp.pad(c, ((0, rows - bsz), (0, 0)))
    out = pl.pallas_call(
        _adaln_kernel,
        grid=(n // tn,),
        in_specs=[
            pl.BlockSpec((rows, d), lambda j: (0, 0)),
            pl.BlockSpec((d, tn), lambda j: (0, j)),
            pl.BlockSpec((1, tn), lambda j: (0, j)),
        ],
        out_specs=pl.BlockSpec((rows, tn), lambda j: (0, j)),
        out_shape=jax.ShapeDtypeStruct((rows, n), F32),
        compiler_params=_params(("arbitrary",)),
        name="adaln",
    )(c_pad, w, b.reshape(1, n))
    return out[:bsz]


def _ffn_kernel(x_ref, mod_ref, g_ref, b_ref, w1_hbm, w3_hbm, w2_hbm, o_ref,
                w1_buf, w3_buf, w2_buf, sem, *, sub, tiles_per_batch, alpha, row_chunk, tf):
    i = pl.program_id(0)
    n_tiles = pl.num_programs(0)
    bidx = i // tiles_per_batch
    tm = x_ref.shape[0]
    n_row_chunks = tm // row_chunk
    n_f = w1_hbm.shape[1] // tf

    def chunk_copies(c, slot):
        cols = pl.ds(pl.multiple_of(c * tf, tf), tf)
        return (pltpu.make_async_copy(w1_hbm.at[:, cols], w1_buf.at[slot], sem.at[0, slot]),
                pltpu.make_async_copy(w3_hbm.at[:, cols], w3_buf.at[slot], sem.at[1, slot]),
                pltpu.make_async_copy(w2_hbm.at[cols, :], w2_buf.at[slot], sem.at[2, slot]))

    def start_chunk(c, slot):
        for cp in chunk_copies(c, slot):
            cp.start()

    def wait_chunk(c, slot):
        for cp in chunk_copies(c, slot):
            cp.wait()

    @pl.when(i == 0)
    def _():
        start_chunk(0, 0)

    def zero_rows(r, carry):
        rows = pl.ds(pl.multiple_of(r * row_chunk, row_chunk), row_chunk)
        o_ref[rows, :] = jnp.zeros((row_chunk, o_ref.shape[1]), F32)
        return carry

    lax.fori_loop(0, n_row_chunks, zero_rows, 0)

    shift = mod_ref[bidx, 3 * sub:3 * sub + 1, :]
    scale1 = 1.0 + mod_ref[bidx, 3 * sub + 1:3 * sub + 2, :]

    def accumulate(slot):
        u = (x_ref[...] * scale1 + shift).astype(BF16)
        a = jnp.dot(u, w1_buf[slot].astype(BF16), preferred_element_type=F32)
        b = jnp.dot(u, w3_buf[slot].astype(BF16), preferred_element_type=F32)
        h = (_silu(a) * b).astype(BF16)
        o_ref[...] += jnp.dot(h, w2_buf[slot].astype(BF16), preferred_element_type=F32)

    def chunk_pair(k, carry):
        c0 = 2 * k
        start_chunk(c0 + 1, 1)
        wait_chunk(c0, 0)
        accumulate(0)
        nxt = c0 + 2

        @pl.when(jnp.logical_or(nxt < n_f, i + 1 < n_tiles))
        def _():
            start_chunk(jnp.where(nxt < n_f, nxt, 0), 0)

        wait_chunk(c0 + 1, 1)
        accumulate(1)
        return carry

    lax.fori_loop(0, n_f // 2, chunk_pair, 0)

    gate = FFN_RES_WEIGHT * (1.0 + mod_ref[bidx, 3 * sub + 2:3 * sub + 3, :])
    g = g_ref[...]
    b_ln = b_ref[...]

    def norm_rows(r, carry):
        rows = pl.ds(pl.multiple_of(r * row_chunk, row_chunk), row_chunk)
        z = alpha * x_ref[rows, :] + gate * o_ref[rows, :]
        o_ref[rows, :] = _layer_norm_rows(z, g, b_ln)
        return carry

    lax.fori_loop(0, n_row_chunks, norm_rows, 0)


def _ffn(x2d, mod, w1, w3, w2, ln_g, ln_b, *, sub, seq, alpha, tm=1024, tf=256, row_chunk=64):
    n_tok, d = x2d.shape
    f = w1.shape[1]
    tm = min(tm, seq)
    assert (f // tf) % 2 == 0 and f % tf == 0
    kern = functools.partial(_ffn_kernel, sub=sub, tiles_per_batch=seq // tm, alpha=alpha,
                             row_chunk=row_chunk, tf=tf)
    hbm = pl.BlockSpec(memory_space=pl.ANY)
    return pl.pallas_call(
        kern,
        grid=(n_tok // tm,),
        in_specs=[
            pl.BlockSpec((tm, d), lambda i: (i, 0)),
            pl.BlockSpec(mod.shape, lambda i: (0, 0, 0)),
            pl.BlockSpec((1, d), lambda i: (0, 0)),
            pl.BlockSpec((1, d), lambda i: (0, 0)),
            hbm, hbm, hbm,
        ],
        out_specs=pl.BlockSpec((tm, d), lambda i: (i, 0)),
        out_shape=jax.ShapeDtypeStruct((n_tok, d), F32),
        scratch_shapes=[
            pltpu.VMEM((2, d, tf), w1.dtype),
            pltpu.VMEM((2, d, tf), w3.dtype),
            pltpu.VMEM((2, tf, d), w2.dtype),
            pltpu.SemaphoreType.DMA((3, 2)),
        ],
        compiler_params=_params(("arbitrary",)),
        name=f"ffn{sub}",
    )(x2d, mod, ln_g.reshape(1, d), ln_b.reshape(1, d), w1, w3, w2)


def _inproj_kernel(x_ref, mod_ref, w_ref, oa_ref, ob_ref, wbf_ref, *,
                   sub, tiles_per_batch, n_a, n_valid):
    i = pl.program_id(0)
    j = pl.program_id(1)
    bidx = i // tiles_per_batch
    tn = w_ref.shape[0]

    @pl.when(i == 0)
    def _():
        out_col = j * tn + lax.broadcasted_iota(jnp.int32, w_ref.shape, 0)
        wbf_ref[j] = jnp.where(out_col < n_valid, w_ref[...], 0.0).astype(BF16)

    def project():
        shift = mod_ref[bidx, 3 * sub:3 * sub + 1, :]
        scale1 = 1.0 + mod_ref[bidx, 3 * sub + 1:3 * sub + 2, :]
        u = (x_ref[...] * scale1 + shift).astype(BF16)
        return lax.dot_general(u, wbf_ref[j], (((1,), (1,)), ((), ())), preferred_element_type=F32)

    @pl.when(j < n_a)
    def _():
        oa_ref[...] = project().astype(BF16)

    @pl.when(j >= n_a)
    def _():
        ob_ref[...] = project()


def _inproj(x2d, mod, w_in_t, *, sub, seq, tm=1024, tn=512):
    n_tok, d = x2d.shape
    n_a = PA_COLS // tn
    n_b = PB_COLS // tn
    n_blocks = n_a + n_b
    kern = functools.partial(_inproj_kernel, sub=sub, tiles_per_batch=seq // tm, n_a=n_a,
                             n_valid=w_in_t.shape[0])
    return pl.pallas_call(
        kern,
        grid=(n_tok // tm, n_blocks),
        in_specs=[
            pl.BlockSpec((tm, d), lambda i, j: (i, 0)),
            pl.BlockSpec(mod.shape, lambda i, j: (0, 0, 0)),
            pl.BlockSpec((tn, d), lambda i, j: (jnp.where(i == 0, j, n_blocks - 1), 0)),
        ],
        out_specs=[
            pl.BlockSpec((tm, tn), lambda i, j: (i, jnp.minimum(j, n_a - 1))),
            pl.BlockSpec((tm, tn), lambda i, j: (i, jnp.maximum(j - n_a, 0))),
        ],
        out_shape=[
            jax.ShapeDtypeStruct((n_tok, PA_COLS), BF16),
            jax.ShapeDtypeStruct((n_tok, PB_COLS), F32),
        ],
        scratch_shapes=[pltpu.VMEM((n_blocks, tn, d), BF16)],
        compiler_params=_params(("arbitrary", "arbitrary")),
        name="inproj",
    )(x2d, mod, w_in_t)


def _lane_scan(v, op, chunk):
    lane = lax.broadcasted_iota(jnp.int32, v.shape, 1) % chunk
    ident = 0.0 if op is jnp.add else NEG_BIG
    s = 1
    while s < chunk:
        shifted = pltpu.roll(v, s, axis=1)
        v = op(v, jnp.where(lane >= s, shifted, ident))
        s *= 2
    return v


def _mlstm_gate_scans(bi, gates_ref, gb_ref, scan_ref, chunk):
    seq = gates_ref.shape[1]
    for c in range(seq // LANES):
        cols = slice(c * LANES, (c + 1) * LANES)
        scan_ref[bi, 0, :, cols] = gates_ref[bi, cols, :].T[0:SUBLANES, :]
    z = scan_ref[bi, 0] + gb_ref[:, 0:1]
    logi = pltpu.roll(z, ML_HEADS, axis=0)
    logf = jnp.minimum(z, 0.0) - jnp.log1p(jnp.exp(-jnp.abs(z)))
    bcum = _lane_scan(logf, jnp.add, chunk)
    r = logi - bcum
    scan_ref[bi, 0] = bcum
    scan_ref[bi, 1] = r
    scan_ref[bi, 2] = _lane_scan(r, jnp.maximum, chunk)


def _mlstm_kernel(qk_ref, v_ref, og_ref, gates_ref, cw_ref, cb_ref, gb_ref, ng_ref, o_ref,
                  c_ref, m_ref, halo_ref, scan_ref, *, chunk):
    t = pl.program_id(1)

    @pl.when(t == 0)
    def _():
        c_ref[...] = jnp.zeros_like(c_ref)
        m_ref[...] = jnp.zeros_like(m_ref)
        halo_ref[...] = jnp.zeros_like(halo_ref)
        for bi in range(qk_ref.shape[0]):
            _mlstm_gate_scans(bi, gates_ref, gb_ref, scan_ref, chunk)

    for bi in range(qk_ref.shape[0]):
        _mlstm_row(bi, t, qk_ref, v_ref, og_ref, cw_ref, cb_ref, ng_ref, o_ref,
                   c_ref, m_ref, halo_ref, scan_ref, chunk)


def _mlstm_row(bi, t, qk_ref, v_ref, og_ref, cw_ref, cb_ref, ng_ref, o_ref,
               c_ref, m_ref, halo_ref, scan_ref, chunk):
    hq = ML_HEADS * ML_QK_DIM

    qk_pre = qk_ref[bi].astype(F32)
    ext = jnp.concatenate([halo_ref[bi], qk_pre], axis=0)
    halo_ref[bi] = qk_pre[chunk - SUBLANES:, :]
    conv = cb_ref[...]
    for tap in range(CONV_WIDTH):
        lo = SUBLANES - (CONV_WIDTH - 1) + tap
        conv = conv + cw_ref[tap:tap + 1, :] * ext[lo:lo + chunk, :]
    qk = _silu(conv)

    lanes = pl.ds(pl.multiple_of(t * chunk, chunk), chunk)
    bcum = scan_ref[bi, 0, :, lanes]
    r = scan_ref[bi, 1, :, lanes]
    cm = scan_ref[bi, 2, :, lanes]
    m_in = m_ref[bi, :, 0:1]
    m_all = jnp.maximum(m_in, cm)
    m_last = m_all[:, chunk - 1:chunk]
    m_ref[bi] = jnp.broadcast_to(bcum[:, chunk - 1:chunk] + m_last, (SUBLANES, LANES))
    s_inter = jnp.exp(m_in - m_all)
    e_neg_m = jnp.exp(-(bcum + m_all))
    w_last = jnp.exp(r - m_last)
    s_old = jnp.exp(m_in - m_last)

    stack = jnp.concatenate(
        [m_all, s_inter, e_neg_m, w_last, jnp.zeros((LANES - 4 * SUBLANES, chunk), F32)], axis=0)
    cols = stack.T

    row_t = lax.broadcasted_iota(jnp.int32, (chunk, chunk), 0)
    col_j = lax.broadcasted_iota(jnp.int32, (chunk, chunk), 1)
    causal = col_j <= row_t
    ones_col = (lax.broadcasted_iota(jnp.int32, (chunk, LANES), 1) == 0).astype(BF16)

    for h in range(ML_HEADS):
        row = ML_HEADS + h
        q_h = qk[:, h * ML_QK_DIM:(h + 1) * ML_QK_DIM].astype(BF16)
        k_f = qk[:, hq + h * ML_QK_DIM:hq + (h + 1) * ML_QK_DIM] * (ML_QK_DIM ** -0.5)
        v_aug = jnp.concatenate([v_ref[bi, :, h * ML_V_DIM:(h + 1) * ML_V_DIM], ones_col], axis=1)
        m_col = cols[:, row:row + 1]
        s_col = cols[:, SUBLANES + row:SUBLANES + row + 1]
        e_col = cols[:, 2 * SUBLANES + row:2 * SUBLANES + row + 1]
        w_col = cols[:, 3 * SUBLANES + row:3 * SUBLANES + row + 1]

        decay = jnp.exp(jnp.where(causal, r[row:row + 1, :] - m_col, NEG_BIG))
        scores = lax.dot_general(q_h, k_f.astype(BF16), (((1,), (1,)), ((), ())),
                                 preferred_element_type=F32)
        p = (decay * scores).astype(BF16)
        c_old = c_ref[bi, h]
        tot = (jnp.dot(p, v_aug, preferred_element_type=F32)
               + s_col * jnp.dot(q_h, c_old.astype(BF16), preferred_element_type=F32))
        num = tot[:, :ML_V_DIM]
        den = tot[:, ML_V_DIM:ML_V_DIM + 1]
        hid = num / jnp.maximum(jnp.abs(den), e_col)

        k_w = (k_f * w_col).astype(BF16)
        c_ref[bi, h] = s_old[row:row + 1, :] * c_old + lax.dot_general(
            k_w, v_aug, (((0,), (0,)), ((), ())), preferred_element_type=F32)

        mu = jnp.mean(hid, axis=-1, keepdims=True)
        hc = hid - mu
        var = jnp.mean(hc * hc, axis=-1, keepdims=True)
        hn = hc * lax.rsqrt(var + LN_EPS) * ng_ref[:, h * ML_V_DIM:(h + 1) * ML_V_DIM]
        gate = jax.nn.sigmoid(og_ref[bi, :, h * ML_V_DIM:(h + 1) * ML_V_DIM].astype(F32))
        o_ref[bi, :, h * ML_V_DIM:(h + 1) * ML_V_DIM] = (hn * gate).astype(o_ref.dtype)


def _mlstm(pa, pb, conv_w, conv_b, b_igate, b_fgate, norm_g, *, chunk=128, rows_per_step=2):
    bsz, seq, _ = pa.shape
    rb = rows_per_step
    gate_bias = jnp.broadcast_to(jnp.concatenate([b_igate, b_fgate])[:, None], (SUBLANES, LANES))
    kern = functools.partial(_mlstm_kernel, chunk=chunk)
    return pl.pallas_call(
        kern,
        grid=(bsz // rb, seq // chunk),
        in_specs=[
            pl.BlockSpec((rb, chunk, ML_QK_COLS), lambda b, t: (b, t, 0)),
            pl.BlockSpec((rb, chunk, ML_V_COLS), lambda b, t: (b, t, ML_QK_COLS // ML_V_COLS)),
            pl.BlockSpec((rb, chunk, ML_V_COLS), lambda b, t: (b, t, ML_QK_COLS // ML_V_COLS + 1)),
            pl.BlockSpec((rb, seq, LANES), lambda b, t: (b, 0, PB_GATES // LANES)),
            pl.BlockSpec((CONV_WIDTH, ML_QK_COLS), lambda b, t: (0, 0)),
            pl.BlockSpec((1, ML_QK_COLS), lambda b, t: (0, 0)),
            pl.BlockSpec((SUBLANES, LANES), lambda b, t: (0, 0)),
            pl.BlockSpec((1, ML_V_COLS), lambda b, t: (0, 0)),
        ],
        out_specs=pl.BlockSpec((rb, chunk, ML_V_COLS), lambda b, t: (b, t, 0)),
        out_shape=jax.ShapeDtypeStruct((bsz, seq, ML_V_COLS), BF16),
        scratch_shapes=[
            pltpu.VMEM((rb, ML_HEADS, ML_QK_DIM, ML_V_DIM + LANES), F32),
            pltpu.VMEM((rb, SUBLANES, LANES), F32),
            pltpu.VMEM((rb, SUBLANES, ML_QK_COLS), F32),
            pltpu.VMEM((rb, 3, SUBLANES, seq), F32),
        ],
        compiler_params=_params(("arbitrary", "arbitrary")),
        name="mlstm",
    )(pa, pa, pa, pb, conv_w, conv_b.reshape(1, -1), gate_bias, norm_g.reshape(1, -1))


def _rope_lanes(x, cos, sin_lo, sin_hi):
    half = MLA_ROPE_DIM // 2
    return x * cos + pltpu.roll(x, LANES - half, axis=1) * sin_lo + pltpu.roll(x, half, axis=1) * sin_hi


def _rms_window(x, ind, g, n):
    ssq = jnp.sum(jnp.square(x * ind), axis=-1, keepdims=True)
    return x * lax.rsqrt(ssq * (1.0 / n) + LN_EPS) * g


def _mla_prep_kernel(xa_ref, xb_ref, qi_ref, qg_ref, kvi_ref, kvg_ref, wq_ref, wkv_ref,
                     cos_ref, sl_ref, sh_ref, q_ref, k_ref, v_ref, *, q_scale, kv_cols, kr_group, kr_shift):
    cos = cos_ref[...]
    sin_lo = sl_ref[...]
    sin_hi = sh_ref[...]
    q_lat = _rms_window(xa_ref[...], qi_ref[...], qg_ref[...], Q_LORA_RANK).astype(BF16)
    kv_lat = _rms_window(xb_ref[:, :kv_cols], kvi_ref[...], kvg_ref[...], KV_LORA_RANK).astype(BF16)
    k_r = pltpu.roll(xb_ref[:, kr_group * LANES:(kr_group + 1) * LANES], LANES - kr_shift, axis=1)
    k_rope = _rope_lanes(k_r, cos, sin_lo, sin_hi).astype(k_ref.dtype)
    for h in range(MLA_HEADS):
        lo = h * MLA_QK_GROUP
        q_h = jnp.dot(q_lat, wq_ref[:, lo:lo + MLA_QK_GROUP], preferred_element_type=F32) * q_scale
        q_ref[:, lo:lo + LANES] = q_h[:, :LANES].astype(q_ref.dtype)
        q_ref[:, lo + LANES:lo + 2 * LANES] = _rope_lanes(q_h[:, LANES:], cos, sin_lo, sin_hi).astype(q_ref.dtype)
        kv_h = jnp.dot(kv_lat, wkv_ref[:, lo:lo + MLA_QK_GROUP], preferred_element_type=F32)
        k_ref[:, lo:lo + LANES] = kv_h[:, :LANES].astype(k_ref.dtype)
        k_ref[:, lo + LANES:lo + 2 * LANES] = k_rope
        v_ref[:, h * MLA_V_DIM:(h + 1) * MLA_V_DIM] = kv_h[:, LANES:].astype(v_ref.dtype)


def _rope_tables(seq):
    half = MLA_ROPE_DIM // 2
    inv = ROPE_THETA ** (-jnp.arange(half, dtype=F32) / half)
    ang = jnp.arange(seq, dtype=F32)[:, None] * inv[None, :]
    cos, sin = jnp.cos(ang), jnp.sin(ang)
    zero = jnp.zeros_like(cos)
    cos_t = jnp.concatenate([cos, cos, zero, zero], axis=1)
    sin_lo = jnp.concatenate([-sin, zero, zero, zero], axis=1)
    sin_hi = jnp.concatenate([zero, sin, zero, zero], axis=1)
    return cos_t, sin_lo, sin_hi


def _window_row(values, lo, width):
    return jnp.pad(values.astype(F32), (lo, width - lo - values.shape[0])).reshape(1, width)


def _mla_prep(pb2d, q_norm_g, kv_norm_g, w_uq, w_ukv, *, seq, tm=512):
    n_tok = pb2d.shape[0]
    dq = MLA_NOPE_DIM + MLA_ROPE_DIM
    xb_width = 512
    xb_start = (PB_CKV // xb_width) * xb_width
    assert PB_VALID <= xb_start + xb_width and PB_COLS % xb_width == 0
    xa_cols = -(-PB_CKV // LANES) * LANES
    ckv_lo = PB_CKV - xb_start
    kv_cols = -(-(ckv_lo + KV_LORA_RANK) // LANES) * LANES
    kr_lo = PB_KR - xb_start
    assert kr_lo % LANES + MLA_ROPE_DIM <= LANES
    wq = w_uq.reshape(Q_LORA_RANK, MLA_HEADS, dq)
    wq = jnp.pad(wq, ((PB_CQ, xa_cols - PB_CKV), (0, 0), (0, MLA_QK_GROUP - dq)))
    wq = wq.reshape(xa_cols, MLA_HEADS * MLA_QK_GROUP).astype(BF16)
    wkv = jnp.pad(w_ukv, ((ckv_lo, kv_cols - ckv_lo - KV_LORA_RANK), (0, 0))).astype(BF16)
    q_ind = _window_row(jnp.ones((Q_LORA_RANK,), F32), PB_CQ, xa_cols)
    q_gain = _window_row(q_norm_g, PB_CQ, xa_cols)
    kv_ind = _window_row(jnp.ones((KV_LORA_RANK,), F32), ckv_lo, kv_cols)
    kv_gain = _window_row(kv_norm_g, ckv_lo, kv_cols)
    cos_t, sin_lo, sin_hi = _rope_tables(seq)
    tiles_per_seq = seq // tm
    pos = lambda i: (i % tiles_per_seq, 0)
    kern = functools.partial(_mla_prep_kernel, q_scale=math.log2(math.e) / math.sqrt(dq),
                             kv_cols=kv_cols, kr_group=kr_lo // LANES, kr_shift=kr_lo % LANES)
    qk_cols = MLA_HEADS * MLA_QK_GROUP
    const = lambda i: (0, 0)
    return pl.pallas_call(
        kern,
        grid=(n_tok // tm,),
        in_specs=[
            pl.BlockSpec((tm, xa_cols), lambda i: (i, 0)),
            pl.BlockSpec((tm, xb_width), lambda i: (i, xb_start // xb_width)),
            pl.BlockSpec((1, xa_cols), const),
            pl.BlockSpec((1, xa_cols), const),
            pl.BlockSpec((1, kv_cols), const),
            pl.BlockSpec((1, kv_cols), const),
            pl.BlockSpec((xa_cols, qk_cols), const),
            pl.BlockSpec((kv_cols, qk_cols), const),
            pl.BlockSpec((tm, LANES), pos),
            pl.BlockSpec((tm, LANES), pos),
            pl.BlockSpec((tm, LANES), pos),
        ],
        out_specs=[
            pl.BlockSpec((tm, qk_cols), lambda i: (i, 0)),
            pl.BlockSpec((tm, qk_cols), lambda i: (i, 0)),
            pl.BlockSpec((tm, MLA_HEADS * MLA_V_DIM), lambda i: (i, 0)),
        ],
        out_shape=[
            jax.ShapeDtypeStruct((n_tok, qk_cols), BF16),
            jax.ShapeDtypeStruct((n_tok, qk_cols), BF16),
            jax.ShapeDtypeStruct((n_tok, MLA_HEADS * MLA_V_DIM), BF16),
        ],
        compiler_params=_params(("arbitrary",)),
        name="mla_prep",
    )(pb2d, pb2d, q_ind, q_gain, kv_ind, kv_gain, wq, wkv, cos_t, sin_lo, sin_hi)


def _flash_kernel(q_ref, k_ref, v_ref, o_ref, m_ref, acc_ref, *, tq, heads):
    qi = pl.program_id(2)
    m_ref[...] = jnp.full_like(m_ref, NEG_BIG)
    acc_ref[...] = jnp.zeros_like(acc_ref)
    ones_col = (lax.broadcasted_iota(jnp.int32, (tq, LANES), 1) == 0).astype(BF16)
    on_or_below_diag = (lax.broadcasted_iota(jnp.int32, (tq, tq), 1)
                        <= lax.broadcasted_iota(jnp.int32, (tq, tq), 0))

    def key_block(j, masked):
        rows = pl.ds(pl.multiple_of(j * tq, tq), tq)
        for h in range(heads):
            q = q_ref[0, :, h * MLA_QK_GROUP:(h + 1) * MLA_QK_GROUP]
            k = k_ref[0, rows, h * MLA_QK_GROUP:(h + 1) * MLA_QK_GROUP]
            s = lax.dot_general(q, k, (((1,), (1,)), ((), ())), preferred_element_type=F32)
            if masked:
                s = jnp.where(on_or_below_diag, s, NEG_BIG)
            m_prev = m_ref[h]
            m_new = jnp.maximum(m_prev, jnp.max(s, axis=-1, keepdims=True))
            p = jnp.exp2(s - m_new[:, 0:1]).astype(BF16)
            alpha = jnp.exp2(m_prev - m_new)
            v_aug = jnp.concatenate([v_ref[0, rows, h * MLA_V_DIM:(h + 1) * MLA_V_DIM], ones_col], axis=1)
            pv = jnp.dot(p, v_aug, preferred_element_type=F32)
            acc_ref[h] = jnp.concatenate([alpha, alpha], axis=1) * acc_ref[h] + pv
            m_ref[h] = m_new

    def body(j, carry):
        key_block(j, False)
        return carry

    lax.fori_loop(0, qi, body, 0)
    key_block(qi, True)
    for h in range(heads):
        acc = acc_ref[h]
        o_ref[0, :, h * MLA_V_DIM:(h + 1) * MLA_V_DIM] = (
            acc[:, :MLA_V_DIM] / acc[:, MLA_V_DIM:MLA_V_DIM + 1]).astype(o_ref.dtype)


def _flash(q, k, v, *, tq=512, heads=4):
    bsz, seq, _ = q.shape
    kern = functools.partial(_flash_kernel, tq=tq, heads=heads)
    return pl.pallas_call(
        kern,
        grid=(bsz, MLA_HEADS // heads, seq // tq),
        in_specs=[
            pl.BlockSpec((1, tq, heads * MLA_QK_GROUP), lambda b, g, qi: (b, qi, g)),
            pl.BlockSpec((1, seq, heads * MLA_QK_GROUP), lambda b, g, qi: (b, 0, g)),
            pl.BlockSpec((1, seq, heads * MLA_V_DIM), lambda b, g, qi: (b, 0, g)),
        ],
        out_specs=pl.BlockSpec((1, tq, heads * MLA_V_DIM), lambda b, g, qi: (b, qi, g)),
        out_shape=jax.ShapeDtypeStruct((bsz, seq, MLA_HEADS * MLA_V_DIM), BF16),
        scratch_shapes=[
            pltpu.VMEM((heads, tq, LANES), F32),
            pltpu.VMEM((heads, tq, MLA_V_DIM + LANES), F32),
        ],
        compiler_params=_params(("arbitrary", "arbitrary", "arbitrary")),
        name="flash",
    )(q, k, v)


def _outproj_kernel(x_ref, hml_ref, hmla_ref, mod_ref, w_ref, g_ref, b_ref, o_ref, wb_ref, *,
                    sub, tiles_per_batch, alpha, row_chunk):
    i = pl.program_id(0)
    bidx = i // tiles_per_batch
    k_ml = hml_ref.shape[1]

    @pl.when(i == 0)
    def _():
        def body(r, carry):
            rows = pl.ds(pl.multiple_of(r * row_chunk, row_chunk), row_chunk)
            wb_ref[rows, :] = w_ref[rows, :].astype(BF16)
            return carry

        lax.fori_loop(0, w_ref.shape[0] // row_chunk, body, 0)

    y = (jnp.dot(hml_ref[...], wb_ref[0:k_ml, :], preferred_element_type=F32)
         + jnp.dot(hmla_ref[...], wb_ref[k_ml:, :], preferred_element_type=F32))
    gate = 1.0 + mod_ref[bidx, 3 * sub + 2:3 * sub + 3, :]
    z = alpha * x_ref[...] + gate * y
    o_ref[...] = _layer_norm_rows(z, g_ref[...], b_ref[...])


def _outproj(x2d, hml, hmla, mod, w_out, ln_g, ln_b, *, sub, seq, alpha, tm=512, row_chunk=128):
    n_tok, d = x2d.shape
    kern = functools.partial(_outproj_kernel, sub=sub, tiles_per_batch=seq // tm, alpha=alpha,
                             row_chunk=row_chunk)
    return pl.pallas_call(
        kern,
        grid=(n_tok // tm,),
        in_specs=[
            pl.BlockSpec((tm, d), lambda i: (i, 0)),
            pl.BlockSpec((tm, hml.shape[1]), lambda i: (i, 0)),
            pl.BlockSpec((tm, hmla.shape[1]), lambda i: (i, 0)),
            pl.BlockSpec(mod.shape, lambda i: (0, 0, 0)),
            pl.BlockSpec(w_out.shape, lambda i: (0, 0), pipeline_mode=pl.Buffered(1)),
            pl.BlockSpec((1, d), lambda i: (0, 0)),
            pl.BlockSpec((1, d), lambda i: (0, 0)),
        ],
        out_specs=pl.BlockSpec((tm, d), lambda i: (i, 0)),
        out_shape=jax.ShapeDtypeStruct((n_tok, d), F32),
        scratch_shapes=[pltpu.VMEM(w_out.shape, BF16)],
        compiler_params=_params(("arbitrary",)),
        name="outproj",
    )(x2d, hml, hmla, mod, w_out, ln_g.reshape(1, d), ln_b.reshape(1, d))


def kernel(x, c, w_ada, b_ada, ffn1_w1, ffn1_w3, ffn1_w2, ln1_g, ln1_b, w_in, conv_w, conv_b, b_igate, b_fgate, ml_norm_g, q_norm_g, w_uq, kv_norm_g, w_ukv, w_out, ln2_g, ln2_b, ffn2_w1, ffn2_w3, ffn2_w2, ln3_g, ln3_b):
    bsz, seq, d = x.shape
    depth = w_ada.shape[0]
    alpha = (2.0 * depth) ** 0.25
    h2d = x.reshape(bsz * seq, d)
    for l in range(depth):
        mod = _adaln(c, w_ada[l], b_ada[l]).reshape(bsz, N_SUBLAYERS * 3, d)
        h2d = _ffn(h2d, mod, ffn1_w1[l], ffn1_w3[l], ffn1_w2[l], ln1_g[l], ln1_b[l],
                   sub=0, seq=seq, alpha=alpha)
        assert w_in.shape[2] == D_IN
        pa, pb = _inproj(h2d, mod, w_in[l].T, sub=1, seq=seq)
        hml = _mlstm(pa.reshape(bsz, seq, PA_COLS), pb.reshape(bsz, seq, PB_COLS), conv_w[l], conv_b[l],
                     b_igate[l], b_fgate[l], ml_norm_g[l])
        q, k, v = _mla_prep(pb, q_norm_g[l], kv_norm_g[l], w_uq[l], w_ukv[l], seq=seq)
        hmla = _flash(q.reshape(bsz, seq, -1), k.reshape(bsz, seq, -1), v.reshape(bsz, seq, -1))
        h2d = _outproj(h2d, hml.reshape(bsz * seq, -1), hmla.reshape(bsz * seq, -1), mod,
                       w_out[l], ln2_g[l], ln2_b[l], sub=1, seq=seq, alpha=alpha)
        h2d = _ffn(h2d, mod, ffn2_w1[l], ffn2_w3[l], ffn2_w2[l], ln3_g[l], ln3_b[l],
                   sub=2, seq=seq, alpha=alpha)
    return h2d.reshape(bsz, seq, d)
```

```python
import functools
import math

import jax
import jax.numpy as jnp
from jax import lax
from jax.experimental import pallas as pl
from jax.experimental.pallas import tpu as pltpu

F32 = jnp.float32
BF16 = jnp.bfloat16

ML_HEADS = 4
ML_QK_DIM = 128
ML_V_DIM = 256
CONV_WIDTH = 4
MLA_HEADS = 8
MLA_NOPE_DIM = 128
MLA_ROPE_DIM = 64
MLA_V_DIM = 128
Q_LORA_RANK = 512
KV_LORA_RANK = 256
ROPE_THETA = 10000.0
FFN_RES_WEIGHT = 0.5
N_SUBLAYERS = 3
LN_EPS = 1e-5

LANES = 128
SUBLANES = 8
VMEM_LIMIT_BYTES = 56 * 1024 * 1024

ML_QK_COLS = 2 * ML_HEADS * ML_QK_DIM
ML_V_COLS = ML_HEADS * ML_V_DIM
PA_COLS = ML_QK_COLS + 2 * ML_V_COLS
PB_GATES = 0
PB_CQ = PB_GATES + 2 * ML_HEADS
PB_CKV = PB_CQ + Q_LORA_RANK
PB_KR = PB_CKV + KV_LORA_RANK
PB_VALID = PB_KR + MLA_ROPE_DIM
PB_COLS = -(-PB_VALID // 512) * 512
D_IN = PA_COLS + PB_VALID
MLA_QK_GROUP = 2 * LANES
NEG_BIG = -1e30


def _params(semantics):
    return pltpu.CompilerParams(dimension_semantics=semantics, vmem_limit_bytes=VMEM_LIMIT_BYTES)


def _silu(v):
    return v * jax.nn.sigmoid(v)


def _layer_norm_rows(z, g, b):
    mu = jnp.mean(z, axis=-1, keepdims=True)
    zc = z - mu
    var = jnp.mean(zc * zc, axis=-1, keepdims=True)
    return zc * lax.rsqrt(var + LN_EPS) * g + b


def _adaln_kernel(c_ref, w_ref, b_ref, o_ref):
    sc = _silu(c_ref[...]).astype(BF16)
    o_ref[...] = jnp.dot(sc, w_ref[...].astype(BF16), preferred_element_type=F32) + b_ref[...]


def _adaln(c, w, b, *, tn=1024):
    bsz, d = c.shape
    n = w.shape[1]
    rows = -(-bsz // SUBLANES) * SUBLANES
    c_pad = jnp.pad(c, ((0, rows - bsz), (0, 0)))
    out = pl.pallas_call(
        _adaln_kernel,
        grid=(n // tn,),
        in_specs=[
            pl.BlockSpec((rows, d), lambda j: (0, 0)),
            pl.BlockSpec((d, tn), lambda j: (0, j)),
            pl.BlockSpec((1, tn), lambda j: (0, j)),
        ],
        out_specs=pl.BlockSpec((rows, tn), lambda j: (0, j)),
        out_shape=jax.ShapeDtypeStruct((rows, n), F32),
        compiler_params=_params(("arbitrary",)),
        name="adaln",
    )(c_pad, w, b.reshape(1, n))
    return out[:bsz]


def _ffn_kernel(x_ref, mod_ref, g_ref, b_ref, w1_hbm, w3_hbm, w2_hbm, o_ref,
                w1_buf, w3_buf, w2_buf, sem, *, sub, tiles_per_batch, alpha, finish_rows, tf):
    i = pl.program_id(0)
    n_tiles = pl.num_programs(0)
    bidx = i // tiles_per_batch
    tm = x_ref.shape[0]
    n_f = w1_hbm.shape[1] // tf

    def chunk_copies(c, slot):
        cols = pl.ds(pl.multiple_of(c * tf, tf), tf)
        return (pltpu.make_async_copy(w1_hbm.at[:, cols], w1_buf.at[slot], sem.at[0, slot]),
                pltpu.make_async_copy(w3_hbm.at[:, cols], w3_buf.at[slot], sem.at[1, slot]),
                pltpu.make_async_copy(w2_hbm.at[cols, :], w2_buf.at[slot], sem.at[2, slot]))

    def start_chunk(c, slot):
        for cp in chunk_copies(c, slot):
            cp.start()

    def wait_chunk(c, slot):
        for cp in chunk_copies(c, slot):
            cp.wait()

    @pl.when(i == 0)
    def _():
        start_chunk(0, 0)

    shift = mod_ref[bidx, 3 * sub:3 * sub + 1, :]
    scale1 = 1.0 + mod_ref[bidx, 3 * sub + 1:3 * sub + 2, :]

    def partial_out(x_rows, w1, w3, w2):
        u = (x_rows * scale1 + shift).astype(BF16)
        a = jnp.dot(u, w1, preferred_element_type=F32)
        b = jnp.dot(u, w3, preferred_element_type=F32)
        h = (_silu(a) * b).astype(BF16)
        return jnp.dot(h, w2, preferred_element_type=F32)

    def chunk_weights(slot):
        return (w1_buf[slot].astype(BF16), w3_buf[slot].astype(BF16), w2_buf[slot].astype(BF16))

    start_chunk(1, 1)
    wait_chunk(0, 0)
    o_ref[...] = partial_out(x_ref[...], *chunk_weights(0))

    def chunk_pair(k, carry):
        c1 = 2 * k + 1
        start_chunk(c1 + 1, 0)
        wait_chunk(c1, 1)
        o_ref[...] += partial_out(x_ref[...], *chunk_weights(1))
        start_chunk(c1 + 2, 1)
        wait_chunk(c1 + 1, 0)
        o_ref[...] += partial_out(x_ref[...], *chunk_weights(0))
        return carry

    lax.fori_loop(0, (n_f - 2) // 2, chunk_pair, 0)

    @pl.when(i + 1 < n_tiles)
    def _():
        start_chunk(0, 0)

    wait_chunk(n_f - 1, 1)
    gate = FFN_RES_WEIGHT * (1.0 + mod_ref[bidx, 3 * sub + 2:3 * sub + 3, :])
    g = g_ref[...]
    b_ln = b_ref[...]
    last_w = chunk_weights(1)
    for r in range(tm // finish_rows):
        rows = slice(r * finish_rows, (r + 1) * finish_rows)
        x_rows = x_ref[rows, :]
        y = o_ref[rows, :] + partial_out(x_rows, *last_w)
        o_ref[rows, :] = _layer_norm_rows(alpha * x_rows + gate * y, g, b_ln)


def _ffn(x2d, mod, w1, w3, w2, ln_g, ln_b, *, sub, seq, alpha, tm=1024, tf=256, finish_rows=256):
    n_tok, d = x2d.shape
    f = w1.shape[1]
    tm = min(tm, seq)
    assert f % (2 * tf) == 0 and f // tf >= 4 and tm % finish_rows == 0
    kern = functools.partial(_ffn_kernel, sub=sub, tiles_per_batch=seq // tm, alpha=alpha,
                             finish_rows=finish_rows, tf=tf)
    hbm = pl.BlockSpec(memory_space=pl.ANY)
    return pl.pallas_call(
        kern,
        grid=(n_tok // tm,),
        in_specs=[
            pl.BlockSpec((tm, d), lambda i: (i, 0)),
            pl.BlockSpec(mod.shape, lambda i: (0, 0, 0)),
            pl.BlockSpec((1, d), lambda i: (0, 0)),
            pl.BlockSpec((1, d), lambda i: (0, 0)),
            hbm, hbm, hbm,
        ],
        out_specs=pl.BlockSpec((tm, d), lambda i: (i, 0)),
        out_shape=jax.ShapeDtypeStruct((n_tok, d), F32),
        scratch_shapes=[
            pltpu.VMEM((2, d, tf), w1.dtype),
            pltpu.VMEM((2, d, tf), w3.dtype),
            pltpu.VMEM((2, tf, d), w2.dtype),
            pltpu.SemaphoreType.DMA((3, 2)),
        ],
        compiler_params=_params(("arbitrary",)),
        name=f"ffn{sub}",
    )(x2d, mod, ln_g.reshape(1, d), ln_b.reshape(1, d), w1, w3, w2)


def _inproj_kernel(x_ref, mod_ref, w_ref, oa_ref, ob_ref, wbf_ref, *,
                   sub, tiles_per_batch, n_a, n_valid):
    i = pl.program_id(0)
    j = pl.program_id(1)
    bidx = i // tiles_per_batch
    tn = w_ref.shape[0]

    @pl.when(i == 0)
    def _():
        out_col = j * tn + lax.broadcasted_iota(jnp.int32, w_ref.shape, 0)
        wbf_ref[j] = jnp.where(out_col < n_valid, w_ref[...], 0.0).astype(BF16)

    def project():
        shift = mod_ref[bidx, 3 * sub:3 * sub + 1, :]
        scale1 = 1.0 + mod_ref[bidx, 3 * sub + 1:3 * sub + 2, :]
        u = (x_ref[...] * scale1 + shift).astype(BF16)
        return lax.dot_general(u, wbf_ref[j], (((1,), (1,)), ((), ())), preferred_element_type=F32)

    @pl.when(j < n_a)
    def _():
        oa_ref[...] = project().astype(BF16)

    @pl.when(j >= n_a)
    def _():
        ob_ref[...] = project()


def _inproj(x2d, mod, w_in_t, *, sub, seq, tm=1024, tn=512):
    n_tok, d = x2d.shape
    n_a = PA_COLS // tn
    n_b = PB_COLS // tn
    n_blocks = n_a + n_b
    kern = functools.partial(_inproj_kernel, sub=sub, tiles_per_batch=seq // tm, n_a=n_a,
                             n_valid=w_in_t.shape[0])
    return pl.pallas_call(
        kern,
        grid=(n_tok // tm, n_blocks),
        in_specs=[
            pl.BlockSpec((tm, d), lambda i, j: (i, 0)),
            pl.BlockSpec(mod.shape, lambda i, j: (0, 0, 0)),
            pl.BlockSpec((tn, d), lambda i, j: (jnp.where(i == 0, j, n_blocks - 1), 0)),
        ],
        out_specs=[
            pl.BlockSpec((tm, tn), lambda i, j: (i, jnp.minimum(j, n_a - 1))),
            pl.BlockSpec((tm, tn), lambda i, j: (i, jnp.maximum(j - n_a, 0))),
        ],
        out_shape=[
            jax.ShapeDtypeStruct((n_tok, PA_COLS), BF16),
            jax.ShapeDtypeStruct((n_tok, PB_COLS), F32),
        ],
        scratch_shapes=[pltpu.VMEM((n_blocks, tn, d), BF16)],
        compiler_params=_params(("arbitrary", "arbitrary")),
        name="inproj",
    )(x2d, mod, w_in_t)


def _lane_scan(v, op, chunk):
    lane = lax.broadcasted_iota(jnp.int32, v.shape, 1) % chunk
    ident = 0.0 if op is jnp.add else NEG_BIG
    s = 1
    while s < chunk:
        shifted = pltpu.roll(v, s, axis=1)
        v = op(v, jnp.where(lane >= s, shifted, ident))
        s *= 2
    return v


def _mlstm_gate_scans(bi, gates_ref, gb_ref, scan_ref, chunk):
    seq = gates_ref.shape[1]
    for c in range(seq // LANES):
        cols = slice(c * LANES, (c + 1) * LANES)
        scan_ref[bi, 0, :, cols] = gates_ref[bi, cols, :].T[0:SUBLANES, :]
    z = scan_ref[bi, 0] + gb_ref[:, 0:1]
    logi = pltpu.roll(z, ML_HEADS, axis=0)
    logf = jnp.minimum(z, 0.0) - jnp.log1p(jnp.exp(-jnp.abs(z)))
    bcum = _lane_scan(logf, jnp.add, chunk)
    r = logi - bcum
    scan_ref[bi, 0] = bcum
    scan_ref[bi, 1] = r
    scan_ref[bi, 2] = _lane_scan(r, jnp.maximum, chunk)


def _mlstm_kernel(qk_ref, v_ref, og_ref, gates_ref, cw_ref, cb_ref, gb_ref, ng_ref, o_ref,
                  c_ref, m_ref, halo_ref, scan_ref, *, chunk):
    t = pl.program_id(1)

    @pl.when(t == 0)
    def _():
        c_ref[...] = jnp.zeros_like(c_ref)
        m_ref[...] = jnp.zeros_like(m_ref)
        halo_ref[...] = jnp.zeros_like(halo_ref)
        for bi in range(qk_ref.shape[0]):
            _mlstm_gate_scans(bi, gates_ref, gb_ref, scan_ref, chunk)

    for bi in range(qk_ref.shape[0]):
        _mlstm_row(bi, t, qk_ref, v_ref, og_ref, cw_ref, cb_ref, ng_ref, o_ref,
                   c_ref, m_ref, halo_ref, scan_ref, chunk)


def _mlstm_row(bi, t, qk_ref, v_ref, og_ref, cw_ref, cb_ref, ng_ref, o_ref,
               c_ref, m_ref, halo_ref, scan_ref, chunk):
    hq = ML_HEADS * ML_QK_DIM

    qk_pre = qk_ref[bi].astype(F32)
    ext = jnp.concatenate([halo_ref[bi], qk_pre], axis=0)
    halo_ref[bi] = qk_pre[chunk - SUBLANES:, :]
    conv = cb_ref[...]
    for tap in range(CONV_WIDTH):
        lo = SUBLANES - (CONV_WIDTH - 1) + tap
        conv = conv + cw_ref[tap:tap + 1, :] * ext[lo:lo + chunk, :]
    qk = _silu(conv)

    lanes = pl.ds(pl.multiple_of(t * chunk, chunk), chunk)
    bcum = scan_ref[bi, 0, :, lanes]
    r = scan_ref[bi, 1, :, lanes]
    cm = scan_ref[bi, 2, :, lanes]
    m_in = m_ref[bi, :, 0:1]
    m_all = jnp.maximum(m_in, cm)
    m_last = m_all[:, chunk - 1:chunk]
    m_ref[bi] = jnp.broadcast_to(bcum[:, chunk - 1:chunk] + m_last, (SUBLANES, LANES))
    s_inter = jnp.exp(m_in - m_all)
    e_neg_m = jnp.exp(-(bcum + m_all))
    w_last = jnp.exp(r - m_last)
    s_old = jnp.exp(m_in - m_last)

    stack = jnp.concatenate(
        [m_all, s_inter, e_neg_m, w_last, jnp.zeros((LANES - 4 * SUBLANES, chunk), F32)], axis=0)
    cols = stack.T

    row_t = lax.broadcasted_iota(jnp.int32, (chunk, chunk), 0)
    col_j = lax.broadcasted_iota(jnp.int32, (chunk, chunk), 1)
    causal = col_j <= row_t
    ones_col = (lax.broadcasted_iota(jnp.int32, (chunk, LANES), 1) == 0).astype(BF16)

    for h in range(ML_HEADS):
        row = ML_HEADS + h
        q_h = qk[:, h * ML_QK_DIM:(h + 1) * ML_QK_DIM].astype(BF16)
        k_f = qk[:, hq + h * ML_QK_DIM:hq + (h + 1) * ML_QK_DIM] * (ML_QK_DIM ** -0.5)
        v_aug = jnp.concatenate([v_ref[bi, :, h * ML_V_DIM:(h + 1) * ML_V_DIM], ones_col], axis=1)
        m_col = cols[:, row:row + 1]
        s_col = cols[:, SUBLANES + row:SUBLANES + row + 1]
        e_col = cols[:, 2 * SUBLANES + row:2 * SUBLANES + row + 1]
        w_col = cols[:, 3 * SUBLANES + row:3 * SUBLANES + row + 1]

        decay = jnp.exp(jnp.where(causal, r[row:row + 1, :] - m_col, NEG_BIG))
        scores = lax.dot_general(q_h, k_f.astype(BF16), (((1,), (1,)), ((), ())),
                                 preferred_element_type=F32)
        p = (decay * scores).astype(BF16)
        c_old = c_ref[bi, h]
        tot = (jnp.dot(p, v_aug, preferred_element_type=F32)
               + s_col * jnp.dot(q_h, c_old.astype(BF16), preferred_element_type=F32))
        num = tot[:, :ML_V_DIM]
        den = tot[:, ML_V_DIM:ML_V_DIM + 1]
        hid = num / jnp.maximum(jnp.abs(den), e_col)

        k_w = (k_f * w_col).astype(BF16)
        c_ref[bi, h] = s_old[row:row + 1, :] * c_old + lax.dot_general(
            k_w, v_aug, (((0,), (0,)), ((), ())), preferred_element_type=F32)

        mu = jnp.mean(hid, axis=-1, keepdims=True)
        hc = hid - mu
        var = jnp.mean(hc * hc, axis=-1, keepdims=True)
        hn = hc * lax.rsqrt(var + LN_EPS) * ng_ref[:, h * ML_V_DIM:(h + 1) * ML_V_DIM]
        gate = jax.nn.sigmoid(og_ref[bi, :, h * ML_V_DIM:(h + 1) * ML_V_DIM].astype(F32))
        o_ref[bi, :, h * ML_V_DIM:(h + 1) * ML_V_DIM] = (hn * gate).astype(o_ref.dtype)


def _mlstm(pa, pb, conv_w, conv_b, b_igate, b_fgate, norm_g, *, chunk=128, rows_per_step=2):
    bsz, seq, _ = pa.shape
    rb = rows_per_step
    gate_bias = jnp.broadcast_to(jnp.concatenate([b_igate, b_fgate])[:, None], (SUBLANES, LANES))
    kern = functools.partial(_mlstm_kernel, chunk=chunk)
    return pl.pallas_call(
        kern,
        grid=(bsz // rb, seq // chunk),
        in_specs=[
            pl.BlockSpec((rb, chunk, ML_QK_COLS), lambda b, t: (b, t, 0)),
            pl.BlockSpec((rb, chunk, ML_V_COLS), lambda b, t: (b, t, ML_QK_COLS // ML_V_COLS)),
            pl.BlockSpec((rb, chunk, ML_V_COLS), lambda b, t: (b, t, ML_QK_COLS // ML_V_COLS + 1)),
            pl.BlockSpec((rb, seq, LANES), lambda b, t: (b, 0, PB_GATES // LANES)),
            pl.BlockSpec((CONV_WIDTH, ML_QK_COLS), lambda b, t: (0, 0)),
            pl.BlockSpec((1, ML_QK_COLS), lambda b, t: (0, 0)),
            pl.BlockSpec((SUBLANES, LANES), lambda b, t: (0, 0)),
            pl.BlockSpec((1, ML_V_COLS), lambda b, t: (0, 0)),
        ],
        out_specs=pl.BlockSpec((rb, chunk, ML_V_COLS), lambda b, t: (b, t, 0)),
        out_shape=jax.ShapeDtypeStruct((bsz, seq, ML_V_COLS), BF16),
        scratch_shapes=[
            pltpu.VMEM((rb, ML_HEADS, ML_QK_DIM, ML_V_DIM + LANES), F32),
            pltpu.VMEM((rb, SUBLANES, LANES), F32),
            pltpu.VMEM((rb, SUBLANES, ML_QK_COLS), F32),
            pltpu.VMEM((rb, 3, SUBLANES, seq), F32),
        ],
        compiler_params=_params(("arbitrary", "arbitrary")),
        name="mlstm",
    )(pa, pa, pa, pb, conv_w, conv_b.reshape(1, -1), gate_bias, norm_g.reshape(1, -1))


def _rope_lanes(x, cos, sin_lo, sin_hi):
    half = MLA_ROPE_DIM // 2
    return x * cos + pltpu.roll(x, LANES - half, axis=1) * sin_lo + pltpu.roll(x, half, axis=1) * sin_hi


def _rms_window(x, ind, g, n):
    ssq = jnp.sum(jnp.square(x * ind), axis=-1, keepdims=True)
    return x * lax.rsqrt(ssq * (1.0 / n) + LN_EPS) * g


def _mla_prep_kernel(xa_ref, xb_ref, qi_ref, qg_ref, kvi_ref, kvg_ref, wq_ref, wkv_ref,
                     cos_ref, sl_ref, sh_ref, q_ref, k_ref, v_ref, *, q_scale, kv_cols, kr_group, kr_shift):
    cos = cos_ref[...]
    sin_lo = sl_ref[...]
    sin_hi = sh_ref[...]
    q_lat = _rms_window(xa_ref[...], qi_ref[...], qg_ref[...], Q_LORA_RANK).astype(BF16)
    kv_lat = _rms_window(xb_ref[:, :kv_cols], kvi_ref[...], kvg_ref[...], KV_LORA_RANK).astype(BF16)
    k_r = pltpu.roll(xb_ref[:, kr_group * LANES:(kr_group + 1) * LANES], LANES - kr_shift, axis=1)
    k_rope = _rope_lanes(k_r, cos, sin_lo, sin_hi).astype(k_ref.dtype)
    for h in range(MLA_HEADS):
        lo = h * MLA_QK_GROUP
        q_h = jnp.dot(q_lat, wq_ref[:, lo:lo + MLA_QK_GROUP], preferred_element_type=F32) * q_scale
        q_ref[:, lo:lo + LANES] = q_h[:, :LANES].astype(q_ref.dtype)
        q_ref[:, lo + LANES:lo + 2 * LANES] = _rope_lanes(q_h[:, LANES:], cos, sin_lo, sin_hi).astype(q_ref.dtype)
        kv_h = jnp.dot(kv_lat, wkv_ref[:, lo:lo + MLA_QK_GROUP], preferred_element_type=F32)
        k_ref[:, lo:lo + LANES] = kv_h[:, :LANES].astype(k_ref.dtype)
        k_ref[:, lo + LANES:lo + 2 * LANES] = k_rope
        v_ref[:, h * MLA_V_DIM:(h + 1) * MLA_V_DIM] = kv_h[:, LANES:].astype(v_ref.dtype)


def _rope_tables(seq):
    half = MLA_ROPE_DIM // 2
    inv = ROPE_THETA ** (-jnp.arange(half, dtype=F32) / half)
    ang = jnp.arange(seq, dtype=F32)[:, None] * inv[None, :]
    cos, sin = jnp.cos(ang), jnp.sin(ang)
    zero = jnp.zeros_like(cos)
    cos_t = jnp.concatenate([cos, cos, zero, zero], axis=1)
    sin_lo = jnp.concatenate([-sin, zero, zero, zero], axis=1)
    sin_hi = jnp.concatenate([zero, sin, zero, zero], axis=1)
    return cos_t, sin_lo, sin_hi


def _window_row(values, lo, width):
    return jnp.pad(values.astype(F32), (lo, width - lo - values.shape[0])).reshape(1, width)


def _mla_prep(pb2d, q_norm_g, kv_norm_g, w_uq, w_ukv, *, seq, tm=512):
    n_tok = pb2d.shape[0]
    dq = MLA_NOPE_DIM + MLA_ROPE_DIM
    xb_width = 512
    xb_start = (PB_CKV // xb_width) * xb_width
    assert PB_VALID <= xb_start + xb_width and PB_COLS % xb_width == 0
    xa_cols = -(-PB_CKV // LANES) * LANES
    ckv_lo = PB_CKV - xb_start
    kv_cols = -(-(ckv_lo + KV_LORA_RANK) // LANES) * LANES
    kr_lo = PB_KR - xb_start
    assert kr_lo % LANES + MLA_ROPE_DIM <= LANES
    wq = w_uq.reshape(Q_LORA_RANK, MLA_HEADS, dq)
    wq = jnp.pad(wq, ((PB_CQ, xa_cols - PB_CKV), (0, 0), (0, MLA_QK_GROUP - dq)))
    wq = wq.reshape(xa_cols, MLA_HEADS * MLA_QK_GROUP).astype(BF16)
    wkv = jnp.pad(w_ukv, ((ckv_lo, kv_cols - ckv_lo - KV_LORA_RANK), (0, 0))).astype(BF16)
    q_ind = _window_row(jnp.ones((Q_LORA_RANK,), F32), PB_CQ, xa_cols)
    q_gain = _window_row(q_norm_g, PB_CQ, xa_cols)
    kv_ind = _window_row(jnp.ones((KV_LORA_RANK,), F32), ckv_lo, kv_cols)
    kv_gain = _window_row(kv_norm_g, ckv_lo, kv_cols)
    cos_t, sin_lo, sin_hi = _rope_tables(seq)
    tiles_per_seq = seq // tm
    pos = lambda i: (i % tiles_per_seq, 0)
    kern = functools.partial(_mla_prep_kernel, q_scale=math.log2(math.e) / math.sqrt(dq),
                             kv_cols=kv_cols, kr_group=kr_lo // LANES, kr_shift=kr_lo % LANES)
    qk_cols = MLA_HEADS * MLA_QK_GROUP
    const = lambda i: (0, 0)
    return pl.pallas_call(
        kern,
        grid=(n_tok // tm,),
        in_specs=[
            pl.BlockSpec((tm, xa_cols), lambda i: (i, 0)),
            pl.BlockSpec((tm, xb_width), lambda i: (i, xb_start // xb_width)),
            pl.BlockSpec((1, xa_cols), const),
            pl.BlockSpec((1, xa_cols), const),
            pl.BlockSpec((1, kv_cols), const),
            pl.BlockSpec((1, kv_cols), const),
            pl.BlockSpec((xa_cols, qk_cols), const),
            pl.BlockSpec((kv_cols, qk_cols), const),
            pl.BlockSpec((tm, LANES), pos),
            pl.BlockSpec((tm, LANES), pos),
            pl.BlockSpec((tm, LANES), pos),
        ],
        out_specs=[
            pl.BlockSpec((tm, qk_cols), lambda i: (i, 0)),
            pl.BlockSpec((tm, qk_cols), lambda i: (i, 0)),
            pl.BlockSpec((tm, MLA_HEADS * MLA_V_DIM), lambda i: (i, 0)),
        ],
        out_shape=[
            jax.ShapeDtypeStruct((n_tok, qk_cols), BF16),
            jax.ShapeDtypeStruct((n_tok, qk_cols), BF16),
            jax.ShapeDtypeStruct((n_tok, MLA_HEADS * MLA_V_DIM), BF16),
        ],
        compiler_params=_params(("arbitrary",)),
        name="mla_prep",
    )(pb2d, pb2d, q_ind, q_gain, kv_ind, kv_gain, wq, wkv, cos_t, sin_lo, sin_hi)


def _flash_kernel(q_ref, k_ref, v_ref, o_ref, m_ref, acc_ref, *, tq, heads):
    qi = pl.program_id(2)
    m_ref[...] = jnp.full_like(m_ref, NEG_BIG)
    acc_ref[...] = jnp.zeros_like(acc_ref)
    ones_col = (lax.broadcasted_iota(jnp.int32, (tq, LANES), 1) == 0).astype(BF16)
    on_or_below_diag = (lax.broadcasted_iota(jnp.int32, (tq, tq), 1)
                        <= lax.broadcasted_iota(jnp.int32, (tq, tq), 0))

    def key_block(j, masked):
        rows = pl.ds(pl.multiple_of(j * tq, tq), tq)
        for h in range(heads):
            q = q_ref[0, :, h * MLA_QK_GROUP:(h + 1) * MLA_QK_GROUP]
            k = k_ref[0, rows, h * MLA_QK_GROUP:(h + 1) * MLA_QK_GROUP]
            s = lax.dot_general(q, k, (((1,), (1,)), ((), ())), preferred_element_type=F32)
            if masked:
                s = jnp.where(on_or_below_diag, s, NEG_BIG)
            m_prev = m_ref[h]
            m_new = jnp.maximum(m_prev, jnp.max(s, axis=-1, keepdims=True))
            p = jnp.exp2(s - jnp.concatenate([m_new] * (tq // LANES), axis=1)).astype(BF16)
            alpha = jnp.exp2(m_prev - m_new)
            v_aug = jnp.concatenate([v_ref[0, rows, h * MLA_V_DIM:(h + 1) * MLA_V_DIM], ones_col], axis=1)
            pv = jnp.dot(p, v_aug, preferred_element_type=F32)
            acc_ref[h] = jnp.concatenate([alpha, alpha], axis=1) * acc_ref[h] + pv
            m_ref[h] = m_new

    def body(j, carry):
        key_block(j, False)
        return carry

    lax.fori_loop(0, qi, body, 0)
    key_block(qi, True)
    for h in range(heads):
        acc = acc_ref[h]
        o_ref[0, :, h * MLA_V_DIM:(h + 1) * MLA_V_DIM] = (
            acc[:, :MLA_V_DIM] / acc[:, MLA_V_DIM:MLA_V_DIM + 1]).astype(o_ref.dtype)


def _flash(q, k, v, *, tq=512, heads=4):
    bsz, seq, _ = q.shape
    kern = functools.partial(_flash_kernel, tq=tq, heads=heads)
    return pl.pallas_call(
        kern,
        grid=(bsz, MLA_HEADS // heads, seq // tq),
        in_specs=[
            pl.BlockSpec((1, tq, heads * MLA_QK_GROUP), lambda b, g, qi: (b, qi, g)),
            pl.BlockSpec((1, seq, heads * MLA_QK_GROUP), lambda b, g, qi: (b, 0, g)),
            pl.BlockSpec((1, seq, heads * MLA_V_DIM), lambda b, g, qi: (b, 0, g)),
        ],
        out_specs=pl.BlockSpec((1, tq, heads * MLA_V_DIM), lambda b, g, qi: (b, qi, g)),
        out_shape=jax.ShapeDtypeStruct((bsz, seq, MLA_HEADS * MLA_V_DIM), BF16),
        scratch_shapes=[
            pltpu.VMEM((heads, tq, LANES), F32),
            pltpu.VMEM((heads, tq, MLA_V_DIM + LANES), F32),
        ],
        compiler_params=_params(("arbitrary", "arbitrary", "arbitrary")),
        name="flash",
    )(q, k, v)


def _outproj_kernel(x_ref, hml_ref, hmla_ref, mod_ref, w_ref, g_ref, b_ref, o_ref, wb_ref, *,
                    sub, tiles_per_batch, alpha, row_chunk):
    i = pl.program_id(0)
    bidx = i // tiles_per_batch
    k_ml = hml_ref.shape[1]

    @pl.when(i == 0)
    def _():
        def body(r, carry):
            rows = pl.ds(pl.multiple_of(r * row_chunk, row_chunk), row_chunk)
            wb_ref[rows, :] = w_ref[rows, :].astype(BF16)
            return carry

        lax.fori_loop(0, w_ref.shape[0] // row_chunk, body, 0)

    y = (jnp.dot(hml_ref[...], wb_ref[0:k_ml, :], preferred_element_type=F32)
         + jnp.dot(hmla_ref[...], wb_ref[k_ml:, :], preferred_element_type=F32))
    gate = 1.0 + mod_ref[bidx, 3 * sub + 2:3 * sub + 3, :]
    z = alpha * x_ref[...] + gate * y
    o_ref[...] = _layer_norm_rows(z, g_ref[...], b_ref[...])


def _outproj(x2d, hml, hmla, mod, w_out, ln_g, ln_b, *, sub, seq, alpha, tm=512, row_chunk=128):
    n_tok, d = x2d.shape
    kern = functools.partial(_outproj_kernel, sub=sub, tiles_per_batch=seq // tm, alpha=alpha,
                             row_chunk=row_chunk)
    return pl.pallas_call(
        kern,
        grid=(n_tok // tm,),
        in_specs=[
            pl.BlockSpec((tm, d), lambda i: (i, 0)),
            pl.BlockSpec((tm, hml.shape[1]), lambda i: (i, 0)),
            pl.BlockSpec((tm, hmla.shape[1]), lambda i: (i, 0)),
            pl.BlockSpec(mod.shape, lambda i: (0, 0, 0)),
            pl.BlockSpec(w_out.shape, lambda i: (0, 0), pipeline_mode=pl.Buffered(1)),
            pl.BlockSpec((1, d), lambda i: (0, 0)),
            pl.BlockSpec((1, d), lambda i: (0, 0)),
        ],
        out_specs=pl.BlockSpec((tm, d), lambda i: (i, 0)),
        out_shape=jax.ShapeDtypeStruct((n_tok, d), F32),
        scratch_shapes=[pltpu.VMEM(w_out.shape, BF16)],
        compiler_params=_params(("arbitrary",)),
        name="outproj",
    )(x2d, hml, hmla, mod, w_out, ln_g.reshape(1, d), ln_b.reshape(1, d))


def kernel(x, c, w_ada, b_ada, ffn1_w1, ffn1_w3, ffn1_w2, ln1_g, ln1_b, w_in, conv_w, conv_b, b_igate, b_fgate, ml_norm_g, q_norm_g, w_uq, kv_norm_g, w_ukv, w_out, ln2_g, ln2_b, ffn2_w1, ffn2_w3, ffn2_w2, ln3_g, ln3_b):
    bsz, seq, d = x.shape
    depth = w_ada.shape[0]
    alpha = (2.0 * depth) ** 0.25
    h2d = x.reshape(bsz * seq, d)
    for l in range(depth):
        mod = _adaln(c, w_ada[l], b_ada[l]).reshape(bsz, N_SUBLAYERS * 3, d)
        h2d = _ffn(h2d, mod, ffn1_w1[l], ffn1_w3[l], ffn1_w2[l], ln1_g[l], ln1_b[l],
                   sub=0, seq=seq, alpha=alpha)
        assert w_in.shape[2] == D_IN
        pa, pb = _inproj(h2d, mod, w_in[l].T, sub=1, seq=seq)
        hml = _mlstm(pa.reshape(bsz, seq, PA_COLS), pb.reshape(bsz, seq, PB_COLS), conv_w[l], conv_b[l],
                     b_igate[l], b_fgate[l], ml_norm_g[l])
        q, k, v = _mla_prep(pb, q_norm_g[l], kv_norm_g[l], w_uq[l], w_ukv[l], seq=seq)
        hmla = _flash(q.reshape(bsz, seq, -1), k.reshape(bsz, seq, -1), v.reshape(bsz, seq, -1))
        h2d = _outproj(h2d, hml.reshape(bsz * seq, -1), hmla.reshape(bsz * seq, -1), mod,
                       w_out[l], ln2_g[l], ln2_b[l], sub=1, seq=seq, alpha=alpha)
        h2d = _ffn(h2d, mod, ffn2_w1[l], ffn2_w3[l], ffn2_w2[l], ln3_g[l], ln3_b[l],
                   sub=2, seq=seq, alpha=alpha)
    return h2d.reshape(bsz, seq, d)
```

```python
import functools
import math

import jax
import jax.numpy as jnp
from jax import lax
from jax.experimental import pallas as pl
from jax.experimental.pallas import tpu as pltpu

F32 = jnp.float32
BF16 = jnp.bfloat16

ML_HEADS = 4
ML_QK_DIM = 128
ML_V_DIM = 256
CONV_WIDTH = 4
MLA_HEADS = 8
MLA_NOPE_DIM = 128
MLA_ROPE_DIM = 64
MLA_V_DIM = 128
Q_LORA_RANK = 512
KV_LORA_RANK = 256
ROPE_THETA = 10000.0
FFN_RES_WEIGHT = 0.5
N_SUBLAYERS = 3
LN_EPS = 1e-5

LANES = 128
SUBLANES = 8
VMEM_LIMIT_BYTES = 56 * 1024 * 1024

ML_QK_COLS = 2 * ML_HEADS * ML_QK_DIM
ML_V_COLS = ML_HEADS * ML_V_DIM
PA_COLS = ML_QK_COLS + 2 * ML_V_COLS
PB_GATES = 0
PB_CQ = PB_GATES + 2 * ML_HEADS
PB_CKV = PB_CQ + Q_LORA_RANK
PB_KR = PB_CKV + KV_LORA_RANK
PB_VALID = PB_KR + MLA_ROPE_DIM
PB_COLS = -(-PB_VALID // 512) * 512
D_IN = PA_COLS + PB_VALID
MLA_QK_GROUP = 2 * LANES
NEG_BIG = -1e30


def _params(semantics):
    return pltpu.CompilerParams(dimension_semantics=semantics, vmem_limit_bytes=VMEM_LIMIT_BYTES)


def _silu(v):
    return v * jax.nn.sigmoid(v)


def _layer_norm_rows(z, g, b):
    mu = jnp.mean(z, axis=-1, keepdims=True)
    zc = z - mu
    var = jnp.mean(zc * zc, axis=-1, keepdims=True)
    return zc * lax.rsqrt(var + LN_EPS) * g + b


def _adaln_kernel(c_ref, w_ref, b_ref, o_ref):
    sc = _silu(c_ref[...]).astype(BF16)
    o_ref[...] = jnp.dot(sc, w_ref[...].astype(BF16), preferred_element_type=F32) + b_ref[...]


def _adaln(c, w, b, *, tn=1024):
    bsz, d = c.shape
    n = w.shape[1]
    rows = -(-bsz // SUBLANES) * SUBLANES
    c_pad = jnp.pad(c, ((0, rows - bsz), (0, 0)))
    out = pl.pallas_call(
        _adaln_kernel,
        grid=(n // tn,),
        in_specs=[
            pl.BlockSpec((rows, d), lambda j: (0, 0)),
            pl.BlockSpec((d, tn), lambda j: (0, j)),
            pl.BlockSpec((1, tn), lambda j: (0, j)),
        ],
        out_specs=pl.BlockSpec((rows, tn), lambda j: (0, j)),
        out_shape=jax.ShapeDtypeStruct((rows, n), F32),
        compiler_params=_params(("arbitrary",)),
        name="adaln",
    )(c_pad, w, b.reshape(1, n))
    return out[:bsz]


def _ffn_kernel(x_ref, mod_ref, g_ref, b_ref, w1_hbm, w3_hbm, w2_hbm, o_ref,
                w1_buf, w3_buf, w2_buf, sem, *, sub, tiles_per_batch, alpha, finish_rows, tf):
    i = pl.program_id(0)
    n_tiles = pl.num_programs(0)
    bidx = i // tiles_per_batch
    tm = x_ref.shape[0]
    n_f = w1_hbm.shape[1] // tf

    def chunk_copies(c, slot):
        cols = pl.ds(pl.multiple_of(c * tf, tf), tf)
        return (pltpu.make_async_copy(w1_hbm.at[:, cols], w1_buf.at[slot], sem.at[0, slot]),
                pltpu.make_async_copy(w3_hbm.at[:, cols], w3_buf.at[slot], sem.at[1, slot]),
                pltpu.make_async_copy(w2_hbm.at[cols, :], w2_buf.at[slot], sem.at[2, slot]))

    def start_chunk(c, slot):
        for cp in chunk_copies(c, slot):
            cp.start()

    def wait_chunk(c, slot):
        for cp in chunk_copies(c, slot):
            cp.wait()

    @pl.when(i == 0)
    def _():
        start_chunk(0, 0)

    shift = mod_ref[bidx, 3 * sub:3 * sub + 1, :]
    scale1 = 1.0 + mod_ref[bidx, 3 * sub + 1:3 * sub + 2, :]

    def partial_out(x_rows, w1, w3, w2):
        u = (x_rows * scale1 + shift).astype(BF16)
        a = jnp.dot(u, w1, preferred_element_type=F32)
        b = jnp.dot(u, w3, preferred_element_type=F32)
        h = (_silu(a) * b).astype(BF16)
        return jnp.dot(h, w2, preferred_element_type=F32)

    def chunk_weights(slot):
        return (w1_buf[slot].astype(BF16), w3_buf[slot].astype(BF16), w2_buf[slot].astype(BF16))

    start_chunk(1, 1)
    wait_chunk(0, 0)
    o_ref[...] = partial_out(x_ref[...], *chunk_weights(0))

    def chunk_pair(k, carry):
        c1 = 2 * k + 1
        start_chunk(c1 + 1, 0)
        wait_chunk(c1, 1)
        o_ref[...] += partial_out(x_ref[...], *chunk_weights(1))
        start_chunk(c1 + 2, 1)
        wait_chunk(c1 + 1, 0)
        o_ref[...] += partial_out(x_ref[...], *chunk_weights(0))
        return carry

    lax.fori_loop(0, (n_f - 2) // 2, chunk_pair, 0)

    @pl.when(i + 1 < n_tiles)
    def _():
        start_chunk(0, 0)

    wait_chunk(n_f - 1, 1)
    gate = FFN_RES_WEIGHT * (1.0 + mod_ref[bidx, 3 * sub + 2:3 * sub + 3, :])
    g = g_ref[...]
    b_ln = b_ref[...]
    last_w = chunk_weights(1)
    for r in range(tm // finish_rows):
        rows = slice(r * finish_rows, (r + 1) * finish_rows)
        x_rows = x_ref[rows, :]
        y = o_ref[rows, :] + partial_out(x_rows, *last_w)
        o_ref[rows, :] = _layer_norm_rows(alpha * x_rows + gate * y, g, b_ln)


def _ffn(x2d, mod, w1, w3, w2, ln_g, ln_b, *, sub, seq, alpha, tm=1024, tf=256, finish_rows=256):
    n_tok, d = x2d.shape
    f = w1.shape[1]
    tm = min(tm, seq)
    assert f % (2 * tf) == 0 and f // tf >= 4 and tm % finish_rows == 0
    kern = functools.partial(_ffn_kernel, sub=sub, tiles_per_batch=seq // tm, alpha=alpha,
                             finish_rows=finish_rows, tf=tf)
    hbm = pl.BlockSpec(memory_space=pl.ANY)
    return pl.pallas_call(
        kern,
        grid=(n_tok // tm,),
        in_specs=[
            pl.BlockSpec((tm, d), lambda i: (i, 0)),
            pl.BlockSpec(mod.shape, lambda i: (0, 0, 0)),
            pl.BlockSpec((1, d), lambda i: (0, 0)),
            pl.BlockSpec((1, d), lambda i: (0, 0)),
            hbm, hbm, hbm,
        ],
        out_specs=pl.BlockSpec((tm, d), lambda i: (i, 0)),
        out_shape=jax.ShapeDtypeStruct((n_tok, d), F32),
        scratch_shapes=[
            pltpu.VMEM((2, d, tf), w1.dtype),
            pltpu.VMEM((2, d, tf), w3.dtype),
            pltpu.VMEM((2, tf, d), w2.dtype),
            pltpu.SemaphoreType.DMA((3, 2)),
        ],
        compiler_params=_params(("arbitrary",)),
        name=f"ffn{sub}",
    )(x2d, mod, ln_g.reshape(1, d), ln_b.reshape(1, d), w1, w3, w2)


def _inproj_kernel(x_ref, mod_ref, w_ref, oa_ref, ob_ref, wbf_ref, *,
                   sub, tiles_per_batch, n_a, n_valid):
    i = pl.program_id(0)
    j = pl.program_id(1)
    bidx = i // tiles_per_batch
    tn = w_ref.shape[0]

    @pl.when(i == 0)
    def _():
        out_col = j * tn + lax.broadcasted_iota(jnp.int32, w_ref.shape, 0)
        wbf_ref[j] = jnp.where(out_col < n_valid, w_ref[...], 0.0).astype(BF16)

    def project():
        shift = mod_ref[bidx, 3 * sub:3 * sub + 1, :]
        scale1 = 1.0 + mod_ref[bidx, 3 * sub + 1:3 * sub + 2, :]
        u = (x_ref[...] * scale1 + shift).astype(BF16)
        return lax.dot_general(u, wbf_ref[j], (((1,), (1,)), ((), ())), preferred_element_type=F32)

    @pl.when(j < n_a)
    def _():
        oa_ref[...] = project().astype(BF16)

    @pl.when(j >= n_a)
    def _():
        ob_ref[...] = project()


def _inproj(x2d, mod, w_in_t, *, sub, seq, tm=1024, tn=512):
    n_tok, d = x2d.shape
    n_a = PA_COLS // tn
    n_b = PB_COLS // tn
    n_blocks = n_a + n_b
    kern = functools.partial(_inproj_kernel, sub=sub, tiles_per_batch=seq // tm, n_a=n_a,
                             n_valid=w_in_t.shape[0])
    return pl.pallas_call(
        kern,
        grid=(n_tok // tm, n_blocks),
        in_specs=[
            pl.BlockSpec((tm, d), lambda i, j: (i, 0)),
            pl.BlockSpec(mod.shape, lambda i, j: (0, 0, 0)),
            pl.BlockSpec((tn, d), lambda i, j: (jnp.where(i == 0, j, n_blocks - 1), 0)),
        ],
        out_specs=[
            pl.BlockSpec((tm, tn), lambda i, j: (i, jnp.minimum(j, n_a - 1))),
            pl.BlockSpec((tm, tn), lambda i, j: (i, jnp.maximum(j - n_a, 0))),
        ],
        out_shape=[
            jax.ShapeDtypeStruct((n_tok, PA_COLS), BF16),
            jax.ShapeDtypeStruct((n_tok, PB_COLS), F32),
        ],
        scratch_shapes=[pltpu.VMEM((n_blocks, tn, d), BF16)],
        compiler_params=_params(("arbitrary", "arbitrary")),
        name="inproj",
    )(x2d, mod, w_in_t)


def _lane_scan(v, op, chunk):
    lane = lax.broadcasted_iota(jnp.int32, v.shape, 1) % chunk
    ident = 0.0 if op is jnp.add else NEG_BIG
    s = 1
    while s < chunk:
        shifted = pltpu.roll(v, s, axis=1)
        v = op(v, jnp.where(lane >= s, shifted, ident))
        s *= 2
    return v


def _mlstm_gate_scans(bi, gates_ref, gb_ref, scan_ref, chunk):
    seq = gates_ref.shape[1]
    for c in range(seq // LANES):
        cols = slice(c * LANES, (c + 1) * LANES)
        scan_ref[bi, 0, :, cols] = gates_ref[bi, cols, :].T[0:SUBLANES, :]
    z = scan_ref[bi, 0] + gb_ref[:, 0:1]
    logi = pltpu.roll(z, ML_HEADS, axis=0)
    logf = jnp.minimum(z, 0.0) - jnp.log1p(jnp.exp(-jnp.abs(z)))
    bcum = _lane_scan(logf, jnp.add, chunk)
    r = logi - bcum
    scan_ref[bi, 0] = bcum
    scan_ref[bi, 1] = r
    scan_ref[bi, 2] = _lane_scan(r, jnp.maximum, chunk)


def _mlstm_kernel(qk_ref, v_ref, og_ref, gates_ref, cw_ref, cb_ref, gb_ref, ng_ref, sel_ref, cbias_ref,
                  o_ref, c_ref, m_ref, prev_ref, scan_ref, *, chunk):
    t = pl.program_id(1)

    @pl.when(t == 0)
    def _():
        c_ref[...] = jnp.zeros_like(c_ref)
        m_ref[...] = jnp.zeros_like(m_ref)
        prev_ref[...] = jnp.zeros_like(prev_ref)
        for bi in range(qk_ref.shape[0]):
            _mlstm_gate_scans(bi, gates_ref, gb_ref, scan_ref, chunk)

    _mlstm_step(t, qk_ref, v_ref, og_ref, cw_ref, cb_ref, ng_ref, sel_ref, cbias_ref, o_ref,
                c_ref, m_ref, prev_ref, scan_ref, chunk)


def _mlstm_step(t, qk_ref, v_ref, og_ref, cw_ref, cb_ref, ng_ref, sel_ref, cbias_ref, o_ref,
                c_ref, m_ref, prev_ref, scan_ref, chunk):
    hq = ML_HEADS * ML_QK_DIM
    n_rows = qk_ref.shape[0]
    chains = [(bi, h) for bi in range(n_rows) for h in range(ML_HEADS)]
    head_cols = lambda h: slice(h * ML_V_DIM, (h + 1) * ML_V_DIM)
    causal_bias = cbias_ref[...]
    ones_col = (lax.broadcasted_iota(jnp.int32, (chunk, LANES), 1) == 0).astype(BF16)
    lanes = pl.ds(pl.multiple_of(t * chunk, chunk), chunk)

    qk = []
    for bi in range(n_rows):
        qk_now = qk_ref[bi]
        taps = jnp.dot(sel_ref[...], jnp.concatenate([prev_ref[bi], qk_now], axis=0),
                       preferred_element_type=F32)
        prev_ref[bi] = qk_now
        conv = cb_ref[...]
        for tap in range(CONV_WIDTH):
            conv = conv + cw_ref[tap:tap + 1, :] * taps[tap * chunk:(tap + 1) * chunk, :]
        qk.append(_silu(conv))

    r, cols, s_old = [], [], []
    for bi in range(n_rows):
        bcum = scan_ref[bi, 0, :, lanes]
        r_b = scan_ref[bi, 1, :, lanes]
        cm = scan_ref[bi, 2, :, lanes]
        m_in = m_ref[bi, :, 0:1]
        m_all = jnp.maximum(m_in, cm)
        m_last = m_all[:, chunk - 1:chunk]
        m_ref[bi] = jnp.broadcast_to(bcum[:, chunk - 1:chunk] + m_last, (SUBLANES, LANES))
        s_inter = jnp.exp(m_in - m_all)
        e_neg_m = jnp.exp(-(bcum + m_all))
        w_last = jnp.exp(r_b - m_last)
        stack = jnp.concatenate(
            [m_all, s_inter, e_neg_m, w_last, jnp.zeros((LANES - 4 * SUBLANES, chunk), F32)], axis=0)
        r.append(r_b)
        cols.append(stack.T)
        s_old.append(jnp.exp(m_in - m_last))

    def col(bi, h, which):
        lane = which * SUBLANES + ML_HEADS + h
        return cols[bi][:, lane:lane + 1]

    q_h = {c: qk[c[0]][:, c[1] * ML_QK_DIM:(c[1] + 1) * ML_QK_DIM].astype(BF16) for c in chains}
    k_f = {c: qk[c[0]][:, hq + c[1] * ML_QK_DIM:hq + (c[1] + 1) * ML_QK_DIM] * (ML_QK_DIM ** -0.5)
           for c in chains}
    v_aug = {c: jnp.concatenate([v_ref[c[0], :, head_cols(c[1])], ones_col], axis=1) for c in chains}
    scores = {c: lax.dot_general(q_h[c], k_f[c].astype(BF16), (((1,), (1,)), ((), ())),
                                 preferred_element_type=F32) for c in chains}
    p = {}
    for bi, h in chains:
        decay = jnp.exp(r[bi][ML_HEADS + h:ML_HEADS + h + 1, :] - col(bi, h, 0) + causal_bias)
        p[bi, h] = (decay * scores[bi, h]).astype(BF16)
    c_old = {c: c_ref[c[0], c[1]] for c in chains}
    tot = {c: (jnp.dot(p[c], v_aug[c], preferred_element_type=F32)
               + col(c[0], c[1], 1) * jnp.dot(q_h[c], c_old[c].astype(BF16), preferred_element_type=F32))
           for c in chains}
    for bi, h in chains:
        k_w = (k_f[bi, h] * col(bi, h, 3)).astype(BF16)
        c_ref[bi, h] = s_old[bi][ML_HEADS + h:ML_HEADS + h + 1, :] * c_old[bi, h] + lax.dot_general(
            k_w, v_aug[bi, h], (((0,), (0,)), ((), ())), preferred_element_type=F32)
    for bi, h in chains:
        num = tot[bi, h][:, :ML_V_DIM]
        den = tot[bi, h][:, ML_V_DIM:ML_V_DIM + 1]
        hid = num / jnp.maximum(jnp.abs(den), col(bi, h, 2))
        mu = jnp.mean(hid, axis=-1, keepdims=True)
        hc = hid - mu
        var = jnp.mean(hc * hc, axis=-1, keepdims=True)
        hn = hc * lax.rsqrt(var + LN_EPS) * ng_ref[:, head_cols(h)]
        gate = jax.nn.sigmoid(og_ref[bi, :, head_cols(h)].astype(F32))
        o_ref[bi, :, head_cols(h)] = (hn * gate).astype(o_ref.dtype)


def _mlstm(pa, pb, conv_w, conv_b, b_igate, b_fgate, norm_g, *, chunk=128, rows_per_step=2):
    bsz, seq, _ = pa.shape
    rb = rows_per_step
    gate_bias = jnp.broadcast_to(jnp.concatenate([b_igate, b_fgate])[:, None], (SUBLANES, LANES))
    t_idx = jnp.arange(CONV_WIDTH * chunk) % chunk
    tap_idx = jnp.arange(CONV_WIDTH * chunk) // chunk
    shift_sel = (jnp.arange(2 * chunk)[None, :]
                 == (chunk + t_idx - (CONV_WIDTH - 1) + tap_idx)[:, None]).astype(BF16)
    causal_bias = jnp.where(jnp.arange(chunk)[None, :] <= jnp.arange(chunk)[:, None], 0.0, NEG_BIG).astype(F32)
    kern = functools.partial(_mlstm_kernel, chunk=chunk)
    return pl.pallas_call(
        kern,
        grid=(bsz // rb, seq // chunk),
        in_specs=[
            pl.BlockSpec((rb, chunk, ML_QK_COLS), lambda b, t: (b, t, 0)),
            pl.BlockSpec((rb, chunk, ML_V_COLS), lambda b, t: (b, t, ML_QK_COLS // ML_V_COLS)),
            pl.BlockSpec((rb, chunk, ML_V_COLS), lambda b, t: (b, t, ML_QK_COLS // ML_V_COLS + 1)),
            pl.BlockSpec((rb, seq, LANES), lambda b, t: (b, 0, PB_GATES // LANES)),
            pl.BlockSpec((CONV_WIDTH, ML_QK_COLS), lambda b, t: (0, 0)),
            pl.BlockSpec((1, ML_QK_COLS), lambda b, t: (0, 0)),
            pl.BlockSpec((SUBLANES, LANES), lambda b, t: (0, 0)),
            pl.BlockSpec((1, ML_V_COLS), lambda b, t: (0, 0)),
            pl.BlockSpec(shift_sel.shape, lambda b, t: (0, 0)),
            pl.BlockSpec(causal_bias.shape, lambda b, t: (0, 0)),
        ],
        out_specs=pl.BlockSpec((rb, chunk, ML_V_COLS), lambda b, t: (b, t, 0)),
        out_shape=jax.ShapeDtypeStruct((bsz, seq, ML_V_COLS), BF16),
        scratch_shapes=[
            pltpu.VMEM((rb, ML_HEADS, ML_QK_DIM, ML_V_DIM + LANES), F32),
            pltpu.VMEM((rb, SUBLANES, LANES), F32),
            pltpu.VMEM((rb, chunk, ML_QK_COLS), pa.dtype),
            pltpu.VMEM((rb, 3, SUBLANES, seq), F32),
        ],
        compiler_params=_params(("arbitrary", "arbitrary")),
        name="mlstm",
    )(pa, pa, pa, pb, conv_w, conv_b.reshape(1, -1), gate_bias, norm_g.reshape(1, -1),
      shift_sel, causal_bias)


def _rope_lanes(x, cos, sin_lo, sin_hi):
    half = MLA_ROPE_DIM // 2
    return x * cos + pltpu.roll(x, LANES - half, axis=1) * sin_lo + pltpu.roll(x, half, axis=1) * sin_hi


def _rms_window(x, ind, g, n):
    ssq = jnp.sum(jnp.square(x * ind), axis=-1, keepdims=True)
    return x * lax.rsqrt(ssq * (1.0 / n) + LN_EPS) * g


def _mla_prep_kernel(xa_ref, xb_ref, qi_ref, qg_ref, kvi_ref, kvg_ref, wq_ref, wkv_ref,
                     cos_ref, sl_ref, sh_ref, q_ref, k_ref, v_ref, *, q_scale, kv_cols, kr_group, kr_shift):
    cos = cos_ref[...]
    sin_lo = sl_ref[...]
    sin_hi = sh_ref[...]
    q_lat = _rms_window(xa_ref[...], qi_ref[...], qg_ref[...], Q_LORA_RANK).astype(BF16)
    kv_lat = _rms_window(xb_ref[:, :kv_cols], kvi_ref[...], kvg_ref[...], KV_LORA_RANK).astype(BF16)
    k_r = pltpu.roll(xb_ref[:, kr_group * LANES:(kr_group + 1) * LANES], LANES - kr_shift, axis=1)
    k_rope = _rope_lanes(k_r, cos, sin_lo, sin_hi).astype(k_ref.dtype)
    for h in range(MLA_HEADS):
        lo = h * MLA_QK_GROUP
        q_h = jnp.dot(q_lat, wq_ref[:, lo:lo + MLA_QK_GROUP], preferred_element_type=F32) * q_scale
        q_ref[:, lo:lo + LANES] = q_h[:, :LANES].astype(q_ref.dtype)
        q_ref[:, lo + LANES:lo + 2 * LANES] = _rope_lanes(q_h[:, LANES:], cos, sin_lo, sin_hi).astype(q_ref.dtype)
        kv_h = jnp.dot(kv_lat, wkv_ref[:, lo:lo + MLA_QK_GROUP], preferred_element_type=F32)
        k_ref[:, lo:lo + LANES] = kv_h[:, :LANES].astype(k_ref.dtype)
        k_ref[:, lo + LANES:lo + 2 * LANES] = k_rope
        v_ref[:, h * MLA_V_DIM:(h + 1) * MLA_V_DIM] = kv_h[:, LANES:].astype(v_ref.dtype)


def _rope_tables(seq):
    half = MLA_ROPE_DIM // 2
    inv = ROPE_THETA ** (-jnp.arange(half, dtype=F32) / half)
    ang = jnp.arange(seq, dtype=F32)[:, None] * inv[None, :]
    cos, sin = jnp.cos(ang), jnp.sin(ang)
    zero = jnp.zeros_like(cos)
    cos_t = jnp.concatenate([cos, cos, zero, zero], axis=1)
    sin_lo = jnp.concatenate([-sin, zero, zero, zero], axis=1)
    sin_hi = jnp.concatenate([zero, sin, zero, zero], axis=1)
    return cos_t, sin_lo, sin_hi


def _window_row(values, lo, width):
    return jnp.pad(values.astype(F32), (lo, width - lo - values.shape[0])).reshape(1, width)


def _mla_prep(pb2d, q_norm_g, kv_norm_g, w_uq, w_ukv, *, seq, tm=512):
    n_tok = pb2d.shape[0]
    dq = MLA_NOPE_DIM + MLA_ROPE_DIM
    xb_width = 512
    xb_start = (PB_CKV // xb_width) * xb_width
    assert PB_VALID <= xb_start + xb_width and PB_COLS % xb_width == 0
    xa_cols = -(-PB_CKV // LANES) * LANES
    ckv_lo = PB_CKV - xb_start
    kv_cols = -(-(ckv_lo + KV_LORA_RANK) // LANES) * LANES
    kr_lo = PB_KR - xb_start
    assert kr_lo % LANES + MLA_ROPE_DIM <= LANES
    wq = w_uq.reshape(Q_LORA_RANK, MLA_HEADS, dq)
    wq = jnp.pad(wq, ((PB_CQ, xa_cols - PB_CKV), (0, 0), (0, MLA_QK_GROUP - dq)))
    wq = wq.reshape(xa_cols, MLA_HEADS * MLA_QK_GROUP).astype(BF16)
    wkv = jnp.pad(w_ukv, ((ckv_lo, kv_cols - ckv_lo - KV_LORA_RANK), (0, 0))).astype(BF16)
    q_ind = _window_row(jnp.ones((Q_LORA_RANK,), F32), PB_CQ, xa_cols)
    q_gain = _window_row(q_norm_g, PB_CQ, xa_cols)
    kv_ind = _window_row(jnp.ones((KV_LORA_RANK,), F32), ckv_lo, kv_cols)
    kv_gain = _window_row(kv_norm_g, ckv_lo, kv_cols)
    cos_t, sin_lo, sin_hi = _rope_tables(seq)
    tiles_per_seq = seq // tm
    pos = lambda i: (i % tiles_per_seq, 0)
    kern = functools.partial(_mla_prep_kernel, q_scale=math.log2(math.e) / math.sqrt(dq),
                             kv_cols=kv_cols, kr_group=kr_lo // LANES, kr_shift=kr_lo % LANES)
    qk_cols = MLA_HEADS * MLA_QK_GROUP
    const = lambda i: (0, 0)
    return pl.pallas_call(
        kern,
        grid=(n_tok // tm,),
        in_specs=[
            pl.BlockSpec((tm, xa_cols), lambda i: (i, 0)),
            pl.BlockSpec((tm, xb_width), lambda i: (i, xb_start // xb_width)),
            pl.BlockSpec((1, xa_cols), const),
            pl.BlockSpec((1, xa_cols), const),
            pl.BlockSpec((1, kv_cols), const),
            pl.BlockSpec((1, kv_cols), const),
            pl.BlockSpec((xa_cols, qk_cols), const),
            pl.BlockSpec((kv_cols, qk_cols), const),
            pl.BlockSpec((tm, LANES), pos),
            pl.BlockSpec((tm, LANES), pos),
            pl.BlockSpec((tm, LANES), pos),
        ],
        out_specs=[
            pl.BlockSpec((tm, qk_cols), lambda i: (i, 0)),
            pl.BlockSpec((tm, qk_cols), lambda i: (i, 0)),
            pl.BlockSpec((tm, MLA_HEADS * MLA_V_DIM), lambda i: (i, 0)),
        ],
        out_shape=[
            jax.ShapeDtypeStruct((n_tok, qk_cols), BF16),
            jax.ShapeDtypeStruct((n_tok, qk_cols), BF16),
            jax.ShapeDtypeStruct((n_tok, MLA_HEADS * MLA_V_DIM), BF16),
        ],
        compiler_params=_params(("arbitrary",)),
        name="mla_prep",
    )(pb2d, pb2d, q_ind, q_gain, kv_ind, kv_gain, wq, wkv, cos_t, sin_lo, sin_hi)


def _flash_kernel(q_ref, k_ref, v_ref, o_ref, m_ref, acc_ref, *, tq, heads):
    qi = pl.program_id(2)
    m_ref[...] = jnp.full_like(m_ref, NEG_BIG)
    acc_ref[...] = jnp.zeros_like(acc_ref)
    ones_col = (lax.broadcasted_iota(jnp.int32, (tq, LANES), 1) == 0).astype(BF16)
    on_or_below_diag = (lax.broadcasted_iota(jnp.int32, (tq, tq), 1)
                        <= lax.broadcasted_iota(jnp.int32, (tq, tq), 0))

    def key_block(j, masked):
        rows = pl.ds(pl.multiple_of(j * tq, tq), tq)
        for h in range(heads):
            q = q_ref[0, :, h * MLA_QK_GROUP:(h + 1) * MLA_QK_GROUP]
            k = k_ref[0, rows, h * MLA_QK_GROUP:(h + 1) * MLA_QK_GROUP]
            s = lax.dot_general(q, k, (((1,), (1,)), ((), ())), preferred_element_type=F32)
            if masked:
                s = jnp.where(on_or_below_diag, s, NEG_BIG)
            m_prev = m_ref[h]
            m_new = jnp.maximum(m_prev, jnp.max(s, axis=-1, keepdims=True))
            p = jnp.exp2(s - jnp.concatenate([m_new] * (tq // LANES), axis=1)).astype(BF16)
            alpha = jnp.exp2(m_prev - m_new)
            v_aug = jnp.concatenate([v_ref[0, rows, h * MLA_V_DIM:(h + 1) * MLA_V_DIM], ones_col], axis=1)
            pv = jnp.dot(p, v_aug, preferred_element_type=F32)
            acc_ref[h] = jnp.concatenate([alpha, alpha], axis=1) * acc_ref[h] + pv
            m_ref[h] = m_new

    def body(j, carry):
        key_block(j, False)
        return carry

    lax.fori_loop(0, qi, body, 0)
    key_block(qi, True)
    for h in range(heads):
        acc = acc_ref[h]
        o_ref[0, :, h * MLA_V_DIM:(h + 1) * MLA_V_DIM] = (
            acc[:, :MLA_V_DIM] / acc[:, MLA_V_DIM:MLA_V_DIM + 1]).astype(o_ref.dtype)


def _flash(q, k, v, *, tq=512, heads=4):
    bsz, seq, _ = q.shape
    kern = functools.partial(_flash_kernel, tq=tq, heads=heads)
    return pl.pallas_call(
        kern,
        grid=(bsz, MLA_HEADS // heads, seq // tq),
        in_specs=[
            pl.BlockSpec((1, tq, heads * MLA_QK_GROUP), lambda b, g, qi: (b, qi, g)),
            pl.BlockSpec((1, seq, heads * MLA_QK_GROUP), lambda b, g, qi: (b, 0, g)),
            pl.BlockSpec((1, seq, heads * MLA_V_DIM), lambda b, g, qi: (b, 0, g)),
        ],
        out_specs=pl.BlockSpec((1, tq, heads * MLA_V_DIM), lambda b, g, qi: (b, qi, g)),
        out_shape=jax.ShapeDtypeStruct((bsz, seq, MLA_HEADS * MLA_V_DIM), BF16),
        scratch_shapes=[
            pltpu.VMEM((heads, tq, LANES), F32),
            pltpu.VMEM((heads, tq, MLA_V_DIM + LANES), F32),
        ],
        compiler_params=_params(("arbitrary", "arbitrary", "arbitrary")),
        name="flash",
    )(q, k, v)


def _outproj_kernel(x_ref, hml_ref, hmla_ref, mod_ref, w_ref, g_ref, b_ref, o_ref, wb_ref, *,
                    sub, tiles_per_batch, alpha, row_chunk, norm_rows):
    i = pl.program_id(0)
    bidx = i // tiles_per_batch
    k_ml = hml_ref.shape[1]

    @pl.when(i == 0)
    def _():
        def body(r, carry):
            rows = pl.ds(pl.multiple_of(r * row_chunk, row_chunk), row_chunk)
            wb_ref[rows, :] = w_ref[rows, :].astype(BF16)
            return carry

        lax.fori_loop(0, w_ref.shape[0] // row_chunk, body, 0)

    gate = 1.0 + mod_ref[bidx, 3 * sub + 2:3 * sub + 3, :]
    g = g_ref[...]
    b_ln = b_ref[...]
    w_top = wb_ref[0:k_ml, :]
    w_bot = wb_ref[k_ml:, :]
    for r in range(x_ref.shape[0] // norm_rows):
        rows = slice(r * norm_rows, (r + 1) * norm_rows)
        y = (jnp.dot(hml_ref[rows, :], w_top, preferred_element_type=F32)
             + jnp.dot(hmla_ref[rows, :], w_bot, preferred_element_type=F32))
        o_ref[rows, :] = _layer_norm_rows(alpha * x_ref[rows, :] + gate * y, g, b_ln)


def _outproj(x2d, hml, hmla, mod, w_out, ln_g, ln_b, *, sub, seq, alpha, tm=512, row_chunk=128,
             norm_rows=256):
    n_tok, d = x2d.shape
    kern = functools.partial(_outproj_kernel, sub=sub, tiles_per_batch=seq // tm, alpha=alpha,
                             row_chunk=row_chunk, norm_rows=norm_rows)
    return pl.pallas_call(
        kern,
        grid=(n_tok // tm,),
        in_specs=[
            pl.BlockSpec((tm, d), lambda i: (i, 0)),
            pl.BlockSpec((tm, hml.shape[1]), lambda i: (i, 0)),
            pl.BlockSpec((tm, hmla.shape[1]), lambda i: (i, 0)),
            pl.BlockSpec(mod.shape, lambda i: (0, 0, 0)),
            pl.BlockSpec(w_out.shape, lambda i: (0, 0), pipeline_mode=pl.Buffered(1)),
            pl.BlockSpec((1, d), lambda i: (0, 0)),
            pl.BlockSpec((1, d), lambda i: (0, 0)),
        ],
        out_specs=pl.BlockSpec((tm, d), lambda i: (i, 0)),
        out_shape=jax.ShapeDtypeStruct((n_tok, d), F32),
        scratch_shapes=[pltpu.VMEM(w_out.shape, BF16)],
        compiler_params=_params(("arbitrary",)),
        name="outproj",
    )(x2d, hml, hmla, mod, w_out, ln_g.reshape(1, d), ln_b.reshape(1, d))


def kernel(x, c, w_ada, b_ada, ffn1_w1, ffn1_w3, ffn1_w2, ln1_g, ln1_b, w_in, conv_w, conv_b, b_igate, b_fgate, ml_norm_g, q_norm_g, w_uq, kv_norm_g, w_ukv, w_out, ln2_g, ln2_b, ffn2_w1, ffn2_w3, ffn2_w2, ln3_g, ln3_b):
    bsz, seq, d = x.shape
    depth = w_ada.shape[0]
    alpha = (2.0 * depth) ** 0.25
    h2d = x.reshape(bsz * seq, d)
    for l in range(depth):
        mod = _adaln(c, w_ada[l], b_ada[l]).reshape(bsz, N_SUBLAYERS * 3, d)
        h2d = _ffn(h2d, mod, ffn1_w1[l], ffn1_w3[l], ffn1_w2[l], ln1_g[l], ln1_b[l],
                   sub=0, seq=seq, alpha=alpha)
        assert w_in.shape[2] == D_IN
        pa, pb = _inproj(h2d, mod, w_in[l].T, sub=1, seq=seq)
        hml = _mlstm(pa.reshape(bsz, seq, PA_COLS), pb.reshape(bsz, seq, PB_COLS), conv_w[l], conv_b[l],
                     b_igate[l], b_fgate[l], ml_norm_g[l])
        q, k, v = _mla_prep(pb, q_norm_g[l], kv_norm_g[l], w_uq[l], w_ukv[l], seq=seq)
        hmla = _flash(q.reshape(bsz, seq, -1), k.reshape(bsz, seq, -1), v.reshape(bsz, seq, -1))
        h2d = _outproj(h2d, hml.reshape(bsz * seq, -1), hmla.reshape(bsz * seq, -1), mod,
                       w_out[l], ln2_g[l], ln2_b[l], sub=1, seq=seq, alpha=alpha)
        h2d = _ffn(h2d, mod, ffn2_w1[l], ffn2_w3[l], ffn2_w2[l], ln3_g[l], ln3_b[l],
                   sub=2, seq=seq, alpha=alpha)
    return h2d.reshape(bsz, seq, d)
```

```python
import functools
import math

import jax
import jax.numpy as jnp
from jax import lax
from jax.experimental import pallas as pl
from jax.experimental.pallas import tpu as pltpu

F32 = jnp.float32
BF16 = jnp.bfloat16

ML_HEADS = 4
ML_QK_DIM = 128
ML_V_DIM = 256
CONV_WIDTH = 4
MLA_HEADS = 8
MLA_NOPE_DIM = 128
MLA_ROPE_DIM = 64
MLA_V_DIM = 128
Q_LORA_RANK = 512
KV_LORA_RANK = 256
ROPE_THETA = 10000.0
FFN_RES_WEIGHT = 0.5
N_SUBLAYERS = 3
LN_EPS = 1e-5

LANES = 128
SUBLANES = 8
VMEM_LIMIT_BYTES = 56 * 1024 * 1024

ML_QK_COLS = 2 * ML_HEADS * ML_QK_DIM
ML_V_COLS = ML_HEADS * ML_V_DIM
PA_COLS = ML_QK_COLS + 2 * ML_V_COLS
PB_GATES = 0
PB_CQ = PB_GATES + 2 * ML_HEADS
PB_CKV = PB_CQ + Q_LORA_RANK
PB_KR = PB_CKV + KV_LORA_RANK
PB_VALID = PB_KR + MLA_ROPE_DIM
PB_COLS = -(-PB_VALID // 512) * 512
D_IN = PA_COLS + PB_VALID
MLA_QK_GROUP = 2 * LANES
NEG_BIG = -1e30


def _params(semantics):
    return pltpu.CompilerParams(dimension_semantics=semantics, vmem_limit_bytes=VMEM_LIMIT_BYTES)


def _silu(v):
    return v * jax.nn.sigmoid(v)


def _layer_norm_rows(z, g, b):
    mu = jnp.mean(z, axis=-1, keepdims=True)
    zc = z - mu
    var = jnp.mean(zc * zc, axis=-1, keepdims=True)
    return zc * lax.rsqrt(var + LN_EPS) * g + b


def _adaln_kernel(c_ref, w_ref, b_ref, o_ref):
    sc = _silu(c_ref[...]).astype(BF16)
    o_ref[...] = jnp.dot(sc, w_ref[...].astype(BF16), preferred_element_type=F32) + b_ref[...]


def _adaln(c, w, b, *, tn=1024):
    bsz, d = c.shape
    n = w.shape[1]
    rows = -(-bsz // SUBLANES) * SUBLANES
    c_pad = jnp.pad(c, ((0, rows - bsz), (0, 0)))
    out = pl.pallas_call(
        _adaln_kernel,
        grid=(n // tn,),
        in_specs=[
            pl.BlockSpec((rows, d), lambda j: (0, 0)),
            pl.BlockSpec((d, tn), lambda j: (0, j)),
            pl.BlockSpec((1, tn), lambda j: (0, j)),
        ],
        out_specs=pl.BlockSpec((rows, tn), lambda j: (0, j)),
        out_shape=jax.ShapeDtypeStruct((rows, n), F32),
        compiler_params=_params(("arbitrary",)),
        name="adaln",
    )(c_pad, w, b.reshape(1, n))
    return out[:bsz]


def _ffn_kernel(x_ref, mod_ref, g_ref, b_ref, w1_hbm, w3_hbm, w2_hbm, o_ref,
                w1_buf, w3_buf, w2_buf, sem, *, sub, tiles_per_batch, alpha, finish_rows, tf):
    i = pl.program_id(0)
    n_tiles = pl.num_programs(0)
    bidx = i // tiles_per_batch
    tm = x_ref.shape[0]
    n_f = w1_hbm.shape[1] // tf

    def chunk_copies(c, slot):
        cols = pl.ds(pl.multiple_of(c * tf, tf), tf)
        return (pltpu.make_async_copy(w1_hbm.at[:, cols], w1_buf.at[slot], sem.at[0, slot]),
                pltpu.make_async_copy(w3_hbm.at[:, cols], w3_buf.at[slot], sem.at[1, slot]),
                pltpu.make_async_copy(w2_hbm.at[cols, :], w2_buf.at[slot], sem.at[2, slot]))

    def start_chunk(c, slot):
        for cp in chunk_copies(c, slot):
            cp.start()

    def wait_chunk(c, slot):
        for cp in chunk_copies(c, slot):
            cp.wait()

    @pl.when(i == 0)
    def _():
        start_chunk(0, 0)

    shift = mod_ref[bidx, 3 * sub:3 * sub + 1, :]
    scale1 = 1.0 + mod_ref[bidx, 3 * sub + 1:3 * sub + 2, :]

    def partial_out(x_rows, w1, w3, w2):
        u = (x_rows * scale1 + shift).astype(BF16)
        a = jnp.dot(u, w1, preferred_element_type=F32)
        b = jnp.dot(u, w3, preferred_element_type=F32)
        h = (_silu(a) * b).astype(BF16)
        return jnp.dot(h, w2, preferred_element_type=F32)

    def chunk_weights(slot):
        return (w1_buf[slot].astype(BF16), w3_buf[slot].astype(BF16), w2_buf[slot].astype(BF16))

    start_chunk(1, 1)
    wait_chunk(0, 0)
    o_ref[...] = partial_out(x_ref[...], *chunk_weights(0))

    def chunk_pair(k, carry):
        c1 = 2 * k + 1
        start_chunk(c1 + 1, 0)
        wait_chunk(c1, 1)
        o_ref[...] += partial_out(x_ref[...], *chunk_weights(1))
        start_chunk(c1 + 2, 1)
        wait_chunk(c1 + 1, 0)
        o_ref[...] += partial_out(x_ref[...], *chunk_weights(0))
        return carry

    lax.fori_loop(0, (n_f - 2) // 2, chunk_pair, 0)

    @pl.when(i + 1 < n_tiles)
    def _():
        start_chunk(0, 0)

    wait_chunk(n_f - 1, 1)
    gate = FFN_RES_WEIGHT * (1.0 + mod_ref[bidx, 3 * sub + 2:3 * sub + 3, :])
    g = g_ref[...]
    b_ln = b_ref[...]
    last_w = chunk_weights(1)
    for r in range(tm // finish_rows):
        rows = slice(r * finish_rows, (r + 1) * finish_rows)
        x_rows = x_ref[rows, :]
        y = o_ref[rows, :] + partial_out(x_rows, *last_w)
        o_ref[rows, :] = _layer_norm_rows(alpha * x_rows + gate * y, g, b_ln)


def _ffn(x2d, mod, w1, w3, w2, ln_g, ln_b, *, sub, seq, alpha, tm=1024, tf=256, finish_rows=256):
    n_tok, d = x2d.shape
    f = w1.shape[1]
    tm = min(tm, seq)
    assert f % (2 * tf) == 0 and f // tf >= 4 and tm % finish_rows == 0
    kern = functools.partial(_ffn_kernel, sub=sub, tiles_per_batch=seq // tm, alpha=alpha,
                             finish_rows=finish_rows, tf=tf)
    hbm = pl.BlockSpec(memory_space=pl.ANY)
    return pl.pallas_call(
        kern,
        grid=(n_tok // tm,),
        in_specs=[
            pl.BlockSpec((tm, d), lambda i: (i, 0)),
            pl.BlockSpec(mod.shape, lambda i: (0, 0, 0)),
            pl.BlockSpec((1, d), lambda i: (0, 0)),
            pl.BlockSpec((1, d), lambda i: (0, 0)),
            hbm, hbm, hbm,
        ],
        out_specs=pl.BlockSpec((tm, d), lambda i: (i, 0)),
        out_shape=jax.ShapeDtypeStruct((n_tok, d), F32),
        scratch_shapes=[
            pltpu.VMEM((2, d, tf), w1.dtype),
            pltpu.VMEM((2, d, tf), w3.dtype),
            pltpu.VMEM((2, tf, d), w2.dtype),
            pltpu.SemaphoreType.DMA((3, 2)),
        ],
        compiler_params=_params(("arbitrary",)),
        name=f"ffn{sub}",
    )(x2d, mod, ln_g.reshape(1, d), ln_b.reshape(1, d), w1, w3, w2)


def _inproj_kernel(x_ref, mod_ref, w_ref, oa_ref, ob_ref, wbf_ref, *,
                   sub, tiles_per_batch, n_a, n_valid):
    i = pl.program_id(0)
    j = pl.program_id(1)
    bidx = i // tiles_per_batch
    tn = w_ref.shape[0]

    @pl.when(i == 0)
    def _():
        out_col = j * tn + lax.broadcasted_iota(jnp.int32, w_ref.shape, 0)
        wbf_ref[j] = jnp.where(out_col < n_valid, w_ref[...], 0.0).astype(BF16)

    def project():
        shift = mod_ref[bidx, 3 * sub:3 * sub + 1, :]
        scale1 = 1.0 + mod_ref[bidx, 3 * sub + 1:3 * sub + 2, :]
        u = (x_ref[...] * scale1 + shift).astype(BF16)
        return lax.dot_general(u, wbf_ref[j], (((1,), (1,)), ((), ())), preferred_element_type=F32)

    @pl.when(j < n_a)
    def _():
        oa_ref[...] = project().astype(BF16)

    @pl.when(j >= n_a)
    def _():
        ob_ref[...] = project()


def _inproj(x2d, mod, w_in_t, *, sub, seq, tm=1024, tn=512):
    n_tok, d = x2d.shape
    n_a = PA_COLS // tn
    n_b = PB_COLS // tn
    n_blocks = n_a + n_b
    kern = functools.partial(_inproj_kernel, sub=sub, tiles_per_batch=seq // tm, n_a=n_a,
                             n_valid=w_in_t.shape[0])
    return pl.pallas_call(
        kern,
        grid=(n_tok // tm, n_blocks),
        in_specs=[
            pl.BlockSpec((tm, d), lambda i, j: (i, 0)),
            pl.BlockSpec(mod.shape, lambda i, j: (0, 0, 0)),
            pl.BlockSpec((tn, d), lambda i, j: (jnp.where(i == 0, j, n_blocks - 1), 0)),
        ],
        out_specs=[
            pl.BlockSpec((tm, tn), lambda i, j: (i, jnp.minimum(j, n_a - 1))),
            pl.BlockSpec((tm, tn), lambda i, j: (i, jnp.maximum(j - n_a, 0))),
        ],
        out_shape=[
            jax.ShapeDtypeStruct((n_tok, PA_COLS), BF16),
            jax.ShapeDtypeStruct((n_tok, PB_COLS), F32),
        ],
        scratch_shapes=[pltpu.VMEM((n_blocks, tn, d), BF16)],
        compiler_params=_params(("arbitrary", "arbitrary")),
        name="inproj",
    )(x2d, mod, w_in_t)


def _lane_scan(v, op, chunk):
    lane = lax.broadcasted_iota(jnp.int32, v.shape, 1) % chunk
    ident = 0.0 if op is jnp.add else NEG_BIG
    s = 1
    while s < chunk:
        shifted = pltpu.roll(v, s, axis=1)
        v = op(v, jnp.where(lane >= s, shifted, ident))
        s *= 2
    return v


def _mlstm_gate_scans(bi, gates_ref, gb_ref, scan_ref, chunk):
    seq = gates_ref.shape[1]
    for c in range(seq // LANES):
        cols = slice(c * LANES, (c + 1) * LANES)
        scan_ref[bi, 0, :, cols] = gates_ref[bi, cols, :].T[0:SUBLANES, :]
    z = scan_ref[bi, 0] + gb_ref[:, 0:1]
    logi = pltpu.roll(z, ML_HEADS, axis=0)
    logf = jnp.minimum(z, 0.0) - jnp.log1p(jnp.exp(-jnp.abs(z)))
    bcum = _lane_scan(logf, jnp.add, chunk)
    r = logi - bcum
    scan_ref[bi, 0] = bcum
    scan_ref[bi, 1] = r
    scan_ref[bi, 2] = _lane_scan(r, jnp.maximum, chunk)


def _mlstm_kernel(qk_ref, v_ref, og_ref, gates_ref, cw_ref, cb_ref, gb_ref, ng_ref, sel_ref, cbias_ref,
                  o_ref, c_ref, m_ref, prev_ref, scan_ref, *, chunk):
    t = pl.program_id(1)

    @pl.when(t == 0)
    def _():
        c_ref[...] = jnp.zeros_like(c_ref)
        m_ref[...] = jnp.zeros_like(m_ref)
        prev_ref[...] = jnp.zeros_like(prev_ref)
        for bi in range(qk_ref.shape[0]):
            _mlstm_gate_scans(bi, gates_ref, gb_ref, scan_ref, chunk)

    _mlstm_step(t, qk_ref, v_ref, og_ref, cw_ref, cb_ref, ng_ref, sel_ref, cbias_ref, o_ref,
                c_ref, m_ref, prev_ref, scan_ref, chunk)


def _mlstm_step(t, qk_ref, v_ref, og_ref, cw_ref, cb_ref, ng_ref, sel_ref, cbias_ref, o_ref,
                c_ref, m_ref, prev_ref, scan_ref, chunk):
    hq = ML_HEADS * ML_QK_DIM
    n_rows = qk_ref.shape[0]
    chains = [(bi, h) for bi in range(n_rows) for h in range(ML_HEADS)]
    head_cols = lambda h: slice(h * ML_V_DIM, (h + 1) * ML_V_DIM)
    causal_bias = cbias_ref[...]
    ones_col = (lax.broadcasted_iota(jnp.int32, (chunk, LANES), 1) == 0).astype(BF16)
    lanes = pl.ds(pl.multiple_of(t * chunk, chunk), chunk)

    qk = []
    for bi in range(n_rows):
        qk_now = qk_ref[bi]
        taps = jnp.dot(sel_ref[...], jnp.concatenate([prev_ref[bi], qk_now], axis=0),
                       preferred_element_type=F32)
        prev_ref[bi] = qk_now
        conv = cb_ref[...]
        for tap in range(CONV_WIDTH):
            conv = conv + cw_ref[tap:tap + 1, :] * taps[tap * chunk:(tap + 1) * chunk, :]
        qk.append(_silu(conv))

    r, cols, s_old = [], [], []
    for bi in range(n_rows):
        bcum = scan_ref[bi, 0, :, lanes]
        r_b = scan_ref[bi, 1, :, lanes]
        cm = scan_ref[bi, 2, :, lanes]
        m_in = m_ref[bi, :, 0:1]
        m_all = jnp.maximum(m_in, cm)
        m_last = m_all[:, chunk - 1:chunk]
        m_ref[bi] = jnp.broadcast_to(bcum[:, chunk - 1:chunk] + m_last, (SUBLANES, LANES))
        s_inter = jnp.exp(m_in - m_all)
        e_neg_m = jnp.exp(-(bcum + m_all))
        w_last = jnp.exp(r_b - m_last)
        stack = jnp.concatenate(
            [m_all, s_inter, e_neg_m, w_last, jnp.zeros((LANES - 4 * SUBLANES, chunk), F32)], axis=0)
        r.append(r_b)
        cols.append(stack.T)
        s_old.append(jnp.exp(m_in - m_last))

    def col(bi, h, which):
        lane = which * SUBLANES + ML_HEADS + h
        return cols[bi][:, lane:lane + 1]

    q_h = {c: qk[c[0]][:, c[1] * ML_QK_DIM:(c[1] + 1) * ML_QK_DIM].astype(BF16) for c in chains}
    k_f = {c: qk[c[0]][:, hq + c[1] * ML_QK_DIM:hq + (c[1] + 1) * ML_QK_DIM] * (ML_QK_DIM ** -0.5)
           for c in chains}
    v_aug = {c: jnp.concatenate([v_ref[c[0], :, head_cols(c[1])], ones_col], axis=1) for c in chains}
    scores = {c: lax.dot_general(q_h[c], k_f[c].astype(BF16), (((1,), (1,)), ((), ())),
                                 preferred_element_type=F32) for c in chains}
    p = {}
    for bi, h in chains:
        decay = jnp.exp(r[bi][ML_HEADS + h:ML_HEADS + h + 1, :] - col(bi, h, 0) + causal_bias)
        p[bi, h] = (decay * scores[bi, h]).astype(BF16)
    c_old = {c: c_ref[c[0], c[1]] for c in chains}
    tot = {c: (jnp.dot(p[c], v_aug[c], preferred_element_type=F32)
               + col(c[0], c[1], 1) * jnp.dot(q_h[c], c_old[c].astype(BF16), preferred_element_type=F32))
           for c in chains}
    for bi, h in chains:
        k_w = (k_f[bi, h] * col(bi, h, 3)).astype(BF16)
        c_ref[bi, h] = s_old[bi][ML_HEADS + h:ML_HEADS + h + 1, :] * c_old[bi, h] + lax.dot_general(
            k_w, v_aug[bi, h], (((0,), (0,)), ((), ())), preferred_element_type=F32)
    for bi, h in chains:
        num = tot[bi, h][:, :ML_V_DIM]
        den = tot[bi, h][:, ML_V_DIM:ML_V_DIM + 1]
        hid = num / jnp.maximum(jnp.abs(den), col(bi, h, 2))
        mu = jnp.mean(hid, axis=-1, keepdims=True)
        hc = hid - mu
        var = jnp.mean(hc * hc, axis=-1, keepdims=True)
        hn = hc * lax.rsqrt(var + LN_EPS) * ng_ref[:, head_cols(h)]
        gate = jax.nn.sigmoid(og_ref[bi, :, head_cols(h)].astype(F32))
        o_ref[bi, :, head_cols(h)] = (hn * gate).astype(o_ref.dtype)


def _mlstm(pa, pb, conv_w, conv_b, b_igate, b_fgate, norm_g, *, chunk=128, rows_per_step=2):
    bsz, seq, _ = pa.shape
    rb = rows_per_step
    gate_bias = jnp.broadcast_to(jnp.concatenate([b_igate, b_fgate])[:, None], (SUBLANES, LANES))
    t_idx = jnp.arange(CONV_WIDTH * chunk) % chunk
    tap_idx = jnp.arange(CONV_WIDTH * chunk) // chunk
    shift_sel = (jnp.arange(2 * chunk)[None, :]
                 == (chunk + t_idx - (CONV_WIDTH - 1) + tap_idx)[:, None]).astype(BF16)
    causal_bias = jnp.where(jnp.arange(chunk)[None, :] <= jnp.arange(chunk)[:, None], 0.0, NEG_BIG).astype(F32)
    kern = functools.partial(_mlstm_kernel, chunk=chunk)
    return pl.pallas_call(
        kern,
        grid=(bsz // rb, seq // chunk),
        in_specs=[
            pl.BlockSpec((rb, chunk, ML_QK_COLS), lambda b, t: (b, t, 0)),
            pl.BlockSpec((rb, chunk, ML_V_COLS), lambda b, t: (b, t, ML_QK_COLS // ML_V_COLS)),
            pl.BlockSpec((rb, chunk, ML_V_COLS), lambda b, t: (b, t, ML_QK_COLS // ML_V_COLS + 1)),
            pl.BlockSpec((rb, seq, LANES), lambda b, t: (b, 0, PB_GATES // LANES)),
            pl.BlockSpec((CONV_WIDTH, ML_QK_COLS), lambda b, t: (0, 0)),
            pl.BlockSpec((1, ML_QK_COLS), lambda b, t: (0, 0)),
            pl.BlockSpec((SUBLANES, LANES), lambda b, t: (0, 0)),
            pl.BlockSpec((1, ML_V_COLS), lambda b, t: (0, 0)),
            pl.BlockSpec(shift_sel.shape, lambda b, t: (0, 0)),
            pl.BlockSpec(causal_bias.shape, lambda b, t: (0, 0)),
        ],
        out_specs=pl.BlockSpec((rb, chunk, ML_V_COLS), lambda b, t: (b, t, 0)),
        out_shape=jax.ShapeDtypeStruct((bsz, seq, ML_V_COLS), BF16),
        scratch_shapes=[
            pltpu.VMEM((rb, ML_HEADS, ML_QK_DIM, ML_V_DIM + LANES), F32),
            pltpu.VMEM((rb, SUBLANES, LANES), F32),
            pltpu.VMEM((rb, chunk, ML_QK_COLS), pa.dtype),
            pltpu.VMEM((rb, 3, SUBLANES, seq), F32),
        ],
        compiler_params=_params(("arbitrary", "arbitrary")),
        name="mlstm",
    )(pa, pa, pa, pb, conv_w, conv_b.reshape(1, -1), gate_bias, norm_g.reshape(1, -1),
      shift_sel, causal_bias)


def _rope_lanes(x, cos, sin_lo, sin_hi):
    half = MLA_ROPE_DIM // 2
    return x * cos + pltpu.roll(x, LANES - half, axis=1) * sin_lo + pltpu.roll(x, half, axis=1) * sin_hi


def _rms_window(x, ind, g, n):
    ssq = jnp.sum(jnp.square(x * ind), axis=-1, keepdims=True)
    return x * lax.rsqrt(ssq * (1.0 / n) + LN_EPS) * g


def _mla_prep_kernel(xa_ref, xb_ref, qi_ref, qg_ref, kvi_ref, kvg_ref, wq_ref, wkv_ref,
                     cos_ref, sl_ref, sh_ref, q_ref, k_ref, v_ref, *, q_scale, kv_cols, kr_group, kr_shift):
    cos = cos_ref[...]
    sin_lo = sl_ref[...]
    sin_hi = sh_ref[...]
    q_lat = _rms_window(xa_ref[...], qi_ref[...], qg_ref[...], Q_LORA_RANK).astype(BF16)
    kv_lat = _rms_window(xb_ref[:, :kv_cols], kvi_ref[...], kvg_ref[...], KV_LORA_RANK).astype(BF16)
    k_r = pltpu.roll(xb_ref[:, kr_group * LANES:(kr_group + 1) * LANES], LANES - kr_shift, axis=1)
    k_rope = _rope_lanes(k_r, cos, sin_lo, sin_hi).astype(k_ref.dtype)
    for h in range(MLA_HEADS):
        lo = h * MLA_QK_GROUP
        q_h = jnp.dot(q_lat, wq_ref[:, lo:lo + MLA_QK_GROUP], preferred_element_type=F32) * q_scale
        q_ref[:, lo:lo + LANES] = q_h[:, :LANES].astype(q_ref.dtype)
        q_ref[:, lo + LANES:lo + 2 * LANES] = _rope_lanes(q_h[:, LANES:], cos, sin_lo, sin_hi).astype(q_ref.dtype)
        kv_h = jnp.dot(kv_lat, wkv_ref[:, lo:lo + MLA_QK_GROUP], preferred_element_type=F32)
        k_ref[:, lo:lo + LANES] = kv_h[:, :LANES].astype(k_ref.dtype)
        k_ref[:, lo + LANES:lo + 2 * LANES] = k_rope
        v_ref[:, h * MLA_V_DIM:(h + 1) * MLA_V_DIM] = kv_h[:, LANES:].astype(v_ref.dtype)


def _rope_tables(seq):
    half = MLA_ROPE_DIM // 2
    inv = ROPE_THETA ** (-jnp.arange(half, dtype=F32) / half)
    ang = jnp.arange(seq, dtype=F32)[:, None] * inv[None, :]
    cos, sin = jnp.cos(ang), jnp.sin(ang)
    zero = jnp.zeros_like(cos)
    cos_t = jnp.concatenate([cos, cos, zero, zero], axis=1)
    sin_lo = jnp.concatenate([-sin, zero, zero, zero], axis=1)
    sin_hi = jnp.concatenate([zero, sin, zero, zero], axis=1)
    return cos_t, sin_lo, sin_hi


def _window_row(values, lo, width):
    return jnp.pad(values.astype(F32), (lo, width - lo - values.shape[0])).reshape(1, width)


def _mla_prep(pb2d, q_norm_g, kv_norm_g, w_uq, w_ukv, *, seq, tm=512):
    n_tok = pb2d.shape[0]
    dq = MLA_NOPE_DIM + MLA_ROPE_DIM
    xb_width = 512
    xb_start = (PB_CKV // xb_width) * xb_width
    assert PB_VALID <= xb_start + xb_width and PB_COLS % xb_width == 0
    xa_cols = -(-PB_CKV // LANES) * LANES
    ckv_lo = PB_CKV - xb_start
    kv_cols = -(-(ckv_lo + KV_LORA_RANK) // LANES) * LANES
    kr_lo = PB_KR - xb_start
    assert kr_lo % LANES + MLA_ROPE_DIM <= LANES
    wq = w_uq.reshape(Q_LORA_RANK, MLA_HEADS, dq)
    wq = jnp.pad(wq, ((PB_CQ, xa_cols - PB_CKV), (0, 0), (0, MLA_QK_GROUP - dq)))
    wq = wq.reshape(xa_cols, MLA_HEADS * MLA_QK_GROUP).astype(BF16)
    wkv = jnp.pad(w_ukv, ((ckv_lo, kv_cols - ckv_lo - KV_LORA_RANK), (0, 0))).astype(BF16)
    q_ind = _window_row(jnp.ones((Q_LORA_RANK,), F32), PB_CQ, xa_cols)
    q_gain = _window_row(q_norm_g, PB_CQ, xa_cols)
    kv_ind = _window_row(jnp.ones((KV_LORA_RANK,), F32), ckv_lo, kv_cols)
    kv_gain = _window_row(kv_norm_g, ckv_lo, kv_cols)
    cos_t, sin_lo, sin_hi = _rope_tables(seq)
    tiles_per_seq = seq // tm
    pos = lambda i: (i % tiles_per_seq, 0)
    kern = functools.partial(_mla_prep_kernel, q_scale=math.log2(math.e) / math.sqrt(dq),
                             kv_cols=kv_cols, kr_group=kr_lo // LANES, kr_shift=kr_lo % LANES)
    qk_cols = MLA_HEADS * MLA_QK_GROUP
    const = lambda i: (0, 0)
    return pl.pallas_call(
        kern,
        grid=(n_tok // tm,),
        in_specs=[
            pl.BlockSpec((tm, xa_cols), lambda i: (i, 0)),
            pl.BlockSpec((tm, xb_width), lambda i: (i, xb_start // xb_width)),
            pl.BlockSpec((1, xa_cols), const),
            pl.BlockSpec((1, xa_cols), const),
            pl.BlockSpec((1, kv_cols), const),
            pl.BlockSpec((1, kv_cols), const),
            pl.BlockSpec((xa_cols, qk_cols), const),
            pl.BlockSpec((kv_cols, qk_cols), const),
            pl.BlockSpec((tm, LANES), pos),
            pl.BlockSpec((tm, LANES), pos),
            pl.BlockSpec((tm, LANES), pos),
        ],
        out_specs=[
            pl.BlockSpec((tm, qk_cols), lambda i: (i, 0)),
            pl.BlockSpec((tm, qk_cols), lambda i: (i, 0)),
            pl.BlockSpec((tm, MLA_HEADS * MLA_V_DIM), lambda i: (i, 0)),
        ],
        out_shape=[
            jax.ShapeDtypeStruct((n_tok, qk_cols), BF16),
            jax.ShapeDtypeStruct((n_tok, qk_cols), BF16),
            jax.ShapeDtypeStruct((n_tok, MLA_HEADS * MLA_V_DIM), BF16),
        ],
        compiler_params=_params(("arbitrary",)),
        name="mla_prep",
    )(pb2d, pb2d, q_ind, q_gain, kv_ind, kv_gain, wq, wkv, cos_t, sin_lo, sin_hi)


def _flash_kernel(q_ref, k_ref, v_ref, o_ref, m_ref, acc_ref, *, tq, heads):
    qi = pl.program_id(2)
    m_ref[...] = jnp.full_like(m_ref, NEG_BIG)
    acc_ref[...] = jnp.zeros_like(acc_ref)
    ones_col = (lax.broadcasted_iota(jnp.int32, (tq, LANES), 1) == 0).astype(BF16)
    on_or_below_diag = (lax.broadcasted_iota(jnp.int32, (tq, tq), 1)
                        <= lax.broadcasted_iota(jnp.int32, (tq, tq), 0))

    def key_block(j, masked):
        rows = pl.ds(pl.multiple_of(j * tq, tq), tq)
        hs = range(heads)
        s = [lax.dot_general(q_ref[0, :, h * MLA_QK_GROUP:(h + 1) * MLA_QK_GROUP],
                             k_ref[0, rows, h * MLA_QK_GROUP:(h + 1) * MLA_QK_GROUP],
                             (((1,), (1,)), ((), ())), preferred_element_type=F32) for h in hs]
        if masked:
            s = [jnp.where(on_or_below_diag, s_h, NEG_BIG) for s_h in s]
        m_prev = [m_ref[h] for h in hs]
        m_new = [jnp.maximum(m_prev[h], jnp.max(s[h], axis=-1, keepdims=True)) for h in hs]
        p = [jnp.exp2(s[h] - jnp.concatenate([m_new[h]] * (tq // LANES), axis=1)).astype(BF16) for h in hs]
        alpha = [jnp.exp2(m_prev[h] - m_new[h]) for h in hs]
        for h in hs:
            v_aug = jnp.concatenate([v_ref[0, rows, h * MLA_V_DIM:(h + 1) * MLA_V_DIM], ones_col], axis=1)
            pv = jnp.dot(p[h], v_aug, preferred_element_type=F32)
            acc_ref[h] = jnp.concatenate([alpha[h], alpha[h]], axis=1) * acc_ref[h] + pv
            m_ref[h] = m_new[h]

    def body(j, carry):
        key_block(j, False)
        return carry

    lax.fori_loop(0, qi, body, 0)
    key_block(qi, True)
    for h in range(heads):
        acc = acc_ref[h]
        o_ref[0, :, h * MLA_V_DIM:(h + 1) * MLA_V_DIM] = (
            acc[:, :MLA_V_DIM] / acc[:, MLA_V_DIM:MLA_V_DIM + 1]).astype(o_ref.dtype)


def _flash(q, k, v, *, tq=512, heads=8):
    bsz, seq, _ = q.shape
    kern = functools.partial(_flash_kernel, tq=tq, heads=heads)
    return pl.pallas_call(
        kern,
        grid=(bsz, MLA_HEADS // heads, seq // tq),
        in_specs=[
            pl.BlockSpec((1, tq, heads * MLA_QK_GROUP), lambda b, g, qi: (b, qi, g)),
            pl.BlockSpec((1, seq, heads * MLA_QK_GROUP), lambda b, g, qi: (b, 0, g)),
            pl.BlockSpec((1, seq, heads * MLA_V_DIM), lambda b, g, qi: (b, 0, g)),
        ],
        out_specs=pl.BlockSpec((1, tq, heads * MLA_V_DIM), lambda b, g, qi: (b, qi, g)),
        out_shape=jax.ShapeDtypeStruct((bsz, seq, MLA_HEADS * MLA_V_DIM), BF16),
        scratch_shapes=[
            pltpu.VMEM((heads, tq, LANES), F32),
            pltpu.VMEM((heads, tq, MLA_V_DIM + LANES), F32),
        ],
        compiler_params=_params(("arbitrary", "arbitrary", "arbitrary")),
        name="flash",
    )(q, k, v)


def _outproj_kernel(x_ref, hml_ref, hmla_ref, mod_ref, w_ref, g_ref, b_ref, o_ref, wb_ref, *,
                    sub, tiles_per_batch, alpha, row_chunk, norm_rows):
    i = pl.program_id(0)
    bidx = i // tiles_per_batch
    k_ml = hml_ref.shape[1]

    @pl.when(i == 0)
    def _():
        def body(r, carry):
            rows = pl.ds(pl.multiple_of(r * row_chunk, row_chunk), row_chunk)
            wb_ref[rows, :] = w_ref[rows, :].astype(BF16)
            return carry

        lax.fori_loop(0, w_ref.shape[0] // row_chunk, body, 0)

    gate = 1.0 + mod_ref[bidx, 3 * sub + 2:3 * sub + 3, :]
    g = g_ref[...]
    b_ln = b_ref[...]
    w_top = wb_ref[0:k_ml, :]
    w_bot = wb_ref[k_ml:, :]
    for r in range(x_ref.shape[0] // norm_rows):
        rows = slice(r * norm_rows, (r + 1) * norm_rows)
        y = (jnp.dot(hml_ref[rows, :], w_top, preferred_element_type=F32)
             + jnp.dot(hmla_ref[rows, :], w_bot, preferred_element_type=F32))
        o_ref[rows, :] = _layer_norm_rows(alpha * x_ref[rows, :] + gate * y, g, b_ln)


def _outproj(x2d, hml, hmla, mod, w_out, ln_g, ln_b, *, sub, seq, alpha, tm=512, row_chunk=128,
             norm_rows=128):
    n_tok, d = x2d.shape
    kern = functools.partial(_outproj_kernel, sub=sub, tiles_per_batch=seq // tm, alpha=alpha,
                             row_chunk=row_chunk, norm_rows=norm_rows)
    return pl.pallas_call(
        kern,
        grid=(n_tok // tm,),
        in_specs=[
            pl.BlockSpec((tm, d), lambda i: (i, 0)),
            pl.BlockSpec((tm, hml.shape[1]), lambda i: (i, 0)),
            pl.BlockSpec((tm, hmla.shape[1]), lambda i: (i, 0)),
            pl.BlockSpec(mod.shape, lambda i: (0, 0, 0)),
            pl.BlockSpec(w_out.shape, lambda i: (0, 0), pipeline_mode=pl.Buffered(1)),
            pl.BlockSpec((1, d), lambda i: (0, 0)),
            pl.BlockSpec((1, d), lambda i: (0, 0)),
        ],
        out_specs=pl.BlockSpec((tm, d), lambda i: (i, 0)),
        out_shape=jax.ShapeDtypeStruct((n_tok, d), F32),
        scratch_shapes=[pltpu.VMEM(w_out.shape, BF16)],
        compiler_params=_params(("arbitrary",)),
        name="outproj",
    )(x2d, hml, hmla, mod, w_out, ln_g.reshape(1, d), ln_b.reshape(1, d))


def kernel(x, c, w_ada, b_ada, ffn1_w1, ffn1_w3, ffn1_w2, ln1_g, ln1_b, w_in, conv_w, conv_b, b_igate, b_fgate, ml_norm_g, q_norm_g, w_uq, kv_norm_g, w_ukv, w_out, ln2_g, ln2_b, ffn2_w1, ffn2_w3, ffn2_w2, ln3_g, ln3_b):
    bsz, seq, d = x.shape
    depth = w_ada.shape[0]
    alpha = (2.0 * depth) ** 0.25
    h2d = x.reshape(bsz * seq, d)
    for l in range(depth):
        mod = _adaln(c, w_ada[l], b_ada[l]).reshape(bsz, N_SUBLAYERS * 3, d)
        h2d = _ffn(h2d, mod, ffn1_w1[l], ffn1_w3[l], ffn1_w2[l], ln1_g[l], ln1_b[l],
                   sub=0, seq=seq, alpha=alpha)
        assert w_in.shape[2] == D_IN
        pa, pb = _inproj(h2d, mod, w_in[l].T, sub=1, seq=seq)
        hml = _mlstm(pa.reshape(bsz, seq, PA_COLS), pb.reshape(bsz, seq, PB_COLS), conv_w[l], conv_b[l],
                     b_igate[l], b_fgate[l], ml_norm_g[l])
        q, k, v = _mla_prep(pb, q_norm_g[l], kv_norm_g[l], w_uq[l], w_ukv[l], seq=seq)
        hmla = _flash(q.reshape(bsz, seq, -1), k.reshape(bsz, seq, -1), v.reshape(bsz, seq, -1))
        h2d = _outproj(h2d, hml.reshape(bsz * seq, -1), hmla.reshape(bsz * seq, -1), mod,
                       w_out[l], ln2_g[l], ln2_b[l], sub=1, seq=seq, alpha=alpha)
        h2d = _ffn(h2d, mod, ffn2_w1[l], ffn2_w3[l], ffn2_w2[l], ln3_g[l], ln3_b[l],
                   sub=2, seq=seq, alpha=alpha)
    return h2d.reshape(bsz, seq, d)
```

```python
import functools
import math

import jax
import jax.numpy as jnp
from jax import lax
from jax.experimental import pallas as pl
from jax.experimental.pallas import tpu as pltpu

F32 = jnp.float32
BF16 = jnp.bfloat16

ML_HEADS = 4
ML_QK_DIM = 128
ML_V_DIM = 256
CONV_WIDTH = 4
MLA_HEADS = 8
MLA_NOPE_DIM = 128
MLA_ROPE_DIM = 64
MLA_V_DIM = 128
Q_LORA_RANK = 512
KV_LORA_RANK = 256
ROPE_THETA = 10000.0
FFN_RES_WEIGHT = 0.5
N_SUBLAYERS = 3
LN_EPS = 1e-5

LANES = 128
SUBLANES = 8
VMEM_LIMIT_BYTES = 56 * 1024 * 1024

ML_QK_COLS = 2 * ML_HEADS * ML_QK_DIM
ML_V_COLS = ML_HEADS * ML_V_DIM
PA_COLS = ML_QK_COLS + 2 * ML_V_COLS
PB_GATES = 0
PB_CQ = PB_GATES + 2 * ML_HEADS
PB_CKV = PB_CQ + Q_LORA_RANK
PB_KR = PB_CKV + KV_LORA_RANK
PB_VALID = PB_KR + MLA_ROPE_DIM
PB_COLS = -(-PB_VALID // 512) * 512
D_IN = PA_COLS + PB_VALID
MLA_QK_GROUP = 2 * LANES
NEG_BIG = -1e30


def _params(semantics):
    return pltpu.CompilerParams(dimension_semantics=semantics, vmem_limit_bytes=VMEM_LIMIT_BYTES)


def _silu(v):
    return v * jax.nn.sigmoid(v)


def _layer_norm_rows(z, g, b):
    mu = jnp.mean(z, axis=-1, keepdims=True)
    zc = z - mu
    var = jnp.mean(zc * zc, axis=-1, keepdims=True)
    return zc * lax.rsqrt(var + LN_EPS) * g + b


def _adaln_kernel(c_ref, w_ref, b_ref, o_ref):
    sc = _silu(c_ref[...]).astype(BF16)
    o_ref[...] = jnp.dot(sc, w_ref[...].astype(BF16), preferred_element_type=F32) + b_ref[...]


def _adaln(c, w, b, *, tn=1024):
    bsz, d = c.shape
    n = w.shape[1]
    rows = -(-bsz // SUBLANES) * SUBLANES
    c_pad = jnp.pad(c, ((0, rows - bsz), (0, 0)))
    out = pl.pallas_call(
        _adaln_kernel,
        grid=(n // tn,),
        in_specs=[
            pl.BlockSpec((rows, d), lambda j: (0, 0)),
            pl.BlockSpec((d, tn), lambda j: (0, j)),
            pl.BlockSpec((1, tn), lambda j: (0, j)),
        ],
        out_specs=pl.BlockSpec((rows, tn), lambda j: (0, j)),
        out_shape=jax.ShapeDtypeStruct((rows, n), F32),
        compiler_params=_params(("arbitrary",)),
        name="adaln",
    )(c_pad, w, b.reshape(1, n))
    return out[:bsz]


def _ffn_kernel(x_ref, mod_ref, g_ref, b_ref, w1_hbm, w3_hbm, w2_hbm, o_ref,
                w1_buf, w3_buf, w2_buf, sem, *, sub, tiles_per_batch, alpha, finish_rows, tf):
    i = pl.program_id(0)
    n_tiles = pl.num_programs(0)
    bidx = i // tiles_per_batch
    tm = x_ref.shape[0]
    n_f = w1_hbm.shape[1] // tf

    def chunk_copies(c, slot):
        cols = pl.ds(pl.multiple_of(c * tf, tf), tf)
        return (pltpu.make_async_copy(w1_hbm.at[:, cols], w1_buf.at[slot], sem.at[0, slot]),
                pltpu.make_async_copy(w3_hbm.at[:, cols], w3_buf.at[slot], sem.at[1, slot]),
                pltpu.make_async_copy(w2_hbm.at[cols, :], w2_buf.at[slot], sem.at[2, slot]))

    def start_chunk(c, slot):
        for cp in chunk_copies(c, slot):
            cp.start()

    def wait_chunk(c, slot):
        for cp in chunk_copies(c, slot):
            cp.wait()

    @pl.when(i == 0)
    def _():
        start_chunk(0, 0)

    shift = mod_ref[bidx, 3 * sub:3 * sub + 1, :]
    scale1 = 1.0 + mod_ref[bidx, 3 * sub + 1:3 * sub + 2, :]

    def partial_out(x_rows, w1, w3, w2):
        u = (x_rows * scale1 + shift).astype(BF16)
        a = jnp.dot(u, w1, preferred_element_type=F32)
        b = jnp.dot(u, w3, preferred_element_type=F32)
        h = (_silu(a) * b).astype(BF16)
        return jnp.dot(h, w2, preferred_element_type=F32)

    def chunk_weights(slot):
        return (w1_buf[slot].astype(BF16), w3_buf[slot].astype(BF16), w2_buf[slot].astype(BF16))

    start_chunk(1, 1)
    wait_chunk(0, 0)
    o_ref[...] = partial_out(x_ref[...], *chunk_weights(0))

    def chunk_pair(k, carry):
        c1 = 2 * k + 1
        start_chunk(c1 + 1, 0)
        wait_chunk(c1, 1)
        o_ref[...] += partial_out(x_ref[...], *chunk_weights(1))
        start_chunk(c1 + 2, 1)
        wait_chunk(c1 + 1, 0)
        o_ref[...] += partial_out(x_ref[...], *chunk_weights(0))
        return carry

    lax.fori_loop(0, (n_f - 2) // 2, chunk_pair, 0)

    @pl.when(i + 1 < n_tiles)
    def _():
        start_chunk(0, 0)

    wait_chunk(n_f - 1, 1)
    gate = FFN_RES_WEIGHT * (1.0 + mod_ref[bidx, 3 * sub + 2:3 * sub + 3, :])
    g = g_ref[...]
    b_ln = b_ref[...]
    last_w = chunk_weights(1)
    for r in range(tm // finish_rows):
        rows = slice(r * finish_rows, (r + 1) * finish_rows)
        x_rows = x_ref[rows, :]
        y = o_ref[rows, :] + partial_out(x_rows, *last_w)
        o_ref[rows, :] = _layer_norm_rows(alpha * x_rows + gate * y, g, b_ln)


def _ffn(x2d, mod, w1, w3, w2, ln_g, ln_b, *, sub, seq, alpha, tm=1024, tf=256, finish_rows=256):
    n_tok, d = x2d.shape
    f = w1.shape[1]
    tm = min(tm, seq)
    assert f % (2 * tf) == 0 and f // tf >= 4 and tm % finish_rows == 0
    kern = functools.partial(_ffn_kernel, sub=sub, tiles_per_batch=seq // tm, alpha=alpha,
                             finish_rows=finish_rows, tf=tf)
    hbm = pl.BlockSpec(memory_space=pl.ANY)
    return pl.pallas_call(
        kern,
        grid=(n_tok // tm,),
        in_specs=[
            pl.BlockSpec((tm, d), lambda i: (i, 0)),
            pl.BlockSpec(mod.shape, lambda i: (0, 0, 0)),
            pl.BlockSpec((1, d), lambda i: (0, 0)),
            pl.BlockSpec((1, d), lambda i: (0, 0)),
            hbm, hbm, hbm,
        ],
        out_specs=pl.BlockSpec((tm, d), lambda i: (i, 0)),
        out_shape=jax.ShapeDtypeStruct((n_tok, d), F32),
        scratch_shapes=[
            pltpu.VMEM((2, d, tf), w1.dtype),
            pltpu.VMEM((2, d, tf), w3.dtype),
            pltpu.VMEM((2, tf, d), w2.dtype),
            pltpu.SemaphoreType.DMA((3, 2)),
        ],
        compiler_params=_params(("arbitrary",)),
        name=f"ffn{sub}",
    )(x2d, mod, ln_g.reshape(1, d), ln_b.reshape(1, d), w1, w3, w2)


def _inproj_kernel(x_hbm, mod_ref, w_ref, oa_ref, ob_ref, wbf_ref, x_buf, x_sem, *,
                   sub, tiles_per_batch, n_a, n_valid):
    i = pl.program_id(0)
    j = pl.program_id(1)
    n_tiles = pl.num_programs(0)
    bidx = i // tiles_per_batch
    tn = w_ref.shape[0]
    tm = x_buf.shape[1]
    slot = i % 2

    def x_copy(tile, to_slot):
        rows = pl.ds(pl.multiple_of(tile * tm, tm), tm)
        return pltpu.make_async_copy(x_hbm.at[rows, :], x_buf.at[to_slot], x_sem.at[to_slot])

    @pl.when(j == 0)
    def _():
        @pl.when(i == 0)
        def _():
            x_copy(0, 0).start()

        x_copy(i, slot).wait()

        @pl.when(i + 1 < n_tiles)
        def _():
            x_copy(i + 1, 1 - slot).start()

    @pl.when(i == 0)
    def _():
        out_col = j * tn + lax.broadcasted_iota(jnp.int32, w_ref.shape, 0)
        wbf_ref[j] = jnp.where(out_col < n_valid, w_ref[...], 0.0).astype(BF16)

    def project():
        shift = mod_ref[bidx, 3 * sub:3 * sub + 1, :]
        scale1 = 1.0 + mod_ref[bidx, 3 * sub + 1:3 * sub + 2, :]
        u = (x_buf[slot] * scale1 + shift).astype(BF16)
        return lax.dot_general(u, wbf_ref[j], (((1,), (1,)), ((), ())), preferred_element_type=F32)

    @pl.when(j < n_a)
    def _():
        oa_ref[...] = project().astype(BF16)

    @pl.when(j >= n_a)
    def _():
        ob_ref[...] = project()


def _inproj(x2d, mod, w_in_t, *, sub, seq, tm=1024, tn=512):
    n_tok, d = x2d.shape
    n_a = PA_COLS // tn
    n_b = PB_COLS // tn
    n_blocks = n_a + n_b
    kern = functools.partial(_inproj_kernel, sub=sub, tiles_per_batch=seq // tm, n_a=n_a,
                             n_valid=w_in_t.shape[0])
    return pl.pallas_call(
        kern,
        grid=(n_tok // tm, n_blocks),
        in_specs=[
            pl.BlockSpec(memory_space=pl.ANY),
            pl.BlockSpec(mod.shape, lambda i, j: (0, 0, 0)),
            pl.BlockSpec((tn, d), lambda i, j: (jnp.where(i == 0, j, n_blocks - 1), 0)),
        ],
        out_specs=[
            pl.BlockSpec((tm, tn), lambda i, j: (i, jnp.minimum(j, n_a - 1))),
            pl.BlockSpec((tm, tn), lambda i, j: (i, jnp.maximum(j - n_a, 0))),
        ],
        out_shape=[
            jax.ShapeDtypeStruct((n_tok, PA_COLS), BF16),
            jax.ShapeDtypeStruct((n_tok, PB_COLS), F32),
        ],
        scratch_shapes=[
            pltpu.VMEM((n_blocks, tn, d), BF16),
            pltpu.VMEM((2, tm, d), x2d.dtype),
            pltpu.SemaphoreType.DMA((2,)),
        ],
        compiler_params=_params(("arbitrary", "arbitrary")),
        name="inproj",
    )(x2d, mod, w_in_t)


def _lane_scan(v, op, chunk):
    lane = lax.broadcasted_iota(jnp.int32, v.shape, 1) % chunk
    ident = 0.0 if op is jnp.add else NEG_BIG
    s = 1
    while s < chunk:
        shifted = pltpu.roll(v, s, axis=1)
        v = op(v, jnp.where(lane >= s, shifted, ident))
        s *= 2
    return v


def _mlstm_gate_scans(bi, gates_ref, gb_ref, scan_ref, chunk):
    seq = gates_ref.shape[1]
    for c in range(seq // LANES):
        cols = slice(c * LANES, (c + 1) * LANES)
        scan_ref[bi, 0, :, cols] = gates_ref[bi, cols, :].T[0:SUBLANES, :]
    z = scan_ref[bi, 0] + gb_ref[:, 0:1]
    logi = pltpu.roll(z, ML_HEADS, axis=0)
    logf = jnp.minimum(z, 0.0) - jnp.log1p(jnp.exp(-jnp.abs(z)))
    bcum = _lane_scan(logf, jnp.add, chunk)
    r = logi - bcum
    scan_ref[bi, 0] = bcum
    scan_ref[bi, 1] = r
    scan_ref[bi, 2] = _lane_scan(r, jnp.maximum, chunk)


def _mlstm_kernel(qk_ref, v_ref, og_ref, gates_ref, cw_ref, cb_ref, gb_ref, ng_ref, sel_ref, cbias_ref,
                  o_ref, c_ref, m_ref, prev_ref, scan_ref, *, chunk):
    t = pl.program_id(1)

    @pl.when(t == 0)
    def _():
        c_ref[...] = jnp.zeros_like(c_ref)
        m_ref[...] = jnp.zeros_like(m_ref)
        prev_ref[...] = jnp.zeros_like(prev_ref)
        for bi in range(qk_ref.shape[0]):
            _mlstm_gate_scans(bi, gates_ref, gb_ref, scan_ref, chunk)

    _mlstm_step(t, qk_ref, v_ref, og_ref, cw_ref, cb_ref, ng_ref, sel_ref, cbias_ref, o_ref,
                c_ref, m_ref, prev_ref, scan_ref, chunk)


def _mlstm_step(t, qk_ref, v_ref, og_ref, cw_ref, cb_ref, ng_ref, sel_ref, cbias_ref, o_ref,
                c_ref, m_ref, prev_ref, scan_ref, chunk):
    hq = ML_HEADS * ML_QK_DIM
    n_rows = qk_ref.shape[0]
    chains = [(bi, h) for bi in range(n_rows) for h in range(ML_HEADS)]
    head_cols = lambda h: slice(h * ML_V_DIM, (h + 1) * ML_V_DIM)
    causal_bias = cbias_ref[...]
    ones_col = (lax.broadcasted_iota(jnp.int32, (chunk, LANES), 1) == 0).astype(BF16)
    lanes = pl.ds(pl.multiple_of(t * chunk, chunk), chunk)

    qk = []
    for bi in range(n_rows):
        qk_now = qk_ref[bi]
        taps = jnp.dot(sel_ref[...], jnp.concatenate([prev_ref[bi], qk_now], axis=0),
                       preferred_element_type=F32)
        prev_ref[bi] = qk_now
        conv = cb_ref[...]
        for tap in range(CONV_WIDTH):
            conv = conv + cw_ref[tap:tap + 1, :] * taps[tap * chunk:(tap + 1) * chunk, :]
        qk.append(_silu(conv))

    r, cols, s_old = [], [], []
    for bi in range(n_rows):
        bcum = scan_ref[bi, 0, :, lanes]
        r_b = scan_ref[bi, 1, :, lanes]
        cm = scan_ref[bi, 2, :, lanes]
        m_in = m_ref[bi, :, 0:1]
        m_all = jnp.maximum(m_in, cm)
        m_last = m_all[:, chunk - 1:chunk]
        m_ref[bi] = jnp.broadcast_to(bcum[:, chunk - 1:chunk] + m_last, (SUBLANES, LANES))
        s_inter = jnp.exp(m_in - m_all)
        e_neg_m = jnp.exp(-(bcum + m_all))
        w_last = jnp.exp(r_b - m_last)
        stack = jnp.concatenate(
            [m_all, s_inter, e_neg_m, w_last, jnp.zeros((LANES - 4 * SUBLANES, chunk), F32)], axis=0)
        r.append(r_b)
        cols.append(stack.T)
        s_old.append(jnp.exp(m_in - m_last))

    def col(bi, h, which):
        lane = which * SUBLANES + ML_HEADS + h
        return cols[bi][:, lane:lane + 1]

    q_h = {c: qk[c[0]][:, c[1] * ML_QK_DIM:(c[1] + 1) * ML_QK_DIM].astype(BF16) for c in chains}
    k_f = {c: qk[c[0]][:, hq + c[1] * ML_QK_DIM:hq + (c[1] + 1) * ML_QK_DIM] * (ML_QK_DIM ** -0.5)
           for c in chains}
    v_aug = {c: jnp.concatenate([v_ref[c[0], :, head_cols(c[1])], ones_col], axis=1) for c in chains}
    scores = {c: lax.dot_general(q_h[c], k_f[c].astype(BF16), (((1,), (1,)), ((), ())),
                                 preferred_element_type=F32) for c in chains}
    p = {}
    for bi, h in chains:
        decay = jnp.exp(r[bi][ML_HEADS + h:ML_HEADS + h + 1, :] - col(bi, h, 0) + causal_bias)
        p[bi, h] = (decay * scores[bi, h]).astype(BF16)
    c_old = {c: c_ref[c[0], c[1]] for c in chains}
    tot = {c: (jnp.dot(p[c], v_aug[c], preferred_element_type=F32)
               + col(c[0], c[1], 1) * jnp.dot(q_h[c], c_old[c].astype(BF16), preferred_element_type=F32))
           for c in chains}
    for bi, h in chains:
        k_w = (k_f[bi, h] * col(bi, h, 3)).astype(BF16)
        c_ref[bi, h] = s_old[bi][ML_HEADS + h:ML_HEADS + h + 1, :] * c_old[bi, h] + lax.dot_general(
            k_w, v_aug[bi, h], (((0,), (0,)), ((), ())), preferred_element_type=F32)
    for bi, h in chains:
        num = tot[bi, h][:, :ML_V_DIM]
        den = tot[bi, h][:, ML_V_DIM:ML_V_DIM + 1]
        hid = num / jnp.maximum(jnp.abs(den), col(bi, h, 2))
        mu = jnp.mean(hid, axis=-1, keepdims=True)
        hc = hid - mu
        var = jnp.mean(hc * hc, axis=-1, keepdims=True)
        hn = hc * lax.rsqrt(var + LN_EPS) * ng_ref[:, head_cols(h)]
        gate = jax.nn.sigmoid(og_ref[bi, :, head_cols(h)].astype(F32))
        o_ref[bi, :, head_cols(h)] = (hn * gate).astype(o_ref.dtype)


def _mlstm(pa, pb, conv_w, conv_b, b_igate, b_fgate, norm_g, *, chunk=128, rows_per_step=2):
    bsz, seq, _ = pa.shape
    rb = rows_per_step
    gate_bias = jnp.broadcast_to(jnp.concatenate([b_igate, b_fgate])[:, None], (SUBLANES, LANES))
    t_idx = jnp.arange(CONV_WIDTH * chunk) % chunk
    tap_idx = jnp.arange(CONV_WIDTH * chunk) // chunk
    shift_sel = (jnp.arange(2 * chunk)[None, :]
                 == (chunk + t_idx - (CONV_WIDTH - 1) + tap_idx)[:, None]).astype(BF16)
    causal_bias = jnp.where(jnp.arange(chunk)[None, :] <= jnp.arange(chunk)[:, None], 0.0, NEG_BIG).astype(F32)
    kern = functools.partial(_mlstm_kernel, chunk=chunk)
    return pl.pallas_call(
        kern,
        grid=(bsz // rb, seq // chunk),
        in_specs=[
            pl.BlockSpec((rb, chunk, ML_QK_COLS), lambda b, t: (b, t, 0)),
            pl.BlockSpec((rb, chunk, ML_V_COLS), lambda b, t: (b, t, ML_QK_COLS // ML_V_COLS)),
            pl.BlockSpec((rb, chunk, ML_V_COLS), lambda b, t: (b, t, ML_QK_COLS // ML_V_COLS + 1)),
            pl.BlockSpec((rb, seq, LANES), lambda b, t: (b, 0, PB_GATES // LANES)),
            pl.BlockSpec((CONV_WIDTH, ML_QK_COLS), lambda b, t: (0, 0)),
            pl.BlockSpec((1, ML_QK_COLS), lambda b, t: (0, 0)),
            pl.BlockSpec((SUBLANES, LANES), lambda b, t: (0, 0)),
            pl.BlockSpec((1, ML_V_COLS), lambda b, t: (0, 0)),
            pl.BlockSpec(shift_sel.shape, lambda b, t: (0, 0)),
            pl.BlockSpec(causal_bias.shape, lambda b, t: (0, 0)),
        ],
        out_specs=pl.BlockSpec((rb, chunk, ML_V_COLS), lambda b, t: (b, t, 0)),
        out_shape=jax.ShapeDtypeStruct((bsz, seq, ML_V_COLS), BF16),
        scratch_shapes=[
            pltpu.VMEM((rb, ML_HEADS, ML_QK_DIM, ML_V_DIM + LANES), F32),
            pltpu.VMEM((rb, SUBLANES, LANES), F32),
            pltpu.VMEM((rb, chunk, ML_QK_COLS), pa.dtype),
            pltpu.VMEM((rb, 3, SUBLANES, seq), F32),
        ],
        compiler_params=_params(("arbitrary", "arbitrary")),
        name="mlstm",
    )(pa, pa, pa, pb, conv_w, conv_b.reshape(1, -1), gate_bias, norm_g.reshape(1, -1),
      shift_sel, causal_bias)


def _rope_lanes(x, cos, sin_lo, sin_hi):
    half = MLA_ROPE_DIM // 2
    return x * cos + pltpu.roll(x, LANES - half, axis=1) * sin_lo + pltpu.roll(x, half, axis=1) * sin_hi


def _rms_window(x, ind, g, n):
    ssq = jnp.sum(jnp.square(x * ind), axis=-1, keepdims=True)
    return x * lax.rsqrt(ssq * (1.0 / n) + LN_EPS) * g


def _mla_prep_kernel(xa_ref, xb_ref, qi_ref, qg_ref, kvi_ref, kvg_ref, wq_ref, wkv_ref,
                     cos_ref, sl_ref, sh_ref, q_ref, k_ref, v_ref, *, q_scale, kv_cols, kr_group, kr_shift):
    cos = cos_ref[...]
    sin_lo = sl_ref[...]
    sin_hi = sh_ref[...]
    q_lat = _rms_window(xa_ref[...], qi_ref[...], qg_ref[...], Q_LORA_RANK).astype(BF16)
    kv_lat = _rms_window(xb_ref[:, :kv_cols], kvi_ref[...], kvg_ref[...], KV_LORA_RANK).astype(BF16)
    k_r = pltpu.roll(xb_ref[:, kr_group * LANES:(kr_group + 1) * LANES], LANES - kr_shift, axis=1)
    k_rope = _rope_lanes(k_r, cos, sin_lo, sin_hi).astype(k_ref.dtype)
    for h in range(MLA_HEADS):
        lo = h * MLA_QK_GROUP
        q_h = jnp.dot(q_lat, wq_ref[:, lo:lo + MLA_QK_GROUP], preferred_element_type=F32) * q_scale
        q_ref[:, lo:lo + LANES] = q_h[:, :LANES].astype(q_ref.dtype)
        q_ref[:, lo + LANES:lo + 2 * LANES] = _rope_lanes(q_h[:, LANES:], cos, sin_lo, sin_hi).astype(q_ref.dtype)
        kv_h = jnp.dot(kv_lat, wkv_ref[:, lo:lo + MLA_QK_GROUP], preferred_element_type=F32)
        k_ref[:, lo:lo + LANES] = kv_h[:, :LANES].astype(k_ref.dtype)
        k_ref[:, lo + LANES:lo + 2 * LANES] = k_rope
        v_ref[:, h * MLA_V_DIM:(h + 1) * MLA_V_DIM] = kv_h[:, LANES:].astype(v_ref.dtype)


def _rope_tables(seq):
    half = MLA_ROPE_DIM // 2
    inv = ROPE_THETA ** (-jnp.arange(half, dtype=F32) / half)
    ang = jnp.arange(seq, dtype=F32)[:, None] * inv[None, :]
    cos, sin = jnp.cos(ang), jnp.sin(ang)
    zero = jnp.zeros_like(cos)
    cos_t = jnp.concatenate([cos, cos, zero, zero], axis=1)
    sin_lo = jnp.concatenate([-sin, zero, zero, zero], axis=1)
    sin_hi = jnp.concatenate([zero, sin, zero, zero], axis=1)
    return cos_t, sin_lo, sin_hi


def _window_row(values, lo, width):
    return jnp.pad(values.astype(F32), (lo, width - lo - values.shape[0])).reshape(1, width)


def _mla_prep(pb2d, q_norm_g, kv_norm_g, w_uq, w_ukv, *, seq, tm=512):
    n_tok = pb2d.shape[0]
    dq = MLA_NOPE_DIM + MLA_ROPE_DIM
    xb_width = 512
    xb_start = (PB_CKV // xb_width) * xb_width
    assert PB_VALID <= xb_start + xb_width and PB_COLS % xb_width == 0
    xa_cols = -(-PB_CKV // LANES) * LANES
    ckv_lo = PB_CKV - xb_start
    kv_cols = -(-(ckv_lo + KV_LORA_RANK) // LANES) * LANES
    kr_lo = PB_KR - xb_start
    assert kr_lo % LANES + MLA_ROPE_DIM <= LANES
    wq = w_uq.reshape(Q_LORA_RANK, MLA_HEADS, dq)
    wq = jnp.pad(wq, ((PB_CQ, xa_cols - PB_CKV), (0, 0), (0, MLA_QK_GROUP - dq)))
    wq = wq.reshape(xa_cols, MLA_HEADS * MLA_QK_GROUP).astype(BF16)
    wkv = jnp.pad(w_ukv, ((ckv_lo, kv_cols - ckv_lo - KV_LORA_RANK), (0, 0))).astype(BF16)
    q_ind = _window_row(jnp.ones((Q_LORA_RANK,), F32), PB_CQ, xa_cols)
    q_gain = _window_row(q_norm_g, PB_CQ, xa_cols)
    kv_ind = _window_row(jnp.ones((KV_LORA_RANK,), F32), ckv_lo, kv_cols)
    kv_gain = _window_row(kv_norm_g, ckv_lo, kv_cols)
    cos_t, sin_lo, sin_hi = _rope_tables(seq)
    tiles_per_seq = seq // tm
    pos = lambda i: (i % tiles_per_seq, 0)
    kern = functools.partial(_mla_prep_kernel, q_scale=math.log2(math.e) / math.sqrt(dq),
                             kv_cols=kv_cols, kr_group=kr_lo // LANES, kr_shift=kr_lo % LANES)
    qk_cols = MLA_HEADS * MLA_QK_GROUP
    const = lambda i: (0, 0)
    return pl.pallas_call(
        kern,
        grid=(n_tok // tm,),
        in_specs=[
            pl.BlockSpec((tm, xa_cols), lambda i: (i, 0)),
            pl.BlockSpec((tm, xb_width), lambda i: (i, xb_start // xb_width)),
            pl.BlockSpec((1, xa_cols), const),
            pl.BlockSpec((1, xa_cols), const),
            pl.BlockSpec((1, kv_cols), const),
            pl.BlockSpec((1, kv_cols), const),
            pl.BlockSpec((xa_cols, qk_cols), const),
            pl.BlockSpec((kv_cols, qk_cols), const),
            pl.BlockSpec((tm, LANES), pos),
            pl.BlockSpec((tm, LANES), pos),
            pl.BlockSpec((tm, LANES), pos),
        ],
        out_specs=[
            pl.BlockSpec((tm, qk_cols), lambda i: (i, 0)),
            pl.BlockSpec((tm, qk_cols), lambda i: (i, 0)),
            pl.BlockSpec((tm, MLA_HEADS * MLA_V_DIM), lambda i: (i, 0)),
        ],
        out_shape=[
            jax.ShapeDtypeStruct((n_tok, qk_cols), BF16),
            jax.ShapeDtypeStruct((n_tok, qk_cols), BF16),
            jax.ShapeDtypeStruct((n_tok, MLA_HEADS * MLA_V_DIM), BF16),
        ],
        compiler_params=_params(("arbitrary",)),
        name="mla_prep",
    )(pb2d, pb2d, q_ind, q_gain, kv_ind, kv_gain, wq, wkv, cos_t, sin_lo, sin_hi)


def _flash_kernel(q_ref, k_hbm, v_hbm, o_ref, m_ref, acc_ref, k_buf, v_buf, kv_sem, *, tq, heads):
    qi = pl.program_id(2)
    n_groups = pl.num_programs(1)
    unit = pl.program_id(0) * n_groups + pl.program_id(1)
    n_units = pl.num_programs(0) * n_groups
    slot = unit % 2
    k_width = heads * MLA_QK_GROUP
    v_width = heads * MLA_V_DIM

    def kv_copies(u, to_slot):
        ub = u // n_groups
        ug = u % n_groups
        return (pltpu.make_async_copy(k_hbm.at[ub, :, pl.ds(pl.multiple_of(ug * k_width, k_width), k_width)],
                                      k_buf.at[to_slot], kv_sem.at[0, to_slot]),
                pltpu.make_async_copy(v_hbm.at[ub, :, pl.ds(pl.multiple_of(ug * v_width, v_width), v_width)],
                                      v_buf.at[to_slot], kv_sem.at[1, to_slot]))

    @pl.when(qi == 0)
    def _():
        @pl.when(unit == 0)
        def _():
            for cp in kv_copies(0, 0):
                cp.start()

        for cp in kv_copies(unit, slot):
            cp.wait()

        @pl.when(unit + 1 < n_units)
        def _():
            for cp in kv_copies(unit + 1, 1 - slot):
                cp.start()

    m_ref[...] = jnp.full_like(m_ref, NEG_BIG)
    acc_ref[...] = jnp.zeros_like(acc_ref)
    ones_col = (lax.broadcasted_iota(jnp.int32, (tq, LANES), 1) == 0).astype(BF16)
    on_or_below_diag = (lax.broadcasted_iota(jnp.int32, (tq, tq), 1)
                        <= lax.broadcasted_iota(jnp.int32, (tq, tq), 0))

    def key_block(j, masked):
        rows = pl.ds(pl.multiple_of(j * tq, tq), tq)
        hs = range(heads)
        s = [lax.dot_general(q_ref[0, :, h * MLA_QK_GROUP:(h + 1) * MLA_QK_GROUP],
                             k_buf[slot, rows, h * MLA_QK_GROUP:(h + 1) * MLA_QK_GROUP],
                             (((1,), (1,)), ((), ())), preferred_element_type=F32) for h in hs]
        if masked:
            s = [jnp.where(on_or_below_diag, s_h, NEG_BIG) for s_h in s]
        m_prev = [m_ref[h] for h in hs]
        m_new = [jnp.maximum(m_prev[h], jnp.max(s[h], axis=-1, keepdims=True)) for h in hs]
        p = [jnp.exp2(s[h] - jnp.concatenate([m_new[h]] * (tq // LANES), axis=1)).astype(BF16) for h in hs]
        alpha = [jnp.exp2(m_prev[h] - m_new[h]) for h in hs]
        for h in hs:
            v_aug = jnp.concatenate([v_buf[slot, rows, h * MLA_V_DIM:(h + 1) * MLA_V_DIM], ones_col], axis=1)
            pv = jnp.dot(p[h], v_aug, preferred_element_type=F32)
            acc_ref[h] = jnp.concatenate([alpha[h], alpha[h]], axis=1) * acc_ref[h] + pv
            m_ref[h] = m_new[h]

    def body(j, carry):
        key_block(j, False)
        return carry

    lax.fori_loop(0, qi, body, 0)
    key_block(qi, True)
    for h in range(heads):
        acc = acc_ref[h]
        o_ref[0, :, h * MLA_V_DIM:(h + 1) * MLA_V_DIM] = (
            acc[:, :MLA_V_DIM] / acc[:, MLA_V_DIM:MLA_V_DIM + 1]).astype(o_ref.dtype)


def _flash(q, k, v, *, tq=512, heads=8):
    bsz, seq, _ = q.shape
    kern = functools.partial(_flash_kernel, tq=tq, heads=heads)
    return pl.pallas_call(
        kern,
        grid=(bsz, MLA_HEADS // heads, seq // tq),
        in_specs=[
            pl.BlockSpec((1, tq, heads * MLA_QK_GROUP), lambda b, g, qi: (b, qi, g)),
            pl.BlockSpec(memory_space=pl.ANY),
            pl.BlockSpec(memory_space=pl.ANY),
        ],
        out_specs=pl.BlockSpec((1, tq, heads * MLA_V_DIM), lambda b, g, qi: (b, qi, g)),
        out_shape=jax.ShapeDtypeStruct((bsz, seq, MLA_HEADS * MLA_V_DIM), BF16),
        scratch_shapes=[
            pltpu.VMEM((heads, tq, LANES), F32),
            pltpu.VMEM((heads, tq, MLA_V_DIM + LANES), F32),
            pltpu.VMEM((2, seq, heads * MLA_QK_GROUP), k.dtype),
            pltpu.VMEM((2, seq, heads * MLA_V_DIM), v.dtype),
            pltpu.SemaphoreType.DMA((2, 2)),
        ],
        compiler_params=_params(("arbitrary", "arbitrary", "arbitrary")),
        name="flash",
    )(q, k, v)


def _outproj_kernel(x_ref, hml_ref, hmla_ref, mod_ref, w_ref, g_ref, b_ref, o_ref, wb_ref, *,
                    sub, tiles_per_batch, alpha, row_chunk, norm_rows):
    i = pl.program_id(0)
    bidx = i // tiles_per_batch
    k_ml = hml_ref.shape[1]

    @pl.when(i == 0)
    def _():
        def body(r, carry):
            rows = pl.ds(pl.multiple_of(r * row_chunk, row_chunk), row_chunk)
            wb_ref[rows, :] = w_ref[rows, :].astype(BF16)
            return carry

        lax.fori_loop(0, w_ref.shape[0] // row_chunk, body, 0)

    gate = 1.0 + mod_ref[bidx, 3 * sub + 2:3 * sub + 3, :]
    g = g_ref[...]
    b_ln = b_ref[...]
    w_top = wb_ref[0:k_ml, :]
    w_bot = wb_ref[k_ml:, :]
    for r in range(x_ref.shape[0] // norm_rows):
        rows = slice(r * norm_rows, (r + 1) * norm_rows)
        y = (jnp.dot(hml_ref[rows, :], w_top, preferred_element_type=F32)
             + jnp.dot(hmla_ref[rows, :], w_bot, preferred_element_type=F32))
        o_ref[rows, :] = _layer_norm_rows(alpha * x_ref[rows, :] + gate * y, g, b_ln)


def _outproj(x2d, hml, hmla, mod, w_out, ln_g, ln_b, *, sub, seq, alpha, tm=512, row_chunk=128,
             norm_rows=128):
    n_tok, d = x2d.shape
    kern = functools.partial(_outproj_kernel, sub=sub, tiles_per_batch=seq // tm, alpha=alpha,
                             row_chunk=row_chunk, norm_rows=norm_rows)
    return pl.pallas_call(
        kern,
        grid=(n_tok // tm,),
        in_specs=[
            pl.BlockSpec((tm, d), lambda i: (i, 0)),
            pl.BlockSpec((tm, hml.shape[1]), lambda i: (i, 0)),
            pl.BlockSpec((tm, hmla.shape[1]), lambda i: (i, 0)),
            pl.BlockSpec(mod.shape, lambda i: (0, 0, 0)),
            pl.BlockSpec(w_out.shape, lambda i: (0, 0), pipeline_mode=pl.Buffered(1)),
            pl.BlockSpec((1, d), lambda i: (0, 0)),
            pl.BlockSpec((1, d), lambda i: (0, 0)),
        ],
        out_specs=pl.BlockSpec((tm, d), lambda i: (i, 0)),
        out_shape=jax.ShapeDtypeStruct((n_tok, d), F32),
        scratch_shapes=[pltpu.VMEM(w_out.shape, BF16)],
        compiler_params=_params(("arbitrary",)),
        name="outproj",
    )(x2d, hml, hmla, mod, w_out, ln_g.reshape(1, d), ln_b.reshape(1, d))


def kernel(x, c, w_ada, b_ada, ffn1_w1, ffn1_w3, ffn1_w2, ln1_g, ln1_b, w_in, conv_w, conv_b, b_igate, b_fgate, ml_norm_g, q_norm_g, w_uq, kv_norm_g, w_ukv, w_out, ln2_g, ln2_b, ffn2_w1, ffn2_w3, ffn2_w2, ln3_g, ln3_b):
    bsz, seq, d = x.shape
    depth = w_ada.shape[0]
    alpha = (2.0 * depth) ** 0.25
    h2d = x.reshape(bsz * seq, d)
    for l in range(depth):
        mod = _adaln(c, w_ada[l], b_ada[l]).reshape(bsz, N_SUBLAYERS * 3, d)
        h2d = _ffn(h2d, mod, ffn1_w1[l], ffn1_w3[l], ffn1_w2[l], ln1_g[l], ln1_b[l],
                   sub=0, seq=seq, alpha=alpha)
        assert w_in.shape[2] == D_IN
        pa, pb = _inproj(h2d, mod, w_in[l].T, sub=1, seq=seq)
        hml = _mlstm(pa.reshape(bsz, seq, PA_COLS), pb.reshape(bsz, seq, PB_COLS), conv_w[l], conv_b[l],
                     b_igate[l], b_fgate[l], ml_norm_g[l])
        q, k, v = _mla_prep(pb, q_norm_g[l], kv_norm_g[l], w_uq[l], w_ukv[l], seq=seq)
        hmla = _flash(q.reshape(bsz, seq, -1), k.reshape(bsz, seq, -1), v.reshape(bsz, seq, -1))
        h2d = _outproj(h2d, hml.reshape(bsz * seq, -1), hmla.reshape(bsz * seq, -1), mod,
                       w_out[l], ln2_g[l], ln2_b[l], sub=1, seq=seq, alpha=alpha)
        h2d = _ffn(h2d, mod, ffn2_w1[l], ffn2_w3[l], ffn2_w2[l], ln3_g[l], ln3_b[l],
                   sub=2, seq=seq, alpha=alpha)
    return h2d.reshape(bsz, seq, d)
```

```python
import functools
import math

import jax
import jax.numpy as jnp
from jax import lax
from jax.experimental import pallas as pl
from jax.experimental.pallas import tpu as pltpu

F32 = jnp.float32
BF16 = jnp.bfloat16

ML_HEADS = 4
ML_QK_DIM = 128
ML_V_DIM = 256
CONV_WIDTH = 4
MLA_HEADS = 8
MLA_NOPE_DIM = 128
MLA_ROPE_DIM = 64
MLA_V_DIM = 128
Q_LORA_RANK = 512
KV_LORA_RANK = 256
ROPE_THETA = 10000.0
FFN_RES_WEIGHT = 0.5
N_SUBLAYERS = 3
LN_EPS = 1e-5

LANES = 128
SUBLANES = 8
VMEM_LIMIT_BYTES = 56 * 1024 * 1024

ML_QK_COLS = 2 * ML_HEADS * ML_QK_DIM
ML_V_COLS = ML_HEADS * ML_V_DIM
PA_COLS = ML_QK_COLS + 2 * ML_V_COLS
PB_GATES = 0
PB_CQ = PB_GATES + 2 * ML_HEADS
PB_CKV = PB_CQ + Q_LORA_RANK
PB_KR = PB_CKV + KV_LORA_RANK
PB_VALID = PB_KR + MLA_ROPE_DIM
PB_COLS = -(-PB_VALID // 512) * 512
D_IN = PA_COLS + PB_VALID
MLA_QK_GROUP = 2 * LANES
NEG_BIG = -1e30


def _params(semantics):
    return pltpu.CompilerParams(dimension_semantics=semantics, vmem_limit_bytes=VMEM_LIMIT_BYTES)


def _silu(v):
    return v * jax.nn.sigmoid(v)


def _layer_norm_rows(z, g, b):
    mu = jnp.mean(z, axis=-1, keepdims=True)
    zc = z - mu
    var = jnp.mean(zc * zc, axis=-1, keepdims=True)
    return zc * lax.rsqrt(var + LN_EPS) * g + b


def _adaln_kernel(c_ref, w_ref, b_ref, o_ref):
    sc = _silu(c_ref[...]).astype(BF16)
    o_ref[...] = jnp.dot(sc, w_ref[...].astype(BF16), preferred_element_type=F32) + b_ref[...]


def _adaln(c, w, b, *, tn=1024):
    bsz, d = c.shape
    n = w.shape[1]
    rows = -(-bsz // SUBLANES) * SUBLANES
    c_pad = jnp.pad(c, ((0, rows - bsz), (0, 0)))
    out = pl.pallas_call(
        _adaln_kernel,
        grid=(n // tn,),
        in_specs=[
            pl.BlockSpec((rows, d), lambda j: (0, 0)),
            pl.BlockSpec((d, tn), lambda j: (0, j)),
            pl.BlockSpec((1, tn), lambda j: (0, j)),
        ],
        out_specs=pl.BlockSpec((rows, tn), lambda j: (0, j)),
        out_shape=jax.ShapeDtypeStruct((rows, n), F32),
        compiler_params=_params(("arbitrary",)),
        name="adaln",
    )(c_pad, w, b.reshape(1, n))
    return out[:bsz]


def _ffn_kernel(x_ref, mod_ref, g_ref, b_ref, w1_hbm, w3_hbm, w2_hbm, o_ref,
                w1_buf, w3_buf, w2_buf, sem, *, sub, tiles_per_batch, alpha, finish_rows, tf):
    i = pl.program_id(0)
    n_tiles = pl.num_programs(0)
    bidx = i // tiles_per_batch
    tm = x_ref.shape[0]
    n_f = w1_hbm.shape[1] // tf

    def chunk_copies(c, slot):
        cols = pl.ds(pl.multiple_of(c * tf, tf), tf)
        return (pltpu.make_async_copy(w1_hbm.at[:, cols], w1_buf.at[slot], sem.at[0, slot]),
                pltpu.make_async_copy(w3_hbm.at[:, cols], w3_buf.at[slot], sem.at[1, slot]),
                pltpu.make_async_copy(w2_hbm.at[cols, :], w2_buf.at[slot], sem.at[2, slot]))

    def start_chunk(c, slot):
        for cp in chunk_copies(c, slot):
            cp.start()

    def wait_chunk(c, slot):
        for cp in chunk_copies(c, slot):
            cp.wait()

    @pl.when(i == 0)
    def _():
        start_chunk(0, 0)

    shift = mod_ref[bidx, 3 * sub:3 * sub + 1, :]
    scale1 = 1.0 + mod_ref[bidx, 3 * sub + 1:3 * sub + 2, :]

    def partial_out(x_rows, w1, w3, w2):
        u = (x_rows * scale1 + shift).astype(BF16)
        a = jnp.dot(u, w1, preferred_element_type=F32)
        b = jnp.dot(u, w3, preferred_element_type=F32)
        h = (_silu(a) * b).astype(BF16)
        return jnp.dot(h, w2, preferred_element_type=F32)

    def chunk_weights(slot):
        return (w1_buf[slot].astype(BF16), w3_buf[slot].astype(BF16), w2_buf[slot].astype(BF16))

    start_chunk(1, 1)
    wait_chunk(0, 0)
    o_ref[...] = partial_out(x_ref[...], *chunk_weights(0))

    def chunk_pair(k, carry):
        c1 = 2 * k + 1
        start_chunk(c1 + 1, 0)
        wait_chunk(c1, 1)
        o_ref[...] += partial_out(x_ref[...], *chunk_weights(1))
        start_chunk(c1 + 2, 1)
        wait_chunk(c1 + 1, 0)
        o_ref[...] += partial_out(x_ref[...], *chunk_weights(0))
        return carry

    lax.fori_loop(0, (n_f - 2) // 2, chunk_pair, 0)

    @pl.when(i + 1 < n_tiles)
    def _():
        start_chunk(0, 0)

    wait_chunk(n_f - 1, 1)
    gate = FFN_RES_WEIGHT * (1.0 + mod_ref[bidx, 3 * sub + 2:3 * sub + 3, :])
    g = g_ref[...]
    b_ln = b_ref[...]
    last_w = chunk_weights(1)
    for r in range(tm // finish_rows):
        rows = slice(r * finish_rows, (r + 1) * finish_rows)
        x_rows = x_ref[rows, :]
        y = o_ref[rows, :] + partial_out(x_rows, *last_w)
        o_ref[rows, :] = _layer_norm_rows(alpha * x_rows + gate * y, g, b_ln)


def _ffn(x2d, mod, w1, w3, w2, ln_g, ln_b, *, sub, seq, alpha, tm=1024, tf=256, finish_rows=256):
    n_tok, d = x2d.shape
    f = w1.shape[1]
    tm = min(tm, seq)
    assert f % (2 * tf) == 0 and f // tf >= 4 and tm % finish_rows == 0
    kern = functools.partial(_ffn_kernel, sub=sub, tiles_per_batch=seq // tm, alpha=alpha,
                             finish_rows=finish_rows, tf=tf)
    hbm = pl.BlockSpec(memory_space=pl.ANY)
    return pl.pallas_call(
        kern,
        grid=(n_tok // tm,),
        in_specs=[
            pl.BlockSpec((tm, d), lambda i: (i, 0)),
            pl.BlockSpec(mod.shape, lambda i: (0, 0, 0)),
            pl.BlockSpec((1, d), lambda i: (0, 0)),
            pl.BlockSpec((1, d), lambda i: (0, 0)),
            hbm, hbm, hbm,
        ],
        out_specs=pl.BlockSpec((tm, d), lambda i: (i, 0)),
        out_shape=jax.ShapeDtypeStruct((n_tok, d), F32),
        scratch_shapes=[
            pltpu.VMEM((2, d, tf), w1.dtype),
            pltpu.VMEM((2, d, tf), w3.dtype),
            pltpu.VMEM((2, tf, d), w2.dtype),
            pltpu.SemaphoreType.DMA((3, 2)),
        ],
        compiler_params=_params(("arbitrary",)),
        name=f"ffn{sub}",
    )(x2d, mod, ln_g.reshape(1, d), ln_b.reshape(1, d), w1, w3, w2)


def _inproj_kernel(x_hbm, mod_ref, w_ref, oa_ref, ob_ref, wbf_ref, x_buf, x_sem, *,
                   sub, tiles_per_batch, n_a, n_valid):
    i = pl.program_id(0)
    j = pl.program_id(1)
    n_tiles = pl.num_programs(0)
    bidx = i // tiles_per_batch
    tn = w_ref.shape[0]
    tm = x_buf.shape[1]
    slot = i % 2

    def x_copy(tile, to_slot):
        rows = pl.ds(pl.multiple_of(tile * tm, tm), tm)
        return pltpu.make_async_copy(x_hbm.at[rows, :], x_buf.at[to_slot], x_sem.at[to_slot])

    @pl.when(j == 0)
    def _():
        @pl.when(i == 0)
        def _():
            x_copy(0, 0).start()

        x_copy(i, slot).wait()

        @pl.when(i + 1 < n_tiles)
        def _():
            x_copy(i + 1, 1 - slot).start()

    @pl.when(i == 0)
    def _():
        out_col = j * tn + lax.broadcasted_iota(jnp.int32, w_ref.shape, 0)
        wbf_ref[j] = jnp.where(out_col < n_valid, w_ref[...], 0.0).astype(BF16)

    def project():
        shift = mod_ref[bidx, 3 * sub:3 * sub + 1, :]
        scale1 = 1.0 + mod_ref[bidx, 3 * sub + 1:3 * sub + 2, :]
        u = (x_buf[slot] * scale1 + shift).astype(BF16)
        return lax.dot_general(u, wbf_ref[j], (((1,), (1,)), ((), ())), preferred_element_type=F32)

    @pl.when(j < n_a)
    def _():
        oa_ref[...] = project().astype(BF16)

    @pl.when(j >= n_a)
    def _():
        ob_ref[...] = project()


def _inproj(x2d, mod, w_in_t, *, sub, seq, tm=1024, tn=512):
    n_tok, d = x2d.shape
    n_a = PA_COLS // tn
    n_b = PB_COLS // tn
    n_blocks = n_a + n_b
    kern = functools.partial(_inproj_kernel, sub=sub, tiles_per_batch=seq // tm, n_a=n_a,
                             n_valid=w_in_t.shape[0])
    return pl.pallas_call(
        kern,
        grid=(n_tok // tm, n_blocks),
        in_specs=[
            pl.BlockSpec(memory_space=pl.ANY),
            pl.BlockSpec(mod.shape, lambda i, j: (0, 0, 0)),
            pl.BlockSpec((tn, d), lambda i, j: (jnp.where(i == 0, j, n_blocks - 1), 0)),
        ],
        out_specs=[
            pl.BlockSpec((tm, tn), lambda i, j: (i, jnp.minimum(j, n_a - 1))),
            pl.BlockSpec((tm, tn), lambda i, j: (i, jnp.maximum(j - n_a, 0))),
        ],
        out_shape=[
            jax.ShapeDtypeStruct((n_tok, PA_COLS), BF16),
            jax.ShapeDtypeStruct((n_tok, PB_COLS), F32),
        ],
        scratch_shapes=[
            pltpu.VMEM((n_blocks, tn, d), BF16),
            pltpu.VMEM((2, tm, d), x2d.dtype),
            pltpu.SemaphoreType.DMA((2,)),
        ],
        compiler_params=_params(("arbitrary", "arbitrary")),
        name="inproj",
    )(x2d, mod, w_in_t)


def _lane_scan(v, op, chunk):
    lane = lax.broadcasted_iota(jnp.int32, v.shape, 1) % chunk
    ident = 0.0 if op is jnp.add else NEG_BIG
    s = 1
    while s < chunk:
        shifted = pltpu.roll(v, s, axis=1)
        v = op(v, jnp.where(lane >= s, shifted, ident))
        s *= 2
    return v


def _mlstm_gate_scans(bi, gates_ref, gb_ref, scan_ref, chunk):
    seq = gates_ref.shape[1]
    for c in range(seq // LANES):
        cols = slice(c * LANES, (c + 1) * LANES)
        scan_ref[bi, 0, :, cols] = gates_ref[bi, cols, :].T[0:SUBLANES, :]
    z = scan_ref[bi, 0] + gb_ref[:, 0:1]
    logi = pltpu.roll(z, ML_HEADS, axis=0)
    logf = jnp.minimum(z, 0.0) - jnp.log1p(jnp.exp(-jnp.abs(z)))
    bcum = _lane_scan(logf, jnp.add, chunk)
    r = logi - bcum
    scan_ref[bi, 0] = bcum
    scan_ref[bi, 1] = r
    scan_ref[bi, 2] = _lane_scan(r, jnp.maximum, chunk)


def _mlstm_kernel(qk_ref, v_ref, og_ref, gates_ref, cw_ref, cb_ref, gb_ref, ng_ref, sel_ref, cbias_ref,
                  o_ref, c_ref, m_ref, prev_ref, scan_ref, *, chunk):
    t = pl.program_id(1)

    @pl.when(t == 0)
    def _():
        c_ref[...] = jnp.zeros_like(c_ref)
        m_ref[...] = jnp.zeros_like(m_ref)
        prev_ref[...] = jnp.zeros_like(prev_ref)
        for bi in range(qk_ref.shape[0]):
            _mlstm_gate_scans(bi, gates_ref, gb_ref, scan_ref, chunk)

    _mlstm_step(t, qk_ref, v_ref, og_ref, cw_ref, cb_ref, ng_ref, sel_ref, cbias_ref, o_ref,
                c_ref, m_ref, prev_ref, scan_ref, chunk)


def _mlstm_step(t, qk_ref, v_ref, og_ref, cw_ref, cb_ref, ng_ref, sel_ref, cbias_ref, o_ref,
                c_ref, m_ref, prev_ref, scan_ref, chunk):
    hq = ML_HEADS * ML_QK_DIM
    n_rows = qk_ref.shape[0]
    chains = [(bi, h) for bi in range(n_rows) for h in range(ML_HEADS)]
    head_cols = lambda h: slice(h * ML_V_DIM, (h + 1) * ML_V_DIM)
    causal_bias = cbias_ref[...]
    ones_col = (lax.broadcasted_iota(jnp.int32, (chunk, LANES), 1) == 0).astype(BF16)
    lanes = pl.ds(pl.multiple_of(t * chunk, chunk), chunk)

    qk = []
    for bi in range(n_rows):
        qk_now = qk_ref[bi]
        taps = jnp.dot(sel_ref[...], jnp.concatenate([prev_ref[bi], qk_now], axis=0),
                       preferred_element_type=F32)
        prev_ref[bi] = qk_now
        conv = cb_ref[...]
        for tap in range(CONV_WIDTH):
            conv = conv + cw_ref[tap:tap + 1, :] * taps[tap * chunk:(tap + 1) * chunk, :]
        qk.append(_silu(conv))

    r, cols, s_old = [], [], []
    for bi in range(n_rows):
        bcum = scan_ref[bi, 0, :, lanes]
        r_b = scan_ref[bi, 1, :, lanes]
        cm = scan_ref[bi, 2, :, lanes]
        m_in = m_ref[bi, :, 0:1]
        m_all = jnp.maximum(m_in, cm)
        m_last = m_all[:, chunk - 1:chunk]
        m_ref[bi] = jnp.broadcast_to(bcum[:, chunk - 1:chunk] + m_last, (SUBLANES, LANES))
        s_inter = jnp.exp(m_in - m_all)
        e_neg_m = jnp.exp(-(bcum + m_all))
        w_last = jnp.exp(r_b - m_last)
        stack = jnp.concatenate(
            [m_all, s_inter, e_neg_m, w_last, jnp.zeros((LANES - 4 * SUBLANES, chunk), F32)], axis=0)
        r.append(r_b)
        cols.append(stack.T)
        s_old.append(jnp.exp(m_in - m_last))

    def col(bi, h, which):
        lane = which * SUBLANES + ML_HEADS + h
        return cols[bi][:, lane:lane + 1]

    q_h = {c: qk[c[0]][:, c[1] * ML_QK_DIM:(c[1] + 1) * ML_QK_DIM].astype(BF16) for c in chains}
    k_f = {c: qk[c[0]][:, hq + c[1] * ML_QK_DIM:hq + (c[1] + 1) * ML_QK_DIM] * (ML_QK_DIM ** -0.5)
           for c in chains}
    v_aug = {c: jnp.concatenate([v_ref[c[0], :, head_cols(c[1])], ones_col], axis=1) for c in chains}
    scores = {c: lax.dot_general(q_h[c], k_f[c].astype(BF16), (((1,), (1,)), ((), ())),
                                 preferred_element_type=F32) for c in chains}
    p = {}
    for bi, h in chains:
        decay = jnp.exp(r[bi][ML_HEADS + h:ML_HEADS + h + 1, :] - col(bi, h, 0) + causal_bias)
        p[bi, h] = (decay * scores[bi, h]).astype(BF16)
    c_old = {c: c_ref[c[0], c[1]] for c in chains}
    tot = {c: (jnp.dot(p[c], v_aug[c], preferred_element_type=F32)
               + col(c[0], c[1], 1) * jnp.dot(q_h[c], c_old[c].astype(BF16), preferred_element_type=F32))
           for c in chains}
    for bi, h in chains:
        k_w = (k_f[bi, h] * col(bi, h, 3)).astype(BF16)
        c_ref[bi, h] = s_old[bi][ML_HEADS + h:ML_HEADS + h + 1, :] * c_old[bi, h] + lax.dot_general(
            k_w, v_aug[bi, h], (((0,), (0,)), ((), ())), preferred_element_type=F32)
    for bi, h in chains:
        num = tot[bi, h][:, :ML_V_DIM]
        den = tot[bi, h][:, ML_V_DIM:ML_V_DIM + 1]
        hid = num / jnp.maximum(jnp.abs(den), col(bi, h, 2))
        mu = jnp.mean(hid, axis=-1, keepdims=True)
        hc = hid - mu
        var = jnp.mean(hc * hc, axis=-1, keepdims=True)
        hn = hc * lax.rsqrt(var + LN_EPS) * ng_ref[:, head_cols(h)]
        gate = jax.nn.sigmoid(og_ref[bi, :, head_cols(h)].astype(F32))
        o_ref[bi, :, head_cols(h)] = (hn * gate).astype(o_ref.dtype)


def _mlstm(pa, pb, conv_w, conv_b, b_igate, b_fgate, norm_g, *, chunk=128, rows_per_step=2):
    bsz, seq, _ = pa.shape
    rb = rows_per_step
    gate_bias = jnp.broadcast_to(jnp.concatenate([b_igate, b_fgate])[:, None], (SUBLANES, LANES))
    t_idx = jnp.arange(CONV_WIDTH * chunk) % chunk
    tap_idx = jnp.arange(CONV_WIDTH * chunk) // chunk
    shift_sel = (jnp.arange(2 * chunk)[None, :]
                 == (chunk + t_idx - (CONV_WIDTH - 1) + tap_idx)[:, None]).astype(BF16)
    causal_bias = jnp.where(jnp.arange(chunk)[None, :] <= jnp.arange(chunk)[:, None], 0.0, NEG_BIG).astype(F32)
    kern = functools.partial(_mlstm_kernel, chunk=chunk)
    return pl.pallas_call(
        kern,
        grid=(bsz // rb, seq // chunk),
        in_specs=[
            pl.BlockSpec((rb, chunk, ML_QK_COLS), lambda b, t: (b, t, 0)),
            pl.BlockSpec((rb, chunk, ML_V_COLS), lambda b, t: (b, t, ML_QK_COLS // ML_V_COLS)),
            pl.BlockSpec((rb, chunk, ML_V_COLS), lambda b, t: (b, t, ML_QK_COLS // ML_V_COLS + 1)),
            pl.BlockSpec((rb, seq, LANES), lambda b, t: (b, 0, PB_GATES // LANES)),
            pl.BlockSpec((CONV_WIDTH, ML_QK_COLS), lambda b, t: (0, 0)),
            pl.BlockSpec((1, ML_QK_COLS), lambda b, t: (0, 0)),
            pl.BlockSpec((SUBLANES, LANES), lambda b, t: (0, 0)),
            pl.BlockSpec((1, ML_V_COLS), lambda b, t: (0, 0)),
            pl.BlockSpec(shift_sel.shape, lambda b, t: (0, 0)),
            pl.BlockSpec(causal_bias.shape, lambda b, t: (0, 0)),
        ],
        out_specs=pl.BlockSpec((rb, chunk, ML_V_COLS), lambda b, t: (b, t, 0)),
        out_shape=jax.ShapeDtypeStruct((bsz, seq, ML_V_COLS), BF16),
        scratch_shapes=[
            pltpu.VMEM((rb, ML_HEADS, ML_QK_DIM, ML_V_DIM + LANES), F32),
            pltpu.VMEM((rb, SUBLANES, LANES), F32),
            pltpu.VMEM((rb, chunk, ML_QK_COLS), pa.dtype),
            pltpu.VMEM((rb, 3, SUBLANES, seq), F32),
        ],
        compiler_params=_params(("arbitrary", "arbitrary")),
        name="mlstm",
    )(pa, pa, pa, pb, conv_w, conv_b.reshape(1, -1), gate_bias, norm_g.reshape(1, -1),
      shift_sel, causal_bias)


def _rope_lanes(x, cos, sin_lo, sin_hi):
    half = MLA_ROPE_DIM // 2
    return x * cos + pltpu.roll(x, LANES - half, axis=1) * sin_lo + pltpu.roll(x, half, axis=1) * sin_hi


def _rms_rows(x, g):
    return x * lax.rsqrt(jnp.mean(x * x, axis=-1, keepdims=True) + LN_EPS) * g


def _mla_prep_kernel(xa_ref, xb_ref, qg_ref, kvg_ref, wq_ref, wkv_ref, cos_ref, sl_ref, sh_ref,
                     q_ref, k_ref, v_ref, *, q_scale, q_shift, kv_cols, kv_shift, kr_group, kr_shift):
    cos = cos_ref[...]
    sin_lo = sl_ref[...]
    sin_hi = sh_ref[...]
    xa = xa_ref[...]
    xb = xb_ref[:, :kv_cols]
    c_q = pltpu.roll(xa, xa.shape[1] - q_shift, axis=1)[:, :Q_LORA_RANK]
    c_kv = pltpu.roll(xb, xb.shape[1] - kv_shift, axis=1)[:, :KV_LORA_RANK]
    q_lat = _rms_rows(c_q, qg_ref[...]).astype(BF16)
    kv_lat = _rms_rows(c_kv, kvg_ref[...]).astype(BF16)
    k_r = pltpu.roll(xb_ref[:, kr_group * LANES:(kr_group + 1) * LANES], LANES - kr_shift, axis=1)
    k_rope = _rope_lanes(k_r, cos, sin_lo, sin_hi).astype(k_ref.dtype)
    group = lambda h: slice(h * MLA_QK_GROUP, (h + 1) * MLA_QK_GROUP)
    q_all = [jnp.dot(q_lat, wq_ref[:, group(h)], preferred_element_type=F32) for h in range(MLA_HEADS)]
    kv_all = [jnp.dot(kv_lat, wkv_ref[:, group(h)], preferred_element_type=F32) for h in range(MLA_HEADS)]
    for h in range(MLA_HEADS):
        lo = h * MLA_QK_GROUP
        q_h = q_all[h] * q_scale
        q_ref[:, lo:lo + LANES] = q_h[:, :LANES].astype(q_ref.dtype)
        q_ref[:, lo + LANES:lo + 2 * LANES] = _rope_lanes(q_h[:, LANES:], cos, sin_lo, sin_hi).astype(q_ref.dtype)
        k_ref[:, lo:lo + LANES] = kv_all[h][:, :LANES].astype(k_ref.dtype)
        k_ref[:, lo + LANES:lo + 2 * LANES] = k_rope
        v_ref[:, h * MLA_V_DIM:(h + 1) * MLA_V_DIM] = kv_all[h][:, LANES:].astype(v_ref.dtype)


def _rope_tables(seq):
    half = MLA_ROPE_DIM // 2
    inv = ROPE_THETA ** (-jnp.arange(half, dtype=F32) / half)
    ang = jnp.arange(seq, dtype=F32)[:, None] * inv[None, :]
    cos, sin = jnp.cos(ang), jnp.sin(ang)
    zero = jnp.zeros_like(cos)
    cos_t = jnp.concatenate([cos, cos, zero, zero], axis=1)
    sin_lo = jnp.concatenate([-sin, zero, zero, zero], axis=1)
    sin_hi = jnp.concatenate([zero, sin, zero, zero], axis=1)
    return cos_t, sin_lo, sin_hi


def _mla_prep(pb2d, q_norm_g, kv_norm_g, w_uq, w_ukv, *, seq, tm=512):
    n_tok = pb2d.shape[0]
    dq = MLA_NOPE_DIM + MLA_ROPE_DIM
    xb_width = 512
    xb_start = (PB_CKV // xb_width) * xb_width
    assert PB_VALID <= xb_start + xb_width and PB_COLS % xb_width == 0
    xa_cols = -(-PB_CKV // LANES) * LANES
    ckv_lo = PB_CKV - xb_start
    kv_cols = -(-(ckv_lo + KV_LORA_RANK) // LANES) * LANES
    kr_lo = PB_KR - xb_start
    assert kr_lo % LANES + MLA_ROPE_DIM <= LANES
    wq = w_uq.reshape(Q_LORA_RANK, MLA_HEADS, dq)
    wq = jnp.pad(wq, ((0, 0), (0, 0), (0, MLA_QK_GROUP - dq)))
    wq = wq.reshape(Q_LORA_RANK, MLA_HEADS * MLA_QK_GROUP).astype(BF16)
    wkv = w_ukv.astype(BF16)
    cos_t, sin_lo, sin_hi = _rope_tables(seq)
    tiles_per_seq = seq // tm
    pos = lambda i: (i % tiles_per_seq, 0)
    kern = functools.partial(_mla_prep_kernel, q_scale=math.log2(math.e) / math.sqrt(dq),
                             q_shift=PB_CQ, kv_cols=kv_cols, kv_shift=ckv_lo,
                             kr_group=kr_lo // LANES, kr_shift=kr_lo % LANES)
    qk_cols = MLA_HEADS * MLA_QK_GROUP
    const = lambda i: (0, 0)
    return pl.pallas_call(
        kern,
        grid=(n_tok // tm,),
        in_specs=[
            pl.BlockSpec((tm, xa_cols), lambda i: (i, 0)),
            pl.BlockSpec((tm, xb_width), lambda i: (i, xb_start // xb_width)),
            pl.BlockSpec((1, Q_LORA_RANK), const),
            pl.BlockSpec((1, KV_LORA_RANK), const),
            pl.BlockSpec((Q_LORA_RANK, qk_cols), const),
            pl.BlockSpec((KV_LORA_RANK, qk_cols), const),
            pl.BlockSpec((tm, LANES), pos),
            pl.BlockSpec((tm, LANES), pos),
            pl.BlockSpec((tm, LANES), pos),
        ],
        out_specs=[
            pl.BlockSpec((tm, qk_cols), lambda i: (i, 0)),
            pl.BlockSpec((tm, qk_cols), lambda i: (i, 0)),
            pl.BlockSpec((tm, MLA_HEADS * MLA_V_DIM), lambda i: (i, 0)),
        ],
        out_shape=[
            jax.ShapeDtypeStruct((n_tok, qk_cols), BF16),
            jax.ShapeDtypeStruct((n_tok, qk_cols), BF16),
            jax.ShapeDtypeStruct((n_tok, MLA_HEADS * MLA_V_DIM), BF16),
        ],
        compiler_params=_params(("arbitrary",)),
        name="mla_prep",
    )(pb2d, pb2d, q_norm_g.reshape(1, -1), kv_norm_g.reshape(1, -1), wq, wkv, cos_t, sin_lo, sin_hi)


def _flash_kernel(q_ref, k_ref, v_ref, o_ref, m_ref, acc_ref, *, tq, heads):
    qi = pl.program_id(2)
    m_ref[...] = jnp.full_like(m_ref, NEG_BIG)
    acc_ref[...] = jnp.zeros_like(acc_ref)
    ones_col = (lax.broadcasted_iota(jnp.int32, (tq, LANES), 1) == 0).astype(BF16)
    on_or_below_diag = (lax.broadcasted_iota(jnp.int32, (tq, tq), 1)
                        <= lax.broadcasted_iota(jnp.int32, (tq, tq), 0))

    def key_block(j, masked):
        rows = pl.ds(pl.multiple_of(j * tq, tq), tq)
        hs = range(heads)
        s = [lax.dot_general(q_ref[0, :, h * MLA_QK_GROUP:(h + 1) * MLA_QK_GROUP],
                             k_ref[0, rows, h * MLA_QK_GROUP:(h + 1) * MLA_QK_GROUP],
                             (((1,), (1,)), ((), ())), preferred_element_type=F32) for h in hs]
        if masked:
            s = [jnp.where(on_or_below_diag, s_h, NEG_BIG) for s_h in s]
        m_prev = [m_ref[h] for h in hs]
        m_new = [jnp.maximum(m_prev[h], jnp.max(s[h], axis=-1, keepdims=True)) for h in hs]
        p = [jnp.exp2(s[h] - jnp.concatenate([m_new[h]] * (tq // LANES), axis=1)).astype(BF16) for h in hs]
        alpha = [jnp.exp2(m_prev[h] - m_new[h]) for h in hs]
        for h in hs:
            v_aug = jnp.concatenate([v_ref[0, rows, h * MLA_V_DIM:(h + 1) * MLA_V_DIM], ones_col], axis=1)
            pv = jnp.dot(p[h], v_aug, preferred_element_type=F32)
            acc_ref[h] = jnp.concatenate([alpha[h], alpha[h]], axis=1) * acc_ref[h] + pv
            m_ref[h] = m_new[h]

    def body(j, carry):
        key_block(j, False)
        return carry

    lax.fori_loop(0, qi, body, 0)
    key_block(qi, True)
    for h in range(heads):
        acc = acc_ref[h]
        o_ref[0, :, h * MLA_V_DIM:(h + 1) * MLA_V_DIM] = (
            acc[:, :MLA_V_DIM] / acc[:, MLA_V_DIM:MLA_V_DIM + 1]).astype(o_ref.dtype)


def _flash(q, k, v, *, tq=512, heads=8):
    bsz, seq, _ = q.shape
    kern = functools.partial(_flash_kernel, tq=tq, heads=heads)
    return pl.pallas_call(
        kern,
        grid=(bsz, MLA_HEADS // heads, seq // tq),
        in_specs=[
            pl.BlockSpec((1, tq, heads * MLA_QK_GROUP), lambda b, g, qi: (b, qi, g)),
            pl.BlockSpec((1, seq, heads * MLA_QK_GROUP), lambda b, g, qi: (b, 0, g)),
            pl.BlockSpec((1, seq, heads * MLA_V_DIM), lambda b, g, qi: (b, 0, g)),
        ],
        out_specs=pl.BlockSpec((1, tq, heads * MLA_V_DIM), lambda b, g, qi: (b, qi, g)),
        out_shape=jax.ShapeDtypeStruct((bsz, seq, MLA_HEADS * MLA_V_DIM), BF16),
        scratch_shapes=[
            pltpu.VMEM((heads, tq, LANES), F32),
            pltpu.VMEM((heads, tq, MLA_V_DIM + LANES), F32),
        ],
        compiler_params=_params(("arbitrary", "arbitrary", "arbitrary")),
        name="flash",
    )(q, k, v)


def _outproj_kernel(x_ref, hml_ref, hmla_ref, mod_ref, w_ref, g_ref, b_ref, o_ref, wb_ref, *,
                    sub, tiles_per_batch, alpha, row_chunk, norm_rows):
    i = pl.program_id(0)
    bidx = i // tiles_per_batch
    k_ml = hml_ref.shape[1]

    @pl.when(i == 0)
    def _():
        def body(r, carry):
            rows = pl.ds(pl.multiple_of(r * row_chunk, row_chunk), row_chunk)
            wb_ref[rows, :] = w_ref[rows, :].astype(BF16)
            return carry

        lax.fori_loop(0, w_ref.shape[0] // row_chunk, body, 0)

    gate = 1.0 + mod_ref[bidx, 3 * sub + 2:3 * sub + 3, :]
    g = g_ref[...]
    b_ln = b_ref[...]
    w_top = wb_ref[0:k_ml, :]
    w_bot = wb_ref[k_ml:, :]
    for r in range(x_ref.shape[0] // norm_rows):
        rows = slice(r * norm_rows, (r + 1) * norm_rows)
        y = (jnp.dot(hml_ref[rows, :], w_top, preferred_element_type=F32)
             + jnp.dot(hmla_ref[rows, :], w_bot, preferred_element_type=F32))
        o_ref[rows, :] = _layer_norm_rows(alpha * x_ref[rows, :] + gate * y, g, b_ln)


def _outproj(x2d, hml, hmla, mod, w_out, ln_g, ln_b, *, sub, seq, alpha, tm=512, row_chunk=128,
             norm_rows=128):
    n_tok, d = x2d.shape
    kern = functools.partial(_outproj_kernel, sub=sub, tiles_per_batch=seq // tm, alpha=alpha,
                             row_chunk=row_chunk, norm_rows=norm_rows)
    return pl.pallas_call(
        kern,
        grid=(n_tok // tm,),
        in_specs=[
            pl.BlockSpec((tm, d), lambda i: (i, 0)),
            pl.BlockSpec((tm, hml.shape[1]), lambda i: (i, 0)),
            pl.BlockSpec((tm, hmla.shape[1]), lambda i: (i, 0)),
            pl.BlockSpec(mod.shape, lambda i: (0, 0, 0)),
            pl.BlockSpec(w_out.shape, lambda i: (0, 0), pipeline_mode=pl.Buffered(1)),
            pl.BlockSpec((1, d), lambda i: (0, 0)),
            pl.BlockSpec((1, d), lambda i: (0, 0)),
        ],
        out_specs=pl.BlockSpec((tm, d), lambda i: (i, 0)),
        out_shape=jax.ShapeDtypeStruct((n_tok, d), F32),
        scratch_shapes=[pltpu.VMEM(w_out.shape, BF16)],
        compiler_params=_params(("arbitrary",)),
        name="outproj",
    )(x2d, hml, hmla, mod, w_out, ln_g.reshape(1, d), ln_b.reshape(1, d))


def kernel(x, c, w_ada, b_ada, ffn1_w1, ffn1_w3, ffn1_w2, ln1_g, ln1_b, w_in, conv_w, conv_b, b_igate, b_fgate, ml_norm_g, q_norm_g, w_uq, kv_norm_g, w_ukv, w_out, ln2_g, ln2_b, ffn2_w1, ffn2_w3, ffn2_w2, ln3_g, ln3_b):
    bsz, seq, d = x.shape
    depth = w_ada.shape[0]
    alpha = (2.0 * depth) ** 0.25
    h2d = x.reshape(bsz * seq, d)
    for l in range(depth):
        mod = _adaln(c, w_ada[l], b_ada[l]).reshape(bsz, N_SUBLAYERS * 3, d)
        h2d = _ffn(h2d, mod, ffn1_w1[l], ffn1_w3[l], ffn1_w2[l], ln1_g[l], ln1_b[l],
                   sub=0, seq=seq, alpha=alpha)
        assert w_in.shape[2] == D_IN
        pa, pb = _inproj(h2d, mod, w_in[l].T, sub=1, seq=seq)
        hml = _mlstm(pa.reshape(bsz, seq, PA_COLS), pb.reshape(bsz, seq, PB_COLS), conv_w[l], conv_b[l],
                     b_igate[l], b_fgate[l], ml_norm_g[l])
        q, k, v = _mla_prep(pb, q_norm_g[l], kv_norm_g[l], w_uq[l], w_ukv[l], seq=seq)
        hmla = _flash(q.reshape(bsz, seq, -1), k.reshape(bsz, seq, -1), v.reshape(bsz, seq, -1))
        h2d = _outproj(h2d, hml.reshape(bsz * seq, -1), hmla.reshape(bsz * seq, -1), mod,
                       w_out[l], ln2_g[l], ln2_b[l], sub=1, seq=seq, alpha=alpha)
        h2d = _ffn(h2d, mod, ffn2_w1[l], ffn2_w3[l], ffn2_w2[l], ln3_g[l], ln3_b[l],
                   sub=2, seq=seq, alpha=alpha)
    return h2d.reshape(bsz, seq, d)
```

```python
import functools
import math

import jax
import jax.numpy as jnp
from jax import lax
from jax.experimental import pallas as pl
from jax.experimental.pallas import tpu as pltpu

F32 = jnp.float32
BF16 = jnp.bfloat16

ML_HEADS = 4
ML_QK_DIM = 128
ML_V_DIM = 256
CONV_WIDTH = 4
MLA_HEADS = 8
MLA_NOPE_DIM = 128
MLA_ROPE_DIM = 64
MLA_V_DIM = 128
Q_LORA_RANK = 512
KV_LORA_RANK = 256
ROPE_THETA = 10000.0
FFN_RES_WEIGHT = 0.5
N_SUBLAYERS = 3
LN_EPS = 1e-5

LANES = 128
SUBLANES = 8
VMEM_LIMIT_BYTES = 56 * 1024 * 1024

ML_QK_COLS = 2 * ML_HEADS * ML_QK_DIM
ML_V_COLS = ML_HEADS * ML_V_DIM
PA_COLS = ML_QK_COLS + 2 * ML_V_COLS
PB_GATES = 0
PB_CQ = PB_GATES + 2 * ML_HEADS
PB_CKV = PB_CQ + Q_LORA_RANK
PB_KR = PB_CKV + KV_LORA_RANK
PB_VALID = PB_KR + MLA_ROPE_DIM
PROJ_BLOCK = 512
PB_COLS = -(-PB_VALID // PROJ_BLOCK) * PROJ_BLOCK
D_IN = PA_COLS + PB_VALID
MLA_QK_GROUP = 2 * LANES
NEG_BIG = -1e30


def _params(semantics):
    return pltpu.CompilerParams(dimension_semantics=semantics, vmem_limit_bytes=VMEM_LIMIT_BYTES)


def _silu(v):
    return v * jax.nn.sigmoid(v)


def _layer_norm_rows(z, g, b):
    mu = jnp.mean(z, axis=-1, keepdims=True)
    zc = z - mu
    var = jnp.mean(zc * zc, axis=-1, keepdims=True)
    return zc * lax.rsqrt(var + LN_EPS) * g + b


def _adaln_kernel(c_ref, w_ref, b_ref, o_ref):
    sc = _silu(c_ref[...]).astype(BF16)
    o_ref[...] = jnp.dot(sc, w_ref[...].astype(BF16), preferred_element_type=F32) + b_ref[...]


def _adaln(c, w, b, *, tn=1024):
    bsz, d = c.shape
    n = w.shape[1]
    rows = -(-bsz // SUBLANES) * SUBLANES
    c_pad = jnp.pad(c, ((0, rows - bsz), (0, 0)))
    out = pl.pallas_call(
        _adaln_kernel,
        grid=(n // tn,),
        in_specs=[
            pl.BlockSpec((rows, d), lambda j: (0, 0)),
            pl.BlockSpec((d, tn), lambda j: (0, j)),
            pl.BlockSpec((1, tn), lambda j: (0, j)),
        ],
        out_specs=pl.BlockSpec((rows, tn), lambda j: (0, j)),
        out_shape=jax.ShapeDtypeStruct((rows, n), F32),
        compiler_params=_params(("arbitrary",)),
        name="adaln",
    )(c_pad, w, b.reshape(1, n))
    return out[:bsz]


def _ffn_kernel(x_ref, mod_ref, g_ref, b_ref, w1_hbm, w3_hbm, w2_hbm, o_ref,
                w1_buf, w3_buf, w2_buf, sem, *, sub, tiles_per_batch, alpha, finish_rows, tf):
    i = pl.program_id(0)
    n_tiles = pl.num_programs(0)
    bidx = i // tiles_per_batch
    tm = x_ref.shape[0]
    n_f = w1_hbm.shape[1] // tf

    def chunk_copies(c, slot):
        cols = pl.ds(pl.multiple_of(c * tf, tf), tf)
        return (pltpu.make_async_copy(w1_hbm.at[:, cols], w1_buf.at[slot], sem.at[0, slot]),
                pltpu.make_async_copy(w3_hbm.at[:, cols], w3_buf.at[slot], sem.at[1, slot]),
                pltpu.make_async_copy(w2_hbm.at[cols, :], w2_buf.at[slot], sem.at[2, slot]))

    def start_chunk(c, slot):
        for cp in chunk_copies(c, slot):
            cp.start()

    def wait_chunk(c, slot):
        for cp in chunk_copies(c, slot):
            cp.wait()

    @pl.when(i == 0)
    def _():
        start_chunk(0, 0)

    shift = mod_ref[bidx, 3 * sub:3 * sub + 1, :]
    scale1 = 1.0 + mod_ref[bidx, 3 * sub + 1:3 * sub + 2, :]

    def partial_out(x_rows, w1, w3, w2):
        u = (x_rows * scale1 + shift).astype(BF16)
        a = jnp.dot(u, w1, preferred_element_type=F32)
        b = jnp.dot(u, w3, preferred_element_type=F32)
        h = (_silu(a) * b).astype(BF16)
        return jnp.dot(h, w2, preferred_element_type=F32)

    def chunk_weights(slot):
        return (w1_buf[slot].astype(BF16), w3_buf[slot].astype(BF16), w2_buf[slot].astype(BF16))

    start_chunk(1, 1)
    wait_chunk(0, 0)
    o_ref[...] = partial_out(x_ref[...], *chunk_weights(0))

    def chunk_pair(k, carry):
        c1 = 2 * k + 1
        start_chunk(c1 + 1, 0)
        wait_chunk(c1, 1)
        o_ref[...] += partial_out(x_ref[...], *chunk_weights(1))
        start_chunk(c1 + 2, 1)
        wait_chunk(c1 + 1, 0)
        o_ref[...] += partial_out(x_ref[...], *chunk_weights(0))
        return carry

    lax.fori_loop(0, (n_f - 2) // 2, chunk_pair, 0)

    @pl.when(i + 1 < n_tiles)
    def _():
        start_chunk(0, 0)

    wait_chunk(n_f - 1, 1)
    gate = FFN_RES_WEIGHT * (1.0 + mod_ref[bidx, 3 * sub + 2:3 * sub + 3, :])
    g = g_ref[...]
    b_ln = b_ref[...]
    last_w = chunk_weights(1)
    for r in range(tm // finish_rows):
        rows = slice(r * finish_rows, (r + 1) * finish_rows)
        x_rows = x_ref[rows, :]
        y = o_ref[rows, :] + partial_out(x_rows, *last_w)
        o_ref[rows, :] = _layer_norm_rows(alpha * x_rows + gate * y, g, b_ln)


def _ffn(x2d, mod, w1, w3, w2, ln_g, ln_b, *, sub, seq, alpha, tm=1024, tf=256, finish_rows=256):
    n_tok, d = x2d.shape
    f = w1.shape[1]
    tm = min(tm, seq)
    assert f % (2 * tf) == 0 and f // tf >= 4 and tm % finish_rows == 0
    kern = functools.partial(_ffn_kernel, sub=sub, tiles_per_batch=seq // tm, alpha=alpha,
                             finish_rows=finish_rows, tf=tf)
    hbm = pl.BlockSpec(memory_space=pl.ANY)
    return pl.pallas_call(
        kern,
        grid=(n_tok // tm,),
        in_specs=[
            pl.BlockSpec((tm, d), lambda i: (i, 0)),
            pl.BlockSpec(mod.shape, lambda i: (0, 0, 0)),
            pl.BlockSpec((1, d), lambda i: (0, 0)),
            pl.BlockSpec((1, d), lambda i: (0, 0)),
            hbm, hbm, hbm,
        ],
        out_specs=pl.BlockSpec((tm, d), lambda i: (i, 0)),
        out_shape=jax.ShapeDtypeStruct((n_tok, d), F32),
        scratch_shapes=[
            pltpu.VMEM((2, d, tf), w1.dtype),
            pltpu.VMEM((2, d, tf), w3.dtype),
            pltpu.VMEM((2, tf, d), w2.dtype),
            pltpu.SemaphoreType.DMA((3, 2)),
        ],
        compiler_params=_params(("arbitrary",)),
        name=f"ffn{sub}",
    )(x2d, mod, ln_g.reshape(1, d), ln_b.reshape(1, d), w1, w3, w2)


def _inproj_kernel(x_hbm, mod_ref, w_ref, oa_ref, ob_ref, wbf_ref, x_buf, x_sem, *,
                   sub, tiles_per_batch, n_a, n_valid):
    i = pl.program_id(0)
    j = pl.program_id(1)
    n_tiles = pl.num_programs(0)
    bidx = i // tiles_per_batch
    tn = w_ref.shape[0]
    tm = x_buf.shape[1]
    slot = i % 2

    def x_copy(tile, to_slot):
        rows = pl.ds(pl.multiple_of(tile * tm, tm), tm)
        return pltpu.make_async_copy(x_hbm.at[rows, :], x_buf.at[to_slot], x_sem.at[to_slot])

    @pl.when(j == 0)
    def _():
        @pl.when(i == 0)
        def _():
            x_copy(0, 0).start()

        x_copy(i, slot).wait()

        @pl.when(i + 1 < n_tiles)
        def _():
            x_copy(i + 1, 1 - slot).start()

    @pl.when(i == 0)
    def _():
        out_col = j * tn + lax.broadcasted_iota(jnp.int32, w_ref.shape, 0)
        wbf_ref[j] = jnp.where(out_col < n_valid, w_ref[...], 0.0).astype(BF16)

    def project():
        shift = mod_ref[bidx, 3 * sub:3 * sub + 1, :]
        scale1 = 1.0 + mod_ref[bidx, 3 * sub + 1:3 * sub + 2, :]
        u = (x_buf[slot] * scale1 + shift).astype(BF16)
        return lax.dot_general(u, wbf_ref[j], (((1,), (1,)), ((), ())), preferred_element_type=F32)

    @pl.when(j < n_a)
    def _():
        oa_ref[...] = project().astype(BF16)

    @pl.when(j >= n_a)
    def _():
        ob_ref[...] = project()


def _inproj(x2d, mod, w_in_t, *, sub, seq, tm=1024, tn=PROJ_BLOCK):
    n_tok, d = x2d.shape
    n_a = PA_COLS // tn
    n_b = PB_COLS // tn
    n_blocks = n_a + n_b
    kern = functools.partial(_inproj_kernel, sub=sub, tiles_per_batch=seq // tm, n_a=n_a,
                             n_valid=w_in_t.shape[0])
    return pl.pallas_call(
        kern,
        grid=(n_tok // tm, n_blocks),
        in_specs=[
            pl.BlockSpec(memory_space=pl.ANY),
            pl.BlockSpec(mod.shape, lambda i, j: (0, 0, 0)),
            pl.BlockSpec((tn, d), lambda i, j: (jnp.where(i == 0, j, n_blocks - 1), 0)),
        ],
        out_specs=[
            pl.BlockSpec((tm, tn), lambda i, j: (i, jnp.minimum(j, n_a - 1))),
            pl.BlockSpec((tm, tn), lambda i, j: (i, jnp.maximum(j - n_a, 0))),
        ],
        out_shape=[
            jax.ShapeDtypeStruct((n_tok, PA_COLS), BF16),
            jax.ShapeDtypeStruct((n_tok, PB_COLS), F32),
        ],
        scratch_shapes=[
            pltpu.VMEM((n_blocks, tn, d), BF16),
            pltpu.VMEM((2, tm, d), x2d.dtype),
            pltpu.SemaphoreType.DMA((2,)),
        ],
        compiler_params=_params(("arbitrary", "arbitrary")),
        name="inproj",
    )(x2d, mod, w_in_t)


def _lane_scan(v, op, chunk):
    lane = lax.broadcasted_iota(jnp.int32, v.shape, 1) % chunk
    ident = 0.0 if op is jnp.add else NEG_BIG
    s = 1
    while s < chunk:
        shifted = pltpu.roll(v, s, axis=1)
        v = op(v, jnp.where(lane >= s, shifted, ident))
        s *= 2
    return v


def _mlstm_gate_scans(bi, gates_ref, gb_ref, scan_ref, chunk):
    seq = gates_ref.shape[1]
    for c in range(seq // LANES):
        cols = slice(c * LANES, (c + 1) * LANES)
        scan_ref[bi, 0, :, cols] = gates_ref[bi, cols, :].T[0:SUBLANES, :]
    z = scan_ref[bi, 0] + gb_ref[:, 0:1]
    logi = pltpu.roll(z, ML_HEADS, axis=0)
    logf = jnp.minimum(z, 0.0) - jnp.log1p(jnp.exp(-jnp.abs(z)))
    bcum = _lane_scan(logf, jnp.add, chunk)
    r = logi - bcum
    scan_ref[bi, 0] = bcum
    scan_ref[bi, 1] = r
    scan_ref[bi, 2] = _lane_scan(r, jnp.maximum, chunk)


def _mlstm_kernel(qk_ref, v_ref, og_ref, gates_ref, cw_ref, cb_ref, gb_ref, ng_ref, sel_ref, cbias_ref,
                  o_ref, c_ref, m_ref, prev_ref, scan_ref, *, chunk):
    t = pl.program_id(1)

    @pl.when(t == 0)
    def _():
        c_ref[...] = jnp.zeros_like(c_ref)
        m_ref[...] = jnp.zeros_like(m_ref)
        prev_ref[...] = jnp.zeros_like(prev_ref)
        for bi in range(qk_ref.shape[0]):
            _mlstm_gate_scans(bi, gates_ref, gb_ref, scan_ref, chunk)

    _mlstm_step(t, qk_ref, v_ref, og_ref, cw_ref, cb_ref, ng_ref, sel_ref, cbias_ref, o_ref,
                c_ref, m_ref, prev_ref, scan_ref, chunk)


def _mlstm_step(t, qk_ref, v_ref, og_ref, cw_ref, cb_ref, ng_ref, sel_ref, cbias_ref, o_ref,
                c_ref, m_ref, prev_ref, scan_ref, chunk):
    hq = ML_HEADS * ML_QK_DIM
    n_rows = qk_ref.shape[0]
    chains = [(bi, h) for bi in range(n_rows) for h in range(ML_HEADS)]
    head_cols = lambda h: slice(h * ML_V_DIM, (h + 1) * ML_V_DIM)
    causal_bias = cbias_ref[...]
    ones_col = (lax.broadcasted_iota(jnp.int32, (chunk, LANES), 1) == 0).astype(BF16)
    lanes = pl.ds(pl.multiple_of(t * chunk, chunk), chunk)

    qk = []
    for bi in range(n_rows):
        qk_now = qk_ref[bi]
        taps = jnp.dot(sel_ref[...], jnp.concatenate([prev_ref[bi], qk_now], axis=0),
                       preferred_element_type=F32)
        prev_ref[bi] = qk_now
        conv = cb_ref[...]
        for tap in range(CONV_WIDTH):
            conv = conv + cw_ref[tap:tap + 1, :] * taps[tap * chunk:(tap + 1) * chunk, :]
        qk.append(_silu(conv))

    r, cols, s_old = [], [], []
    for bi in range(n_rows):
        bcum = scan_ref[bi, 0, :, lanes]
        r_b = scan_ref[bi, 1, :, lanes]
        cm = scan_ref[bi, 2, :, lanes]
        m_in = m_ref[bi, :, 0:1]
        m_all = jnp.maximum(m_in, cm)
        m_last = m_all[:, chunk - 1:chunk]
        m_ref[bi] = jnp.broadcast_to(bcum[:, chunk - 1:chunk] + m_last, (SUBLANES, LANES))
        s_inter = jnp.exp(m_in - m_all)
        e_neg_m = jnp.exp(-(bcum + m_all))
        w_last = jnp.exp(r_b - m_last)
        stack = jnp.concatenate(
            [m_all, s_inter, e_neg_m, w_last, jnp.zeros((LANES - 4 * SUBLANES, chunk), F32)], axis=0)
        r.append(r_b)
        cols.append(stack.T)
        s_old.append(jnp.exp(m_in - m_last))

    def col(bi, h, which):
        lane = which * SUBLANES + ML_HEADS + h
        return cols[bi][:, lane:lane + 1]

    q_h = {c: qk[c[0]][:, c[1] * ML_QK_DIM:(c[1] + 1) * ML_QK_DIM].astype(BF16) for c in chains}
    k_f = {c: qk[c[0]][:, hq + c[1] * ML_QK_DIM:hq + (c[1] + 1) * ML_QK_DIM] * (ML_QK_DIM ** -0.5)
           for c in chains}
    v_aug = {c: jnp.concatenate([v_ref[c[0], :, head_cols(c[1])], ones_col], axis=1) for c in chains}
    scores = {c: lax.dot_general(q_h[c], k_f[c].astype(BF16), (((1,), (1,)), ((), ())),
                                 preferred_element_type=F32) for c in chains}
    p = {}
    for bi, h in chains:
        decay = jnp.exp(r[bi][ML_HEADS + h:ML_HEADS + h + 1, :] - col(bi, h, 0) + causal_bias)
        p[bi, h] = (decay * scores[bi, h]).astype(BF16)
    c_old = {c: c_ref[c[0], c[1]] for c in chains}
    tot = {c: (jnp.dot(p[c], v_aug[c], preferred_element_type=F32)
               + col(c[0], c[1], 1) * jnp.dot(q_h[c], c_old[c].astype(BF16), preferred_element_type=F32))
           for c in chains}
    for bi, h in chains:
        k_w = (k_f[bi, h] * col(bi, h, 3)).astype(BF16)
        c_ref[bi, h] = s_old[bi][ML_HEADS + h:ML_HEADS + h + 1, :] * c_old[bi, h] + lax.dot_general(
            k_w, v_aug[bi, h], (((0,), (0,)), ((), ())), preferred_element_type=F32)
    for bi, h in chains:
        num = tot[bi, h][:, :ML_V_DIM]
        den = tot[bi, h][:, ML_V_DIM:ML_V_DIM + 1]
        hid = num / jnp.maximum(jnp.abs(den), col(bi, h, 2))
        mu = jnp.mean(hid, axis=-1, keepdims=True)
        hc = hid - mu
        var = jnp.mean(hc * hc, axis=-1, keepdims=True)
        hn = hc * lax.rsqrt(var + LN_EPS) * ng_ref[:, head_cols(h)]
        gate = jax.nn.sigmoid(og_ref[bi, :, head_cols(h)].astype(F32))
        o_ref[bi, :, head_cols(h)] = (hn * gate).astype(o_ref.dtype)


def _mlstm(pa, pb, conv_w, conv_b, b_igate, b_fgate, norm_g, *, chunk=128, rows_per_step=2):
    bsz, seq, _ = pa.shape
    rb = rows_per_step
    gate_bias = jnp.broadcast_to(jnp.concatenate([b_igate, b_fgate])[:, None], (SUBLANES, LANES))
    t_idx = jnp.arange(CONV_WIDTH * chunk) % chunk
    tap_idx = jnp.arange(CONV_WIDTH * chunk) // chunk
    shift_sel = (jnp.arange(2 * chunk)[None, :]
                 == (chunk + t_idx - (CONV_WIDTH - 1) + tap_idx)[:, None]).astype(BF16)
    causal_bias = jnp.where(jnp.arange(chunk)[None, :] <= jnp.arange(chunk)[:, None], 0.0, NEG_BIG).astype(F32)
    kern = functools.partial(_mlstm_kernel, chunk=chunk)
    return pl.pallas_call(
        kern,
        grid=(bsz // rb, seq // chunk),
        in_specs=[
            pl.BlockSpec((rb, chunk, ML_QK_COLS), lambda b, t: (b, t, 0)),
            pl.BlockSpec((rb, chunk, ML_V_COLS), lambda b, t: (b, t, ML_QK_COLS // ML_V_COLS)),
            pl.BlockSpec((rb, chunk, ML_V_COLS), lambda b, t: (b, t, ML_QK_COLS // ML_V_COLS + 1)),
            pl.BlockSpec((rb, seq, LANES), lambda b, t: (b, 0, PB_GATES // LANES)),
            pl.BlockSpec((CONV_WIDTH, ML_QK_COLS), lambda b, t: (0, 0)),
            pl.BlockSpec((1, ML_QK_COLS), lambda b, t: (0, 0)),
            pl.BlockSpec((SUBLANES, LANES), lambda b, t: (0, 0)),
            pl.BlockSpec((1, ML_V_COLS), lambda b, t: (0, 0)),
            pl.BlockSpec(shift_sel.shape, lambda b, t: (0, 0)),
            pl.BlockSpec(causal_bias.shape, lambda b, t: (0, 0)),
        ],
        out_specs=pl.BlockSpec((rb, chunk, ML_V_COLS), lambda b, t: (b, t, 0)),
        out_shape=jax.ShapeDtypeStruct((bsz, seq, ML_V_COLS), BF16),
        scratch_shapes=[
            pltpu.VMEM((rb, ML_HEADS, ML_QK_DIM, ML_V_DIM + LANES), F32),
            pltpu.VMEM((rb, SUBLANES, LANES), F32),
            pltpu.VMEM((rb, chunk, ML_QK_COLS), pa.dtype),
            pltpu.VMEM((rb, 3, SUBLANES, seq), F32),
        ],
        compiler_params=_params(("arbitrary", "arbitrary")),
        name="mlstm",
    )(pa, pa, pa, pb, conv_w, conv_b.reshape(1, -1), gate_bias, norm_g.reshape(1, -1),
      shift_sel, causal_bias)


def _rope_lanes(x, cos, sin):
    return x * cos + pltpu.roll(x, LANES // 2, axis=1) * sin


def _rms_rows(x, g):
    return x * lax.rsqrt(jnp.mean(x * x, axis=-1, keepdims=True) + LN_EPS) * g


def _mla_prep_kernel(xa_ref, xb_ref, qg_ref, kvg_ref, wq_ref, wkv_ref, cos_ref, sin_ref,
                     q_ref, k_ref, v_ref, *, q_scale, q_shift, kv_cols, kv_shift, kr_group, kr_shift):
    cos = cos_ref[...]
    sin = sin_ref[...]
    half = MLA_ROPE_DIM // 2
    xa = xa_ref[...]
    xb = xb_ref[:, :kv_cols]
    c_q = pltpu.roll(xa, xa.shape[1] - q_shift, axis=1)[:, :Q_LORA_RANK]
    c_kv = pltpu.roll(xb, xb.shape[1] - kv_shift, axis=1)[:, :KV_LORA_RANK]
    q_lat = _rms_rows(c_q, qg_ref[...]).astype(BF16)
    kv_lat = _rms_rows(c_kv, kvg_ref[...]).astype(BF16)
    k_r = pltpu.roll(xb_ref[:, kr_group * LANES:(kr_group + 1) * LANES], LANES - kr_shift, axis=1)
    lane = lax.broadcasted_iota(jnp.int32, k_r.shape, 1)
    k_r = (jnp.where(lane < half, k_r, 0.0)
           + jnp.where((lane >= LANES // 2) & (lane < LANES // 2 + half), pltpu.roll(k_r, half, axis=1), 0.0))
    k_rope = _rope_lanes(k_r, cos, sin).astype(k_ref.dtype)
    group = lambda h: slice(h * MLA_QK_GROUP, (h + 1) * MLA_QK_GROUP)
    q_all = [jnp.dot(q_lat, wq_ref[:, group(h)], preferred_element_type=F32) for h in range(MLA_HEADS)]
    kv_all = [jnp.dot(kv_lat, wkv_ref[:, group(h)], preferred_element_type=F32) for h in range(MLA_HEADS)]
    for h in range(MLA_HEADS):
        lo = h * MLA_QK_GROUP
        q_h = q_all[h] * q_scale
        q_ref[:, lo:lo + LANES] = q_h[:, :LANES].astype(q_ref.dtype)
        q_ref[:, lo + LANES:lo + 2 * LANES] = _rope_lanes(q_h[:, LANES:], cos, sin).astype(q_ref.dtype)
        k_ref[:, lo:lo + LANES] = kv_all[h][:, :LANES].astype(k_ref.dtype)
        k_ref[:, lo + LANES:lo + 2 * LANES] = k_rope
        v_ref[:, h * MLA_V_DIM:(h + 1) * MLA_V_DIM] = kv_all[h][:, LANES:].astype(v_ref.dtype)


def _rope_tables(seq):
    half = MLA_ROPE_DIM // 2
    inv = ROPE_THETA ** (-jnp.arange(half, dtype=F32) / half)
    ang = jnp.arange(seq, dtype=F32)[:, None] * inv[None, :]
    cos, sin = jnp.cos(ang), jnp.sin(ang)
    zero = jnp.zeros_like(cos)
    return (jnp.concatenate([cos, zero, cos, zero], axis=1),
            jnp.concatenate([-sin, zero, sin, zero], axis=1))


def _mla_prep(pb2d, q_norm_g, kv_norm_g, w_uq, w_ukv, *, seq, tm=512):
    n_tok = pb2d.shape[0]
    dq = MLA_NOPE_DIM + MLA_ROPE_DIM
    xb_width = PROJ_BLOCK
    xb_start = (PB_CKV // xb_width) * xb_width
    assert PB_VALID <= xb_start + xb_width and PB_COLS % xb_width == 0
    xa_cols = -(-PB_CKV // LANES) * LANES
    ckv_lo = PB_CKV - xb_start
    kv_cols = -(-(ckv_lo + KV_LORA_RANK) // LANES) * LANES
    kr_lo = PB_KR - xb_start
    assert kr_lo % LANES + MLA_ROPE_DIM <= LANES
    half = MLA_ROPE_DIM // 2
    wq = w_uq.reshape(Q_LORA_RANK, MLA_HEADS, dq)
    gap = jnp.zeros(wq.shape[:2] + (LANES // 2 - half,), wq.dtype)
    wq = jnp.concatenate([wq[..., :MLA_NOPE_DIM], wq[..., MLA_NOPE_DIM:MLA_NOPE_DIM + half], gap,
                          wq[..., MLA_NOPE_DIM + half:], gap], axis=-1)
    wq = wq.reshape(Q_LORA_RANK, MLA_HEADS * MLA_QK_GROUP).astype(BF16)
    wkv = w_ukv.astype(BF16)
    cos_t, sin_t = _rope_tables(seq)
    tiles_per_seq = seq // tm
    pos = lambda i: (i % tiles_per_seq, 0)
    kern = functools.partial(_mla_prep_kernel, q_scale=math.log2(math.e) / math.sqrt(dq),
                             q_shift=PB_CQ, kv_cols=kv_cols, kv_shift=ckv_lo,
                             kr_group=kr_lo // LANES, kr_shift=kr_lo % LANES)
    qk_cols = MLA_HEADS * MLA_QK_GROUP
    const = lambda i: (0, 0)
    return pl.pallas_call(
        kern,
        grid=(n_tok // tm,),
        in_specs=[
            pl.BlockSpec((tm, xa_cols), lambda i: (i, 0)),
            pl.BlockSpec((tm, xb_width), lambda i: (i, xb_start // xb_width)),
            pl.BlockSpec((1, Q_LORA_RANK), const),
            pl.BlockSpec((1, KV_LORA_RANK), const),
            pl.BlockSpec((Q_LORA_RANK, qk_cols), const),
            pl.BlockSpec((KV_LORA_RANK, qk_cols), const),
            pl.BlockSpec((tm, LANES), pos),
            pl.BlockSpec((tm, LANES), pos),
        ],
        out_specs=[
            pl.BlockSpec((tm, qk_cols), lambda i: (i, 0)),
            pl.BlockSpec((tm, qk_cols), lambda i: (i, 0)),
            pl.BlockSpec((tm, MLA_HEADS * MLA_V_DIM), lambda i: (i, 0)),
        ],
        out_shape=[
            jax.ShapeDtypeStruct((n_tok, qk_cols), BF16),
            jax.ShapeDtypeStruct((n_tok, qk_cols), BF16),
            jax.ShapeDtypeStruct((n_tok, MLA_HEADS * MLA_V_DIM), BF16),
        ],
        compiler_params=_params(("arbitrary",)),
        name="mla_prep",
    )(pb2d, pb2d, q_norm_g.reshape(1, -1), kv_norm_g.reshape(1, -1), wq, wkv, cos_t, sin_t)


def _flash_kernel(q_ref, k_ref, v_ref, o_ref, m_ref, acc_ref, *, tq, heads):
    qi = pl.program_id(2)
    m_ref[...] = jnp.full_like(m_ref, NEG_BIG)
    acc_ref[...] = jnp.zeros_like(acc_ref)
    ones_col = (lax.broadcasted_iota(jnp.int32, (tq, LANES), 1) == 0).astype(BF16)
    on_or_below_diag = (lax.broadcasted_iota(jnp.int32, (tq, tq), 1)
                        <= lax.broadcasted_iota(jnp.int32, (tq, tq), 0))

    def key_block(j, masked):
        rows = pl.ds(pl.multiple_of(j * tq, tq), tq)
        hs = range(heads)
        s = [lax.dot_general(q_ref[0, :, h * MLA_QK_GROUP:(h + 1) * MLA_QK_GROUP],
                             k_ref[0, rows, h * MLA_QK_GROUP:(h + 1) * MLA_QK_GROUP],
                             (((1,), (1,)), ((), ())), preferred_element_type=F32) for h in hs]
        if masked:
            s = [jnp.where(on_or_below_diag, s_h, NEG_BIG) for s_h in s]
        m_prev = [m_ref[h] for h in hs]
        m_new = [jnp.maximum(m_prev[h], jnp.max(s[h], axis=-1, keepdims=True)) for h in hs]
        p = [jnp.exp2(s[h] - jnp.concatenate([m_new[h]] * (tq // LANES), axis=1)).astype(BF16) for h in hs]
        alpha = [jnp.exp2(m_prev[h] - m_new[h]) for h in hs]
        for h in hs:
            v_aug = jnp.concatenate([v_ref[0, rows, h * MLA_V_DIM:(h + 1) * MLA_V_DIM], ones_col], axis=1)
            pv = jnp.dot(p[h], v_aug, preferred_element_type=F32)
            acc_ref[h] = jnp.concatenate([alpha[h], alpha[h]], axis=1) * acc_ref[h] + pv
            m_ref[h] = m_new[h]

    def body(j, carry):
        key_block(j, False)
        return carry

    lax.fori_loop(0, qi, body, 0)
    key_block(qi, True)
    for h in range(heads):
        acc = acc_ref[h]
        o_ref[0, :, h * MLA_V_DIM:(h + 1) * MLA_V_DIM] = (
            acc[:, :MLA_V_DIM] / acc[:, MLA_V_DIM:MLA_V_DIM + 1]).astype(o_ref.dtype)


def _flash(q, k, v, *, tq=512, heads=8):
    bsz, seq, _ = q.shape
    kern = functools.partial(_flash_kernel, tq=tq, heads=heads)
    return pl.pallas_call(
        kern,
        grid=(bsz, MLA_HEADS // heads, seq // tq),
        in_specs=[
            pl.BlockSpec((1, tq, heads * MLA_QK_GROUP), lambda b, g, qi: (b, qi, g)),
            pl.BlockSpec((1, seq, heads * MLA_QK_GROUP), lambda b, g, qi: (b, 0, g)),
            pl.BlockSpec((1, seq, heads * MLA_V_DIM), lambda b, g, qi: (b, 0, g)),
        ],
        out_specs=pl.BlockSpec((1, tq, heads * MLA_V_DIM), lambda b, g, qi: (b, qi, g)),
        out_shape=jax.ShapeDtypeStruct((bsz, seq, MLA_HEADS * MLA_V_DIM), BF16),
        scratch_shapes=[
            pltpu.VMEM((heads, tq, LANES), F32),
            pltpu.VMEM((heads, tq, MLA_V_DIM + LANES), F32),
        ],
        compiler_params=_params(("arbitrary", "arbitrary", "arbitrary")),
        name="flash",
    )(q, k, v)


def _outproj_kernel(x_ref, hml_ref, hmla_ref, mod_ref, w_ref, g_ref, b_ref, o_ref, wb_ref, *,
                    sub, tiles_per_batch, alpha, row_chunk, norm_rows):
    i = pl.program_id(0)
    bidx = i // tiles_per_batch
    k_ml = hml_ref.shape[1]

    @pl.when(i == 0)
    def _():
        def body(r, carry):
            rows = pl.ds(pl.multiple_of(r * row_chunk, row_chunk), row_chunk)
            wb_ref[rows, :] = w_ref[rows, :].astype(BF16)
            return carry

        lax.fori_loop(0, w_ref.shape[0] // row_chunk, body, 0)

    gate = 1.0 + mod_ref[bidx, 3 * sub + 2:3 * sub + 3, :]
    g = g_ref[...]
    b_ln = b_ref[...]
    w_top = wb_ref[0:k_ml, :]
    w_bot = wb_ref[k_ml:, :]
    for r in range(x_ref.shape[0] // norm_rows):
        rows = slice(r * norm_rows, (r + 1) * norm_rows)
        y = (jnp.dot(hml_ref[rows, :], w_top, preferred_element_type=F32)
             + jnp.dot(hmla_ref[rows, :], w_bot, preferred_element_type=F32))
        o_ref[rows, :] = _layer_norm_rows(alpha * x_ref[rows, :] + gate * y, g, b_ln)


def _outproj(x2d, hml, hmla, mod, w_out, ln_g, ln_b, *, sub, seq, alpha, tm=512, row_chunk=128,
             norm_rows=128):
    n_tok, d = x2d.shape
    kern = functools.partial(_outproj_kernel, sub=sub, tiles_per_batch=seq // tm, alpha=alpha,
                             row_chunk=row_chunk, norm_rows=norm_rows)
    return pl.pallas_call(
        kern,
        grid=(n_tok // tm,),
        in_specs=[
            pl.BlockSpec((tm, d), lambda i: (i, 0)),
            pl.BlockSpec((tm, hml.shape[1]), lambda i: (i, 0)),
            pl.BlockSpec((tm, hmla.shape[1]), lambda i: (i, 0)),
            pl.BlockSpec(mod.shape, lambda i: (0, 0, 0)),
            pl.BlockSpec(w_out.shape, lambda i: (0, 0), pipeline_mode=pl.Buffered(1)),
            pl.BlockSpec((1, d), lambda i: (0, 0)),
            pl.BlockSpec((1, d), lambda i: (0, 0)),
        ],
        out_specs=pl.BlockSpec((tm, d), lambda i: (i, 0)),
        out_shape=jax.ShapeDtypeStruct((n_tok, d), F32),
        scratch_shapes=[pltpu.VMEM(w_out.shape, BF16)],
        compiler_params=_params(("arbitrary",)),
        name="outproj",
    )(x2d, hml, hmla, mod, w_out, ln_g.reshape(1, d), ln_b.reshape(1, d))


def kernel(x, c, w_ada, b_ada, ffn1_w1, ffn1_w3, ffn1_w2, ln1_g, ln1_b, w_in, conv_w, conv_b, b_igate, b_fgate, ml_norm_g, q_norm_g, w_uq, kv_norm_g, w_ukv, w_out, ln2_g, ln2_b, ffn2_w1, ffn2_w3, ffn2_w2, ln3_g, ln3_b):
    bsz, seq, d = x.shape
    depth = w_ada.shape[0]
    alpha = (2.0 * depth) ** 0.25
    h2d = x.reshape(bsz * seq, d)
    for l in range(depth):
        mod = _adaln(c, w_ada[l], b_ada[l]).reshape(bsz, N_SUBLAYERS * 3, d)
        h2d = _ffn(h2d, mod, ffn1_w1[l], ffn1_w3[l], ffn1_w2[l], ln1_g[l], ln1_b[l],
                   sub=0, seq=seq, alpha=alpha)
        assert w_in.shape[2] == D_IN
        pa, pb = _inproj(h2d, mod, w_in[l].T, sub=1, seq=seq)
        hml = _mlstm(pa.reshape(bsz, seq, PA_COLS), pb.reshape(bsz, seq, PB_COLS), conv_w[l], conv_b[l],
                     b_igate[l], b_fgate[l], ml_norm_g[l])
        q, k, v = _mla_prep(pb, q_norm_g[l], kv_norm_g[l], w_uq[l], w_ukv[l], seq=seq)
        hmla = _flash(q.reshape(bsz, seq, -1), k.reshape(bsz, seq, -1), v.reshape(bsz, seq, -1))
        h2d = _outproj(h2d, hml.reshape(bsz * seq, -1), hmla.reshape(bsz * seq, -1), mod,
                       w_out[l], ln2_g[l], ln2_b[l], sub=1, seq=seq, alpha=alpha)
        h2d = _ffn(h2d, mod, ffn2_w1[l], ffn2_w3[l], ffn2_w2[l], ln3_g[l], ln3_b[l],
                   sub=2, seq=seq, alpha=alpha)
    return h2d.reshape(bsz, seq, d)
```

```python
import functools
import math

import jax
import jax.numpy as jnp
from jax import lax
from jax.experimental import pallas as pl
from jax.experimental.pallas import tpu as pltpu

F32 = jnp.float32
BF16 = jnp.bfloat16

ML_HEADS = 4
ML_QK_DIM = 128
ML_V_DIM = 256
CONV_WIDTH = 4
MLA_HEADS = 8
MLA_NOPE_DIM = 128
MLA_ROPE_DIM = 64
MLA_V_DIM = 128
Q_LORA_RANK = 512
KV_LORA_RANK = 256
ROPE_THETA = 10000.0
FFN_RES_WEIGHT = 0.5
N_SUBLAYERS = 3
LN_EPS = 1e-5

LANES = 128
SUBLANES = 8
VMEM_LIMIT_BYTES = 56 * 1024 * 1024

ML_QK_COLS = 2 * ML_HEADS * ML_QK_DIM
ML_V_COLS = ML_HEADS * ML_V_DIM
PA_COLS = ML_QK_COLS + 2 * ML_V_COLS
PB_GATES = 0
PB_CQ = PB_GATES + 2 * ML_HEADS
PB_CKV = PB_CQ + Q_LORA_RANK
PB_KR = PB_CKV + KV_LORA_RANK
PB_VALID = PB_KR + MLA_ROPE_DIM
PROJ_BLOCK = 512
PB_COLS = -(-PB_VALID // PROJ_BLOCK) * PROJ_BLOCK
D_IN = PA_COLS + PB_VALID
MLA_QK_GROUP = 2 * LANES
NEG_BIG = -1e30


def _params(semantics):
    return pltpu.CompilerParams(dimension_semantics=semantics, vmem_limit_bytes=VMEM_LIMIT_BYTES)


def _silu(v):
    return v * jax.nn.sigmoid(v)


def _layer_norm_rows(z, g, b):
    mu = jnp.mean(z, axis=-1, keepdims=True)
    zc = z - mu
    var = jnp.mean(zc * zc, axis=-1, keepdims=True)
    return zc * lax.rsqrt(var + LN_EPS) * g + b


def _adaln_kernel(c_ref, w_ref, b_ref, o_ref):
    sc = _silu(c_ref[...]).astype(BF16)
    o_ref[...] = jnp.dot(sc, w_ref[...].astype(BF16), preferred_element_type=F32) + b_ref[...]


def _adaln(c, w, b, *, tn=1024):
    bsz, d = c.shape
    n = w.shape[1]
    return pl.pallas_call(
        _adaln_kernel,
        grid=(n // tn,),
        in_specs=[
            pl.BlockSpec((bsz, d), lambda j: (0, 0)),
            pl.BlockSpec((d, tn), lambda j: (0, j)),
            pl.BlockSpec((1, tn), lambda j: (0, j)),
        ],
        out_specs=pl.BlockSpec((bsz, tn), lambda j: (0, j)),
        out_shape=jax.ShapeDtypeStruct((bsz, n), F32),
        compiler_params=_params(("arbitrary",)),
        name="adaln",
    )(c, w, b.reshape(1, n))


def _ffn_kernel(x_ref, mod_ref, g_ref, b_ref, w1_hbm, w3_hbm, w2_hbm, o_ref,
                w1_buf, w3_buf, w2_buf, sem, *, sub, tiles_per_batch, alpha, finish_rows, tf):
    i = pl.program_id(0)
    n_tiles = pl.num_programs(0)
    bidx = i // tiles_per_batch
    tm = x_ref.shape[0]
    n_f = w1_hbm.shape[1] // tf

    def chunk_copies(c, slot):
        cols = pl.ds(pl.multiple_of(c * tf, tf), tf)
        return (pltpu.make_async_copy(w1_hbm.at[:, cols], w1_buf.at[slot], sem.at[0, slot]),
                pltpu.make_async_copy(w3_hbm.at[:, cols], w3_buf.at[slot], sem.at[1, slot]),
                pltpu.make_async_copy(w2_hbm.at[cols, :], w2_buf.at[slot], sem.at[2, slot]))

    def start_chunk(c, slot):
        for cp in chunk_copies(c, slot):
            cp.start()

    def wait_chunk(c, slot):
        for cp in chunk_copies(c, slot):
            cp.wait()

    @pl.when(i == 0)
    def _():
        start_chunk(0, 0)

    shift = mod_ref[bidx, 3 * sub:3 * sub + 1, :]
    scale1 = 1.0 + mod_ref[bidx, 3 * sub + 1:3 * sub + 2, :]

    def partial_out(x_rows, w1, w3, w2):
        u = (x_rows * scale1 + shift).astype(BF16)
        a = jnp.dot(u, w1, preferred_element_type=F32)
        b = jnp.dot(u, w3, preferred_element_type=F32)
        h = (_silu(a) * b).astype(BF16)
        return jnp.dot(h, w2, preferred_element_type=F32)

    def chunk_weights(slot):
        return (w1_buf[slot].astype(BF16), w3_buf[slot].astype(BF16), w2_buf[slot].astype(BF16))

    start_chunk(1, 1)
    wait_chunk(0, 0)
    o_ref[...] = partial_out(x_ref[...], *chunk_weights(0))

    def chunk_pair(k, carry):
        c1 = 2 * k + 1
        start_chunk(c1 + 1, 0)
        wait_chunk(c1, 1)
        o_ref[...] += partial_out(x_ref[...], *chunk_weights(1))
        start_chunk(c1 + 2, 1)
        wait_chunk(c1 + 1, 0)
        o_ref[...] += partial_out(x_ref[...], *chunk_weights(0))
        return carry

    lax.fori_loop(0, (n_f - 2) // 2, chunk_pair, 0)

    @pl.when(i + 1 < n_tiles)
    def _():
        start_chunk(0, 0)

    wait_chunk(n_f - 1, 1)
    gate = FFN_RES_WEIGHT * (1.0 + mod_ref[bidx, 3 * sub + 2:3 * sub + 3, :])
    g = g_ref[...]
    b_ln = b_ref[...]
    last_w = chunk_weights(1)
    for r in range(tm // finish_rows):
        rows = slice(r * finish_rows, (r + 1) * finish_rows)
        x_rows = x_ref[rows, :]
        y = o_ref[rows, :] + partial_out(x_rows, *last_w)
        o_ref[rows, :] = _layer_norm_rows(alpha * x_rows + gate * y, g, b_ln)


def _ffn(x2d, mod, w1, w3, w2, ln_g, ln_b, *, sub, seq, alpha, tm=1024, tf=256, finish_rows=256):
    n_tok, d = x2d.shape
    f = w1.shape[1]
    tm = min(tm, seq)
    assert f % (2 * tf) == 0 and f // tf >= 4 and tm % finish_rows == 0
    kern = functools.partial(_ffn_kernel, sub=sub, tiles_per_batch=seq // tm, alpha=alpha,
                             finish_rows=finish_rows, tf=tf)
    hbm = pl.BlockSpec(memory_space=pl.ANY)
    return pl.pallas_call(
        kern,
        grid=(n_tok // tm,),
        in_specs=[
            pl.BlockSpec((tm, d), lambda i: (i, 0)),
            pl.BlockSpec(mod.shape, lambda i: (0, 0, 0)),
            pl.BlockSpec((1, d), lambda i: (0, 0)),
            pl.BlockSpec((1, d), lambda i: (0, 0)),
            hbm, hbm, hbm,
        ],
        out_specs=pl.BlockSpec((tm, d), lambda i: (i, 0)),
        out_shape=jax.ShapeDtypeStruct((n_tok, d), F32),
        scratch_shapes=[
            pltpu.VMEM((2, d, tf), w1.dtype),
            pltpu.VMEM((2, d, tf), w3.dtype),
            pltpu.VMEM((2, tf, d), w2.dtype),
            pltpu.SemaphoreType.DMA((3, 2)),
        ],
        compiler_params=_params(("arbitrary",)),
        name=f"ffn{sub}",
    )(x2d, mod, ln_g.reshape(1, d), ln_b.reshape(1, d), w1, w3, w2)


def _inproj_kernel(x_hbm, mod_ref, w_ref, oa_ref, ob_ref, wbf_ref, x_buf, x_sem, *,
                   sub, tiles_per_batch, n_a, n_valid):
    i = pl.program_id(0)
    j = pl.program_id(1)
    n_tiles = pl.num_programs(0)
    bidx = i // tiles_per_batch
    tn = w_ref.shape[0]
    tm = x_buf.shape[1]
    slot = i % 2

    def x_copy(tile, to_slot):
        rows = pl.ds(pl.multiple_of(tile * tm, tm), tm)
        return pltpu.make_async_copy(x_hbm.at[rows, :], x_buf.at[to_slot], x_sem.at[to_slot])

    @pl.when(j == 0)
    def _():
        @pl.when(i == 0)
        def _():
            x_copy(0, 0).start()

        x_copy(i, slot).wait()

        @pl.when(i + 1 < n_tiles)
        def _():
            x_copy(i + 1, 1 - slot).start()

    @pl.when(i == 0)
    def _():
        out_col = j * tn + lax.broadcasted_iota(jnp.int32, w_ref.shape, 0)
        wbf_ref[j] = jnp.where(out_col < n_valid, w_ref[...], 0.0).astype(BF16)

    def project():
        shift = mod_ref[bidx, 3 * sub:3 * sub + 1, :]
        scale1 = 1.0 + mod_ref[bidx, 3 * sub + 1:3 * sub + 2, :]
        u = (x_buf[slot] * scale1 + shift).astype(BF16)
        return lax.dot_general(u, wbf_ref[j], (((1,), (1,)), ((), ())), preferred_element_type=F32)

    @pl.when(j < n_a)
    def _():
        oa_ref[...] = project().astype(BF16)

    @pl.when(j >= n_a)
    def _():
        ob_ref[...] = project()


def _inproj(x2d, mod, w_in_t, *, sub, seq, tm=1024, tn=PROJ_BLOCK):
    n_tok, d = x2d.shape
    n_a = PA_COLS // tn
    n_b = PB_COLS // tn
    n_blocks = n_a + n_b
    kern = functools.partial(_inproj_kernel, sub=sub, tiles_per_batch=seq // tm, n_a=n_a,
                             n_valid=w_in_t.shape[0])
    return pl.pallas_call(
        kern,
        grid=(n_tok // tm, n_blocks),
        in_specs=[
            pl.BlockSpec(memory_space=pl.ANY),
            pl.BlockSpec(mod.shape, lambda i, j: (0, 0, 0)),
            pl.BlockSpec((tn, d), lambda i, j: (jnp.where(i == 0, j, n_blocks - 1), 0)),
        ],
        out_specs=[
            pl.BlockSpec((tm, tn), lambda i, j: (i, jnp.minimum(j, n_a - 1))),
            pl.BlockSpec((tm, tn), lambda i, j: (i, jnp.maximum(j - n_a, 0))),
        ],
        out_shape=[
            jax.ShapeDtypeStruct((n_tok, PA_COLS), BF16),
            jax.ShapeDtypeStruct((n_tok, PB_COLS), F32),
        ],
        scratch_shapes=[
            pltpu.VMEM((n_blocks, tn, d), BF16),
            pltpu.VMEM((2, tm, d), x2d.dtype),
            pltpu.SemaphoreType.DMA((2,)),
        ],
        compiler_params=_params(("arbitrary", "arbitrary")),
        name="inproj",
    )(x2d, mod, w_in_t)


def _lane_scan(v, op, chunk):
    lane = lax.broadcasted_iota(jnp.int32, v.shape, 1) % chunk
    ident = 0.0 if op is jnp.add else NEG_BIG
    s = 1
    while s < chunk:
        shifted = pltpu.roll(v, s, axis=1)
        v = op(v, jnp.where(lane >= s, shifted, ident))
        s *= 2
    return v


def _mlstm_gate_scans(bi, gates_ref, gb_ref, scan_ref, chunk):
    seq = gates_ref.shape[1]
    for c in range(seq // LANES):
        cols = slice(c * LANES, (c + 1) * LANES)
        scan_ref[bi, 0, :, cols] = gates_ref[bi, cols, :].T[0:SUBLANES, :]
    z = scan_ref[bi, 0] + gb_ref[:, 0:1]
    logi = pltpu.roll(z, ML_HEADS, axis=0)
    logf = jnp.minimum(z, 0.0) - jnp.log1p(jnp.exp(-jnp.abs(z)))
    bcum = _lane_scan(logf, jnp.add, chunk)
    r = logi - bcum
    scan_ref[bi, 0] = bcum
    scan_ref[bi, 1] = r
    scan_ref[bi, 2] = _lane_scan(r, jnp.maximum, chunk)


def _mlstm_kernel(qk_ref, v_ref, og_ref, gates_ref, cw_ref, cb_ref, gb_ref, ng_ref, sel_ref, cbias_ref,
                  o_ref, c_ref, m_ref, prev_ref, scan_ref, *, chunk):
    t = pl.program_id(1)

    @pl.when(t == 0)
    def _():
        c_ref[...] = jnp.zeros_like(c_ref)
        m_ref[...] = jnp.zeros_like(m_ref)
        prev_ref[...] = jnp.zeros_like(prev_ref)
        for bi in range(qk_ref.shape[0]):
            _mlstm_gate_scans(bi, gates_ref, gb_ref, scan_ref, chunk)

    _mlstm_step(t, qk_ref, v_ref, og_ref, cw_ref, cb_ref, ng_ref, sel_ref, cbias_ref, o_ref,
                c_ref, m_ref, prev_ref, scan_ref, chunk)


def _mlstm_step(t, qk_ref, v_ref, og_ref, cw_ref, cb_ref, ng_ref, sel_ref, cbias_ref, o_ref,
                c_ref, m_ref, prev_ref, scan_ref, chunk):
    hq = ML_HEADS * ML_QK_DIM
    n_rows = qk_ref.shape[0]
    chains = [(bi, h) for bi in range(n_rows) for h in range(ML_HEADS)]
    head_cols = lambda h: slice(h * ML_V_DIM, (h + 1) * ML_V_DIM)
    causal_bias = cbias_ref[...]
    ones_col = (lax.broadcasted_iota(jnp.int32, (chunk, LANES), 1) == 0).astype(BF16)
    lanes = pl.ds(pl.multiple_of(t * chunk, chunk), chunk)

    qk = []
    for bi in range(n_rows):
        qk_now = qk_ref[bi]
        taps = jnp.dot(sel_ref[...], jnp.concatenate([prev_ref[bi], qk_now], axis=0),
                       preferred_element_type=F32)
        prev_ref[bi] = qk_now
        conv = cb_ref[...]
        for tap in range(CONV_WIDTH):
            conv = conv + cw_ref[tap:tap + 1, :] * taps[tap * chunk:(tap + 1) * chunk, :]
        qk.append(_silu(conv))

    r, cols, s_old = [], [], []
    for bi in range(n_rows):
        bcum = scan_ref[bi, 0, :, lanes]
        r_b = scan_ref[bi, 1, :, lanes]
        cm = scan_ref[bi, 2, :, lanes]
        m_in = m_ref[bi, :, 0:1]
        m_all = jnp.maximum(m_in, cm)
        m_last = m_all[:, chunk - 1:chunk]
        m_ref[bi] = jnp.broadcast_to(bcum[:, chunk - 1:chunk] + m_last, (SUBLANES, LANES))
        s_inter = jnp.exp(m_in - m_all)
        e_neg_m = jnp.exp(-(bcum + m_all))
        w_last = jnp.exp(r_b - m_last)
        stack = jnp.concatenate(
            [m_all, s_inter, e_neg_m, w_last, jnp.zeros((LANES - 4 * SUBLANES, chunk), F32)], axis=0)
        r.append(r_b)
        cols.append(stack.T)
        s_old.append(jnp.exp(m_in - m_last))

    def col(bi, h, which):
        lane = which * SUBLANES + ML_HEADS + h
        return cols[bi][:, lane:lane + 1]

    q_h = {c: qk[c[0]][:, c[1] * ML_QK_DIM:(c[1] + 1) * ML_QK_DIM].astype(BF16) for c in chains}
    k_f = {c: qk[c[0]][:, hq + c[1] * ML_QK_DIM:hq + (c[1] + 1) * ML_QK_DIM] * (ML_QK_DIM ** -0.5)
           for c in chains}
    v_aug = {c: jnp.concatenate([v_ref[c[0], :, head_cols(c[1])], ones_col], axis=1) for c in chains}
    scores = {c: lax.dot_general(q_h[c], k_f[c].astype(BF16), (((1,), (1,)), ((), ())),
                                 preferred_element_type=F32) for c in chains}
    p = {}
    for bi, h in chains:
        decay = jnp.exp(r[bi][ML_HEADS + h:ML_HEADS + h + 1, :] - col(bi, h, 0) + causal_bias)
        p[bi, h] = (decay * scores[bi, h]).astype(BF16)
    c_old = {c: c_ref[c[0], c[1]] for c in chains}
    tot = {c: (jnp.dot(p[c], v_aug[c], preferred_element_type=F32)
               + col(c[0], c[1], 1) * jnp.dot(q_h[c], c_old[c].astype(BF16), preferred_element_type=F32))
           for c in chains}
    for bi, h in chains:
        k_w = (k_f[bi, h] * col(bi, h, 3)).astype(BF16)
        c_ref[bi, h] = s_old[bi][ML_HEADS + h:ML_HEADS + h + 1, :] * c_old[bi, h] + lax.dot_general(
            k_w, v_aug[bi, h], (((0,), (0,)), ((), ())), preferred_element_type=F32)
    for bi, h in chains:
        num = tot[bi, h][:, :ML_V_DIM]
        den = tot[bi, h][:, ML_V_DIM:ML_V_DIM + 1]
        hid = num / jnp.maximum(jnp.abs(den), col(bi, h, 2))
        mu = jnp.mean(hid, axis=-1, keepdims=True)
        hc = hid - mu
        var = jnp.mean(hc * hc, axis=-1, keepdims=True)
        hn = hc * lax.rsqrt(var + LN_EPS) * ng_ref[:, head_cols(h)]
        gate = jax.nn.sigmoid(og_ref[bi, :, head_cols(h)].astype(F32))
        o_ref[bi, :, head_cols(h)] = (hn * gate).astype(o_ref.dtype)


def _mlstm(pa, pb, conv_w, conv_b, b_igate, b_fgate, norm_g, *, chunk=128, rows_per_step=2):
    bsz, seq, _ = pa.shape
    rb = rows_per_step
    gate_bias = jnp.broadcast_to(jnp.concatenate([b_igate, b_fgate])[:, None], (SUBLANES, LANES))
    t_idx = jnp.arange(CONV_WIDTH * chunk) % chunk
    tap_idx = jnp.arange(CONV_WIDTH * chunk) // chunk
    shift_sel = (jnp.arange(2 * chunk)[None, :]
                 == (chunk + t_idx - (CONV_WIDTH - 1) + tap_idx)[:, None]).astype(BF16)
    causal_bias = jnp.where(jnp.arange(chunk)[None, :] <= jnp.arange(chunk)[:, None], 0.0, NEG_BIG).astype(F32)
    kern = functools.partial(_mlstm_kernel, chunk=chunk)
    return pl.pallas_call(
        kern,
        grid=(bsz // rb, seq // chunk),
        in_specs=[
            pl.BlockSpec((rb, chunk, ML_QK_COLS), lambda b, t: (b, t, 0)),
            pl.BlockSpec((rb, chunk, ML_V_COLS), lambda b, t: (b, t, ML_QK_COLS // ML_V_COLS)),
            pl.BlockSpec((rb, chunk, ML_V_COLS), lambda b, t: (b, t, ML_QK_COLS // ML_V_COLS + 1)),
            pl.BlockSpec((rb, seq, LANES), lambda b, t: (b, 0, PB_GATES // LANES)),
            pl.BlockSpec((CONV_WIDTH, ML_QK_COLS), lambda b, t: (0, 0)),
            pl.BlockSpec((1, ML_QK_COLS), lambda b, t: (0, 0)),
            pl.BlockSpec((SUBLANES, LANES), lambda b, t: (0, 0)),
            pl.BlockSpec((1, ML_V_COLS), lambda b, t: (0, 0)),
            pl.BlockSpec(shift_sel.shape, lambda b, t: (0, 0)),
            pl.BlockSpec(causal_bias.shape, lambda b, t: (0, 0)),
        ],
        out_specs=pl.BlockSpec((rb, chunk, ML_V_COLS), lambda b, t: (b, t, 0)),
        out_shape=jax.ShapeDtypeStruct((bsz, seq, ML_V_COLS), BF16),
        scratch_shapes=[
            pltpu.VMEM((rb, ML_HEADS, ML_QK_DIM, ML_V_DIM + LANES), F32),
            pltpu.VMEM((rb, SUBLANES, LANES), F32),
            pltpu.VMEM((rb, chunk, ML_QK_COLS), pa.dtype),
            pltpu.VMEM((rb, 3, SUBLANES, seq), F32),
        ],
        compiler_params=_params(("arbitrary", "arbitrary")),
        name="mlstm",
    )(pa, pa, pa, pb, conv_w, conv_b.reshape(1, -1), gate_bias, norm_g.reshape(1, -1),
      shift_sel, causal_bias)


def _rope_lanes(x, cos, sin):
    return x * cos + pltpu.roll(x, LANES // 2, axis=1) * sin


def _rms_rows(x, g):
    return x * lax.rsqrt(jnp.mean(x * x, axis=-1, keepdims=True) + LN_EPS) * g


def _mla_prep_kernel(xa_ref, xb_ref, qg_ref, kvg_ref, wq_ref, wkv_ref, cos_ref, sin_ref,
                     q_ref, k_ref, v_ref, *, q_scale, q_shift, kv_cols, kv_shift, kr_group, kr_shift):
    cos = cos_ref[...]
    sin = sin_ref[...]
    half = MLA_ROPE_DIM // 2
    xa = xa_ref[...]
    xb = xb_ref[:, :kv_cols]
    c_q = pltpu.roll(xa, xa.shape[1] - q_shift, axis=1)[:, :Q_LORA_RANK]
    c_kv = pltpu.roll(xb, xb.shape[1] - kv_shift, axis=1)[:, :KV_LORA_RANK]
    q_lat = _rms_rows(c_q, qg_ref[...]).astype(BF16)
    kv_lat = _rms_rows(c_kv, kvg_ref[...]).astype(BF16)
    k_r = pltpu.roll(xb_ref[:, kr_group * LANES:(kr_group + 1) * LANES], LANES - kr_shift, axis=1)
    lane = lax.broadcasted_iota(jnp.int32, k_r.shape, 1)
    k_r = (jnp.where(lane < half, k_r, 0.0)
           + jnp.where((lane >= LANES // 2) & (lane < LANES // 2 + half), pltpu.roll(k_r, half, axis=1), 0.0))
    k_rope = _rope_lanes(k_r, cos, sin).astype(k_ref.dtype)
    group = lambda h: slice(h * MLA_QK_GROUP, (h + 1) * MLA_QK_GROUP)
    q_all = [jnp.dot(q_lat, wq_ref[:, group(h)], preferred_element_type=F32) for h in range(MLA_HEADS)]
    kv_all = [jnp.dot(kv_lat, wkv_ref[:, group(h)], preferred_element_type=F32) for h in range(MLA_HEADS)]
    for h in range(MLA_HEADS):
        lo = h * MLA_QK_GROUP
        q_h = q_all[h] * q_scale
        q_ref[:, lo:lo + LANES] = q_h[:, :LANES].astype(q_ref.dtype)
        q_ref[:, lo + LANES:lo + 2 * LANES] = _rope_lanes(q_h[:, LANES:], cos, sin).astype(q_ref.dtype)
        k_ref[:, lo:lo + LANES] = kv_all[h][:, :LANES].astype(k_ref.dtype)
        k_ref[:, lo + LANES:lo + 2 * LANES] = k_rope
        v_ref[:, h * MLA_V_DIM:(h + 1) * MLA_V_DIM] = kv_all[h][:, LANES:].astype(v_ref.dtype)


def _rope_tables(seq):
    half = MLA_ROPE_DIM // 2
    inv = ROPE_THETA ** (-jnp.arange(half, dtype=F32) / half)
    ang = jnp.arange(seq, dtype=F32)[:, None] * inv[None, :]
    cos, sin = jnp.cos(ang), jnp.sin(ang)
    zero = jnp.zeros_like(cos)
    return (jnp.concatenate([cos, zero, cos, zero], axis=1),
            jnp.concatenate([-sin, zero, sin, zero], axis=1))


def _mla_prep(pb2d, q_norm_g, kv_norm_g, w_uq, w_ukv, *, seq, tm=512):
    n_tok = pb2d.shape[0]
    dq = MLA_NOPE_DIM + MLA_ROPE_DIM
    xb_width = PROJ_BLOCK
    xb_start = (PB_CKV // xb_width) * xb_width
    assert PB_VALID <= xb_start + xb_width and PB_COLS % xb_width == 0
    xa_cols = -(-PB_CKV // LANES) * LANES
    ckv_lo = PB_CKV - xb_start
    kv_cols = -(-(ckv_lo + KV_LORA_RANK) // LANES) * LANES
    kr_lo = PB_KR - xb_start
    assert kr_lo % LANES + MLA_ROPE_DIM <= LANES
    half = MLA_ROPE_DIM // 2
    wq = w_uq.reshape(Q_LORA_RANK, MLA_HEADS, dq)
    gap = jnp.zeros(wq.shape[:2] + (LANES // 2 - half,), wq.dtype)
    wq = jnp.concatenate([wq[..., :MLA_NOPE_DIM], wq[..., MLA_NOPE_DIM:MLA_NOPE_DIM + half], gap,
                          wq[..., MLA_NOPE_DIM + half:], gap], axis=-1)
    wq = wq.reshape(Q_LORA_RANK, MLA_HEADS * MLA_QK_GROUP).astype(BF16)
    wkv = w_ukv.astype(BF16)
    cos_t, sin_t = _rope_tables(seq)
    tiles_per_seq = seq // tm
    pos = lambda i: (i % tiles_per_seq, 0)
    kern = functools.partial(_mla_prep_kernel, q_scale=math.log2(math.e) / math.sqrt(dq),
                             q_shift=PB_CQ, kv_cols=kv_cols, kv_shift=ckv_lo,
                             kr_group=kr_lo // LANES, kr_shift=kr_lo % LANES)
    qk_cols = MLA_HEADS * MLA_QK_GROUP
    const = lambda i: (0, 0)
    return pl.pallas_call(
        kern,
        grid=(n_tok // tm,),
        in_specs=[
            pl.BlockSpec((tm, xa_cols), lambda i: (i, 0)),
            pl.BlockSpec((tm, xb_width), lambda i: (i, xb_start // xb_width)),
            pl.BlockSpec((1, Q_LORA_RANK), const),
            pl.BlockSpec((1, KV_LORA_RANK), const),
            pl.BlockSpec((Q_LORA_RANK, qk_cols), const),
            pl.BlockSpec((KV_LORA_RANK, qk_cols), const),
            pl.BlockSpec((tm, LANES), pos),
            pl.BlockSpec((tm, LANES), pos),
        ],
        out_specs=[
            pl.BlockSpec((tm, qk_cols), lambda i: (i, 0)),
            pl.BlockSpec((tm, qk_cols), lambda i: (i, 0)),
            pl.BlockSpec((tm, MLA_HEADS * MLA_V_DIM), lambda i: (i, 0)),
        ],
        out_shape=[
            jax.ShapeDtypeStruct((n_tok, qk_cols), BF16),
            jax.ShapeDtypeStruct((n_tok, qk_cols), BF16),
            jax.ShapeDtypeStruct((n_tok, MLA_HEADS * MLA_V_DIM), BF16),
        ],
        compiler_params=_params(("arbitrary",)),
        name="mla_prep",
    )(pb2d, pb2d, q_norm_g.reshape(1, -1), kv_norm_g.reshape(1, -1), wq, wkv, cos_t, sin_t)


def _flash_kernel(q_ref, k_ref, v_ref, o_ref, m_ref, acc_ref, *, tq, heads):
    qi = pl.program_id(2)
    m_ref[...] = jnp.full_like(m_ref, NEG_BIG)
    acc_ref[...] = jnp.zeros_like(acc_ref)
    ones_col = (lax.broadcasted_iota(jnp.int32, (tq, LANES), 1) == 0).astype(BF16)
    on_or_below_diag = (lax.broadcasted_iota(jnp.int32, (tq, tq), 1)
                        <= lax.broadcasted_iota(jnp.int32, (tq, tq), 0))

    def key_block(j, masked):
        rows = pl.ds(pl.multiple_of(j * tq, tq), tq)
        hs = range(heads)
        s = [lax.dot_general(q_ref[0, :, h * MLA_QK_GROUP:(h + 1) * MLA_QK_GROUP],
                             k_ref[0, rows, h * MLA_QK_GROUP:(h + 1) * MLA_QK_GROUP],
                             (((1,), (1,)), ((), ())), preferred_element_type=F32) for h in hs]
        if masked:
            s = [jnp.where(on_or_below_diag, s_h, NEG_BIG) for s_h in s]
        m_prev = [m_ref[h] for h in hs]
        m_new = [jnp.maximum(m_prev[h], jnp.max(s[h], axis=-1, keepdims=True)) for h in hs]
        p = [jnp.exp2(s[h] - jnp.concatenate([m_new[h]] * (tq // LANES), axis=1)).astype(BF16) for h in hs]
        alpha = [jnp.exp2(m_prev[h] - m_new[h]) for h in hs]
        for h in hs:
            v_aug = jnp.concatenate([v_ref[0, rows, h * MLA_V_DIM:(h + 1) * MLA_V_DIM], ones_col], axis=1)
            pv = jnp.dot(p[h], v_aug, preferred_element_type=F32)
            acc_ref[h] = jnp.concatenate([alpha[h], alpha[h]], axis=1) * acc_ref[h] + pv
            m_ref[h] = m_new[h]

    def body(j, carry):
        key_block(j, False)
        return carry

    lax.fori_loop(0, qi, body, 0)
    key_block(qi, True)
    for h in range(heads):
        acc = acc_ref[h]
        o_ref[0, :, h * MLA_V_DIM:(h + 1) * MLA_V_DIM] = (
            acc[:, :MLA_V_DIM] / acc[:, MLA_V_DIM:MLA_V_DIM + 1]).astype(o_ref.dtype)


def _flash(q, k, v, *, tq=512, heads=8):
    bsz, seq, _ = q.shape
    kern = functools.partial(_flash_kernel, tq=tq, heads=heads)
    return pl.pallas_call(
        kern,
        grid=(bsz, MLA_HEADS // heads, seq // tq),
        in_specs=[
            pl.BlockSpec((1, tq, heads * MLA_QK_GROUP), lambda b, g, qi: (b, qi, g)),
            pl.BlockSpec((1, seq, heads * MLA_QK_GROUP), lambda b, g, qi: (b, 0, g)),
            pl.BlockSpec((1, seq, heads * MLA_V_DIM), lambda b, g, qi: (b, 0, g)),
        ],
        out_specs=pl.BlockSpec((1, tq, heads * MLA_V_DIM), lambda b, g, qi: (b, qi, g)),
        out_shape=jax.ShapeDtypeStruct((bsz, seq, MLA_HEADS * MLA_V_DIM), BF16),
        scratch_shapes=[
            pltpu.VMEM((heads, tq, LANES), F32),
            pltpu.VMEM((heads, tq, MLA_V_DIM + LANES), F32),
        ],
        compiler_params=_params(("arbitrary", "arbitrary", "arbitrary")),
        name="flash",
    )(q, k, v)


def _outproj_kernel(x_ref, hml_ref, hmla_ref, mod_ref, w_ref, g_ref, b_ref, o_ref, wb_ref, *,
                    sub, tiles_per_batch, alpha, row_chunk, norm_rows):
    i = pl.program_id(0)
    bidx = i // tiles_per_batch
    k_ml = hml_ref.shape[1]

    @pl.when(i == 0)
    def _():
        def body(r, carry):
            rows = pl.ds(pl.multiple_of(r * row_chunk, row_chunk), row_chunk)
            wb_ref[rows, :] = w_ref[rows, :].astype(BF16)
            return carry

        lax.fori_loop(0, w_ref.shape[0] // row_chunk, body, 0)

    gate = 1.0 + mod_ref[bidx, 3 * sub + 2:3 * sub + 3, :]
    g = g_ref[...]
    b_ln = b_ref[...]
    w_top = wb_ref[0:k_ml, :]
    w_bot = wb_ref[k_ml:, :]
    for r in range(x_ref.shape[0] // norm_rows):
        rows = slice(r * norm_rows, (r + 1) * norm_rows)
        y = (jnp.dot(hml_ref[rows, :], w_top, preferred_element_type=F32)
             + jnp.dot(hmla_ref[rows, :], w_bot, preferred_element_type=F32))
        o_ref[rows, :] = _layer_norm_rows(alpha * x_ref[rows, :] + gate * y, g, b_ln)


def _outproj(x2d, hml, hmla, mod, w_out, ln_g, ln_b, *, sub, seq, alpha, tm=512, row_chunk=128,
             norm_rows=128):
    n_tok, d = x2d.shape
    kern = functools.partial(_outproj_kernel, sub=sub, tiles_per_batch=seq // tm, alpha=alpha,
                             row_chunk=row_chunk, norm_rows=norm_rows)
    return pl.pallas_call(
        kern,
        grid=(n_tok // tm,),
        in_specs=[
            pl.BlockSpec((tm, d), lambda i: (i, 0)),
            pl.BlockSpec((tm, hml.shape[1]), lambda i: (i, 0)),
            pl.BlockSpec((tm, hmla.shape[1]), lambda i: (i, 0)),
            pl.BlockSpec(mod.shape, lambda i: (0, 0, 0)),
            pl.BlockSpec(w_out.shape, lambda i: (0, 0), pipeline_mode=pl.Buffered(1)),
            pl.BlockSpec((1, d), lambda i: (0, 0)),
            pl.BlockSpec((1, d), lambda i: (0, 0)),
        ],
        out_specs=pl.BlockSpec((tm, d), lambda i: (i, 0)),
        out_shape=jax.ShapeDtypeStruct((n_tok, d), F32),
        scratch_shapes=[pltpu.VMEM(w_out.shape, BF16)],
        compiler_params=_params(("arbitrary",)),
        name="outproj",
    )(x2d, hml, hmla, mod, w_out, ln_g.reshape(1, d), ln_b.reshape(1, d))


def kernel(x, c, w_ada, b_ada, ffn1_w1, ffn1_w3, ffn1_w2, ln1_g, ln1_b, w_in, conv_w, conv_b, b_igate, b_fgate, ml_norm_g, q_norm_g, w_uq, kv_norm_g, w_ukv, w_out, ln2_g, ln2_b, ffn2_w1, ffn2_w3, ffn2_w2, ln3_g, ln3_b):
    bsz, seq, d = x.shape
    depth = w_ada.shape[0]
    alpha = (2.0 * depth) ** 0.25
    h2d = x.reshape(bsz * seq, d)
    for l in range(depth):
        mod = _adaln(c, w_ada[l], b_ada[l]).reshape(bsz, N_SUBLAYERS * 3, d)
        h2d = _ffn(h2d, mod, ffn1_w1[l], ffn1_w3[l], ffn1_w2[l], ln1_g[l], ln1_b[l],
                   sub=0, seq=seq, alpha=alpha)
        assert w_in.shape[2] == D_IN
        pa, pb = _inproj(h2d, mod, w_in[l].T, sub=1, seq=seq)
        hml = _mlstm(pa.reshape(bsz, seq, PA_COLS), pb.reshape(bsz, seq, PB_COLS), conv_w[l], conv_b[l],
                     b_igate[l], b_fgate[l], ml_norm_g[l])
        q, k, v = _mla_prep(pb, q_norm_g[l], kv_norm_g[l], w_uq[l], w_ukv[l], seq=seq)
        hmla = _flash(q.reshape(bsz, seq, -1), k.reshape(bsz, seq, -1), v.reshape(bsz, seq, -1))
        h2d = _outproj(h2d, hml.reshape(bsz * seq, -1), hmla.reshape(bsz * seq, -1), mod,
                       w_out[l], ln2_g[l], ln2_b[l], sub=1, seq=seq, alpha=alpha)
        h2d = _ffn(h2d, mod, ffn2_w1[l], ffn2_w3[l], ffn2_w2[l], ln3_g[l], ln3_b[l],
                   sub=2, seq=seq, alpha=alpha)
    return h2d.reshape(bsz, seq, d)
```

```python
import functools
import math

import jax
import jax.numpy as jnp
from jax import lax
from jax.experimental import pallas as pl
from jax.experimental.pallas import tpu as pltpu

F32 = jnp.float32
BF16 = jnp.bfloat16

ML_HEADS = 4
ML_QK_DIM = 128
ML_V_DIM = 256
CONV_WIDTH = 4
MLA_HEADS = 8
MLA_NOPE_DIM = 128
MLA_ROPE_DIM = 64
MLA_V_DIM = 128
Q_LORA_RANK = 512
KV_LORA_RANK = 256
ROPE_THETA = 10000.0
FFN_RES_WEIGHT = 0.5
N_SUBLAYERS = 3
LN_EPS = 1e-5

LANES = 128
SUBLANES = 8
VMEM_LIMIT_BYTES = 56 * 1024 * 1024

ML_QK_COLS = 2 * ML_HEADS * ML_QK_DIM
ML_V_COLS = ML_HEADS * ML_V_DIM
PA_COLS = ML_QK_COLS + 2 * ML_V_COLS
PB_GATES = 0
PB_CQ = PB_GATES + 2 * ML_HEADS
PB_CKV = PB_CQ + Q_LORA_RANK
PB_KR = PB_CKV + KV_LORA_RANK
PB_VALID = PB_KR + MLA_ROPE_DIM
PB_COLS = -(-PB_VALID // 512) * 512
D_IN = PA_COLS + PB_VALID
MLA_QK_GROUP = 2 * LANES
NEG_BIG = -1e30


def _params(semantics):
    return pltpu.CompilerParams(dimension_semantics=semantics, vmem_limit_bytes=VMEM_LIMIT_BYTES)


def _silu(v):
    return v * jax.nn.sigmoid(v)


def _layer_norm_rows(z, g, b):
    mu = jnp.mean(z, axis=-1, keepdims=True)
    zc = z - mu
    var = jnp.mean(zc * zc, axis=-1, keepdims=True)
    return zc * lax.rsqrt(var + LN_EPS) * g + b


def _adaln_kernel(c_ref, b_ref, w_hbm, o_ref, w_buf, sem, *, tn):
    depth = w_buf.shape[0]
    n_blocks = w_hbm.shape[1] // tn
    sc = _silu(c_ref[...]).astype(BF16)

    def block_copy(j, slot):
        cols = pl.ds(pl.multiple_of(j * tn, tn), tn)
        return pltpu.make_async_copy(w_hbm.at[:, cols], w_buf.at[slot], sem.at[slot])

    for slot in range(depth):
        block_copy(slot, slot).start()

    def ring_turn(g, carry):
        for slot in range(depth):
            j = g * depth + slot
            block_copy(j, slot).wait()
            cols = pl.ds(pl.multiple_of(j * tn, tn), tn)
            o_ref[:, cols] = (jnp.dot(sc, w_buf[slot].astype(BF16), preferred_element_type=F32)
                              + b_ref[:, cols])

            @pl.when(j + depth < n_blocks)
            def _():
                block_copy(j + depth, slot).start()
        return carry

    lax.fori_loop(0, n_blocks // depth, ring_turn, 0)


def _adaln(c, w, b, *, tn=1024, depth=3):
    bsz, d = c.shape
    n = w.shape[1]
    assert n % (tn * depth) == 0
    rows = -(-bsz // SUBLANES) * SUBLANES
    c_pad = jnp.pad(c, ((0, rows - bsz), (0, 0)))
    vmem = pl.BlockSpec(memory_space=pltpu.VMEM)
    out = pl.pallas_call(
        functools.partial(_adaln_kernel, tn=tn),
        in_specs=[vmem, vmem, pl.BlockSpec(memory_space=pl.ANY)],
        out_specs=vmem,
        out_shape=jax.ShapeDtypeStruct((rows, n), F32),
        scratch_shapes=[pltpu.VMEM((depth, d, tn), w.dtype), pltpu.SemaphoreType.DMA((depth,))],
        compiler_params=pltpu.CompilerParams(vmem_limit_bytes=VMEM_LIMIT_BYTES),
        name="adaln",
    )(c_pad, b.reshape(1, n), w)
    return out[:bsz]


def _ffn_kernel(x_ref, mod_ref, g_ref, b_ref, w1_hbm, w3_hbm, w2_hbm, o_ref,
                w1_buf, w3_buf, w2_buf, sem, *, sub, tiles_per_batch, alpha, finish_rows, tf):
    i = pl.program_id(0)
    n_tiles = pl.num_programs(0)
    bidx = i // tiles_per_batch
    tm = x_ref.shape[0]
    n_f = w1_hbm.shape[1] // tf

    def chunk_copies(c, slot):
        cols = pl.ds(pl.multiple_of(c * tf, tf), tf)
        return (pltpu.make_async_copy(w1_hbm.at[:, cols], w1_buf.at[slot], sem.at[0, slot]),
                pltpu.make_async_copy(w3_hbm.at[:, cols], w3_buf.at[slot], sem.at[1, slot]),
                pltpu.make_async_copy(w2_hbm.at[cols, :], w2_buf.at[slot], sem.at[2, slot]))

    def start_chunk(c, slot):
        for cp in chunk_copies(c, slot):
            cp.start()

    def wait_chunk(c, slot):
        for cp in chunk_copies(c, slot):
            cp.wait()

    @pl.when(i == 0)
    def _():
        start_chunk(0, 0)

    shift = mod_ref[bidx, 3 * sub:3 * sub + 1, :]
    scale1 = 1.0 + mod_ref[bidx, 3 * sub + 1:3 * sub + 2, :]

    def partial_out(x_rows, w1, w3, w2):
        u = (x_rows * scale1 + shift).astype(BF16)
        a = jnp.dot(u, w1, preferred_element_type=F32)
        b = jnp.dot(u, w3, preferred_element_type=F32)
        h = (_silu(a) * b).astype(BF16)
        return jnp.dot(h, w2, preferred_element_type=F32)

    def chunk_weights(slot):
        return (w1_buf[slot].astype(BF16), w3_buf[slot].astype(BF16), w2_buf[slot].astype(BF16))

    start_chunk(1, 1)
    wait_chunk(0, 0)
    o_ref[...] = partial_out(x_ref[...], *chunk_weights(0))

    def chunk_pair(k, carry):
        c1 = 2 * k + 1
        start_chunk(c1 + 1, 0)
        wait_chunk(c1, 1)
        o_ref[...] += partial_out(x_ref[...], *chunk_weights(1))
        start_chunk(c1 + 2, 1)
        wait_chunk(c1 + 1, 0)
        o_ref[...] += partial_out(x_ref[...], *chunk_weights(0))
        return carry

    lax.fori_loop(0, (n_f - 2) // 2, chunk_pair, 0)

    @pl.when(i + 1 < n_tiles)
    def _():
        start_chunk(0, 0)

    wait_chunk(n_f - 1, 1)
    gate = FFN_RES_WEIGHT * (1.0 + mod_ref[bidx, 3 * sub + 2:3 * sub + 3, :])
    g = g_ref[...]
    b_ln = b_ref[...]
    last_w = chunk_weights(1)
    for r in range(tm // finish_rows):
        rows = slice(r * finish_rows, (r + 1) * finish_rows)
        x_rows = x_ref[rows, :]
        y = o_ref[rows, :] + partial_out(x_rows, *last_w)
        o_ref[rows, :] = _layer_norm_rows(alpha * x_rows + gate * y, g, b_ln)


def _ffn(x2d, mod, w1, w3, w2, ln_g, ln_b, *, sub, seq, alpha, tm=1024, tf=256, finish_rows=256):
    n_tok, d = x2d.shape
    f = w1.shape[1]
    tm = min(tm, seq)
    assert f % (2 * tf) == 0 and f // tf >= 4 and tm % finish_rows == 0
    kern = functools.partial(_ffn_kernel, sub=sub, tiles_per_batch=seq // tm, alpha=alpha,
                             finish_rows=finish_rows, tf=tf)
    hbm = pl.BlockSpec(memory_space=pl.ANY)
    return pl.pallas_call(
        kern,
        grid=(n_tok // tm,),
        in_specs=[
            pl.BlockSpec((tm, d), lambda i: (i, 0)),
            pl.BlockSpec(mod.shape, lambda i: (0, 0, 0)),
            pl.BlockSpec((1, d), lambda i: (0, 0)),
            pl.BlockSpec((1, d), lambda i: (0, 0)),
            hbm, hbm, hbm,
        ],
        out_specs=pl.BlockSpec((tm, d), lambda i: (i, 0)),
        out_shape=jax.ShapeDtypeStruct((n_tok, d), F32),
        scratch_shapes=[
            pltpu.VMEM((2, d, tf), w1.dtype),
            pltpu.VMEM((2, d, tf), w3.dtype),
            pltpu.VMEM((2, tf, d), w2.dtype),
            pltpu.SemaphoreType.DMA((3, 2)),
        ],
        compiler_params=_params(("arbitrary",)),
        name=f"ffn{sub}",
    )(x2d, mod, ln_g.reshape(1, d), ln_b.reshape(1, d), w1, w3, w2)


def _inproj_kernel(x_hbm, mod_ref, w_ref, oa_ref, ob_ref, wbf_ref, x_buf, x_sem, *,
                   sub, tiles_per_batch, n_a, n_valid):
    i = pl.program_id(0)
    j = pl.program_id(1)
    n_tiles = pl.num_programs(0)
    bidx = i // tiles_per_batch
    tn = w_ref.shape[0]
    tm = x_buf.shape[1]
    slot = i % 2

    def x_copy(tile, to_slot):
        rows = pl.ds(pl.multiple_of(tile * tm, tm), tm)
        return pltpu.make_async_copy(x_hbm.at[rows, :], x_buf.at[to_slot], x_sem.at[to_slot])

    @pl.when(j == 0)
    def _():
        @pl.when(i == 0)
        def _():
            x_copy(0, 0).start()

        x_copy(i, slot).wait()

        @pl.when(i + 1 < n_tiles)
        def _():
            x_copy(i + 1, 1 - slot).start()

    @pl.when(i == 0)
    def _():
        out_col = j * tn + lax.broadcasted_iota(jnp.int32, w_ref.shape, 0)
        wbf_ref[j] = jnp.where(out_col < n_valid, w_ref[...], 0.0).astype(BF16)

    def project():
        shift = mod_ref[bidx, 3 * sub:3 * sub + 1, :]
        scale1 = 1.0 + mod_ref[bidx, 3 * sub + 1:3 * sub + 2, :]
        u = (x_buf[slot] * scale1 + shift).astype(BF16)
        return lax.dot_general(u, wbf_ref[j], (((1,), (1,)), ((), ())), preferred_element_type=F32)

    @pl.when(j < n_a)
    def _():
        oa_ref[...] = project().astype(BF16)

    @pl.when(j >= n_a)
    def _():
        ob_ref[...] = project()


def _inproj(x2d, mod, w_in_t, *, sub, seq, tm=1024, tn=512):
    n_tok, d = x2d.shape
    n_a = PA_COLS // tn
    n_b = PB_COLS // tn
    n_blocks = n_a + n_b
    kern = functools.partial(_inproj_kernel, sub=sub, tiles_per_batch=seq // tm, n_a=n_a,
                             n_valid=w_in_t.shape[0])
    return pl.pallas_call(
        kern,
        grid=(n_tok // tm, n_blocks),
        in_specs=[
            pl.BlockSpec(memory_space=pl.ANY),
            pl.BlockSpec(mod.shape, lambda i, j: (0, 0, 0)),
            pl.BlockSpec((tn, d), lambda i, j: (jnp.where(i == 0, j, n_blocks - 1), 0)),
        ],
        out_specs=[
            pl.BlockSpec((tm, tn), lambda i, j: (i, jnp.minimum(j, n_a - 1))),
            pl.BlockSpec((tm, tn), lambda i, j: (i, jnp.maximum(j - n_a, 0))),
        ],
        out_shape=[
            jax.ShapeDtypeStruct((n_tok, PA_COLS), BF16),
            jax.ShapeDtypeStruct((n_tok, PB_COLS), F32),
        ],
        scratch_shapes=[
            pltpu.VMEM((n_blocks, tn, d), BF16),
            pltpu.VMEM((2, tm, d), x2d.dtype),
            pltpu.SemaphoreType.DMA((2,)),
        ],
        compiler_params=_params(("arbitrary", "arbitrary")),
        name="inproj",
    )(x2d, mod, w_in_t)


def _lane_scan(v, op, chunk):
    lane = lax.broadcasted_iota(jnp.int32, v.shape, 1) % chunk
    ident = 0.0 if op is jnp.add else NEG_BIG
    s = 1
    while s < chunk:
        shifted = pltpu.roll(v, s, axis=1)
        v = op(v, jnp.where(lane >= s, shifted, ident))
        s *= 2
    return v


def _mlstm_gate_scans(bi, gates_ref, gb_ref, scan_ref, chunk):
    seq = gates_ref.shape[1]
    for c in range(seq // LANES):
        cols = slice(c * LANES, (c + 1) * LANES)
        scan_ref[bi, 0, :, cols] = gates_ref[bi, cols, :].T[0:SUBLANES, :]
    z = scan_ref[bi, 0] + gb_ref[:, 0:1]
    logi = pltpu.roll(z, ML_HEADS, axis=0)
    logf = jnp.minimum(z, 0.0) - jnp.log1p(jnp.exp(-jnp.abs(z)))
    bcum = _lane_scan(logf, jnp.add, chunk)
    r = logi - bcum
    scan_ref[bi, 0] = bcum
    scan_ref[bi, 1] = r
    scan_ref[bi, 2] = _lane_scan(r, jnp.maximum, chunk)


def _mlstm_kernel(qk_ref, v_ref, og_ref, gates_ref, cw_ref, cb_ref, gb_ref, ng_ref, sel_ref, cbias_ref,
                  o_ref, c_ref, m_ref, prev_ref, scan_ref, *, chunk):
    t = pl.program_id(1)

    @pl.when(t == 0)
    def _():
        c_ref[...] = jnp.zeros_like(c_ref)
        m_ref[...] = jnp.zeros_like(m_ref)
        prev_ref[...] = jnp.zeros_like(prev_ref)
        for bi in range(qk_ref.shape[0]):
            _mlstm_gate_scans(bi, gates_ref, gb_ref, scan_ref, chunk)

    _mlstm_step(t, qk_ref, v_ref, og_ref, cw_ref, cb_ref, ng_ref, sel_ref, cbias_ref, o_ref,
                c_ref, m_ref, prev_ref, scan_ref, chunk)


def _mlstm_step(t, qk_ref, v_ref, og_ref, cw_ref, cb_ref, ng_ref, sel_ref, cbias_ref, o_ref,
                c_ref, m_ref, prev_ref, scan_ref, chunk):
    hq = ML_HEADS * ML_QK_DIM
    n_rows = qk_ref.shape[0]
    chains = [(bi, h) for bi in range(n_rows) for h in range(ML_HEADS)]
    head_cols = lambda h: slice(h * ML_V_DIM, (h + 1) * ML_V_DIM)
    causal_bias = cbias_ref[...]
    ones_col = (lax.broadcasted_iota(jnp.int32, (chunk, LANES), 1) == 0).astype(BF16)
    lanes = pl.ds(pl.multiple_of(t * chunk, chunk), chunk)

    qk = []
    for bi in range(n_rows):
        qk_now = qk_ref[bi]
        taps = jnp.dot(sel_ref[...], jnp.concatenate([prev_ref[bi], qk_now], axis=0),
                       preferred_element_type=F32)
        prev_ref[bi] = qk_now
        conv = cb_ref[...]
        for tap in range(CONV_WIDTH):
            conv = conv + cw_ref[tap:tap + 1, :] * taps[tap * chunk:(tap + 1) * chunk, :]
        qk.append(_silu(conv))

    r, cols, s_old = [], [], []
    for bi in range(n_rows):
        bcum = scan_ref[bi, 0, :, lanes]
        r_b = scan_ref[bi, 1, :, lanes]
        cm = scan_ref[bi, 2, :, lanes]
        m_in = m_ref[bi, :, 0:1]
        m_all = jnp.maximum(m_in, cm)
        m_last = m_all[:, chunk - 1:chunk]
        m_ref[bi] = jnp.broadcast_to(bcum[:, chunk - 1:chunk] + m_last, (SUBLANES, LANES))
        s_inter = jnp.exp(m_in - m_all)
        e_neg_m = jnp.exp(-(bcum + m_all))
        w_last = jnp.exp(r_b - m_last)
        stack = jnp.concatenate(
            [m_all, s_inter, e_neg_m, w_last, jnp.zeros((LANES - 4 * SUBLANES, chunk), F32)], axis=0)
        r.append(r_b)
        cols.append(stack.T)
        s_old.append(jnp.exp(m_in - m_last))

    def col(bi, h, which):
        lane = which * SUBLANES + ML_HEADS + h
        return cols[bi][:, lane:lane + 1]

    q_h = {c: qk[c[0]][:, c[1] * ML_QK_DIM:(c[1] + 1) * ML_QK_DIM].astype(BF16) for c in chains}
    k_f = {c: qk[c[0]][:, hq + c[1] * ML_QK_DIM:hq + (c[1] + 1) * ML_QK_DIM] * (ML_QK_DIM ** -0.5)
           for c in chains}
    v_aug = {c: jnp.concatenate([v_ref[c[0], :, head_cols(c[1])], ones_col], axis=1) for c in chains}
    scores = {c: lax.dot_general(q_h[c], k_f[c].astype(BF16), (((1,), (1,)), ((), ())),
                                 preferred_element_type=F32) for c in chains}
    p = {}
    for bi, h in chains:
        decay = jnp.exp(r[bi][ML_HEADS + h:ML_HEADS + h + 1, :] - col(bi, h, 0) + causal_bias)
        p[bi, h] = (decay * scores[bi, h]).astype(BF16)
    c_old = {c: c_ref[c[0], c[1]] for c in chains}
    tot = {c: (jnp.dot(p[c], v_aug[c], preferred_element_type=F32)
               + col(c[0], c[1], 1) * jnp.dot(q_h[c], c_old[c].astype(BF16), preferred_element_type=F32))
           for c in chains}
    for bi, h in chains:
        k_w = (k_f[bi, h] * col(bi, h, 3)).astype(BF16)
        c_ref[bi, h] = s_old[bi][ML_HEADS + h:ML_HEADS + h + 1, :] * c_old[bi, h] + lax.dot_general(
            k_w, v_aug[bi, h], (((0,), (0,)), ((), ())), preferred_element_type=F32)
    for bi, h in chains:
        num = tot[bi, h][:, :ML_V_DIM]
        den = tot[bi, h][:, ML_V_DIM:ML_V_DIM + 1]
        hid = num / jnp.maximum(jnp.abs(den), col(bi, h, 2))
        mu = jnp.mean(hid, axis=-1, keepdims=True)
        hc = hid - mu
        var = jnp.mean(hc * hc, axis=-1, keepdims=True)
        hn = hc * lax.rsqrt(var + LN_EPS) * ng_ref[:, head_cols(h)]
        gate = jax.nn.sigmoid(og_ref[bi, :, head_cols(h)].astype(F32))
        o_ref[bi, :, head_cols(h)] = (hn * gate).astype(o_ref.dtype)


def _mlstm(pa, pb, conv_w, conv_b, b_igate, b_fgate, norm_g, *, chunk=128, rows_per_step=2):
    bsz, seq, _ = pa.shape
    rb = rows_per_step
    gate_bias = jnp.broadcast_to(jnp.concatenate([b_igate, b_fgate])[:, None], (SUBLANES, LANES))
    t_idx = jnp.arange(CONV_WIDTH * chunk) % chunk
    tap_idx = jnp.arange(CONV_WIDTH * chunk) // chunk
    shift_sel = (jnp.arange(2 * chunk)[None, :]
                 == (chunk + t_idx - (CONV_WIDTH - 1) + tap_idx)[:, None]).astype(BF16)
    causal_bias = jnp.where(jnp.arange(chunk)[None, :] <= jnp.arange(chunk)[:, None], 0.0, NEG_BIG).astype(F32)
    kern = functools.partial(_mlstm_kernel, chunk=chunk)
    return pl.pallas_call(
        kern,
        grid=(bsz // rb, seq // chunk),
        in_specs=[
            pl.BlockSpec((rb, chunk, ML_QK_COLS), lambda b, t: (b, t, 0)),
            pl.BlockSpec((rb, chunk, ML_V_COLS), lambda b, t: (b, t, ML_QK_COLS // ML_V_COLS)),
            pl.BlockSpec((rb, chunk, ML_V_COLS), lambda b, t: (b, t, ML_QK_COLS // ML_V_COLS + 1)),
            pl.BlockSpec((rb, seq, LANES), lambda b, t: (b, 0, PB_GATES // LANES)),
            pl.BlockSpec((CONV_WIDTH, ML_QK_COLS), lambda b, t: (0, 0)),
            pl.BlockSpec((1, ML_QK_COLS), lambda b, t: (0, 0)),
            pl.BlockSpec((SUBLANES, LANES), lambda b, t: (0, 0)),
            pl.BlockSpec((1, ML_V_COLS), lambda b, t: (0, 0)),
            pl.BlockSpec(shift_sel.shape, lambda b, t: (0, 0)),
            pl.BlockSpec(causal_bias.shape, lambda b, t: (0, 0)),
        ],
        out_specs=pl.BlockSpec((rb, chunk, ML_V_COLS), lambda b, t: (b, t, 0)),
        out_shape=jax.ShapeDtypeStruct((bsz, seq, ML_V_COLS), BF16),
        scratch_shapes=[
            pltpu.VMEM((rb, ML_HEADS, ML_QK_DIM, ML_V_DIM + LANES), F32),
            pltpu.VMEM((rb, SUBLANES, LANES), F32),
            pltpu.VMEM((rb, chunk, ML_QK_COLS), pa.dtype),
            pltpu.VMEM((rb, 3, SUBLANES, seq), F32),
        ],
        compiler_params=_params(("arbitrary", "arbitrary")),
        name="mlstm",
    )(pa, pa, pa, pb, conv_w, conv_b.reshape(1, -1), gate_bias, norm_g.reshape(1, -1),
      shift_sel, causal_bias)


def _rope_lanes(x, cos, sin):
    return x * cos + pltpu.roll(x, LANES // 2, axis=1) * sin


def _rms_rows(x, g):
    return x * lax.rsqrt(jnp.mean(x * x, axis=-1, keepdims=True) + LN_EPS) * g


def _mla_prep_kernel(xa_ref, xb_ref, qg_ref, kvg_ref, wq_ref, wkv_ref, cos_ref, sin_ref,
                     q_ref, k_ref, v_ref, *, q_scale, q_shift, kv_cols, kv_shift, kr_group, kr_shift):
    cos = cos_ref[...]
    sin = sin_ref[...]
    half = MLA_ROPE_DIM // 2
    xa = xa_ref[...]
    xb = xb_ref[:, :kv_cols]
    c_q = pltpu.roll(xa, xa.shape[1] - q_shift, axis=1)[:, :Q_LORA_RANK]
    c_kv = pltpu.roll(xb, xb.shape[1] - kv_shift, axis=1)[:, :KV_LORA_RANK]
    q_lat = _rms_rows(c_q, qg_ref[...]).astype(BF16)
    kv_lat = _rms_rows(c_kv, kvg_ref[...]).astype(BF16)
    k_r = pltpu.roll(xb_ref[:, kr_group * LANES:(kr_group + 1) * LANES], LANES - kr_shift, axis=1)
    lane = lax.broadcasted_iota(jnp.int32, k_r.shape, 1)
    k_r = (jnp.where(lane < half, k_r, 0.0)
           + jnp.where((lane >= LANES // 2) & (lane < LANES // 2 + half), pltpu.roll(k_r, half, axis=1), 0.0))
    k_rope = _rope_lanes(k_r, cos, sin).astype(k_ref.dtype)
    group = lambda h: slice(h * MLA_QK_GROUP, (h + 1) * MLA_QK_GROUP)
    q_all = [jnp.dot(q_lat, wq_ref[:, group(h)], preferred_element_type=F32) for h in range(MLA_HEADS)]
    kv_all = [jnp.dot(kv_lat, wkv_ref[:, group(h)], preferred_element_type=F32) for h in range(MLA_HEADS)]
    for h in range(MLA_HEADS):
        lo = h * MLA_QK_GROUP
        q_h = q_all[h] * q_scale
        q_ref[:, lo:lo + LANES] = q_h[:, :LANES].astype(q_ref.dtype)
        q_ref[:, lo + LANES:lo + 2 * LANES] = _rope_lanes(q_h[:, LANES:], cos, sin).astype(q_ref.dtype)
        k_ref[:, lo:lo + LANES] = kv_all[h][:, :LANES].astype(k_ref.dtype)
        k_ref[:, lo + LANES:lo + 2 * LANES] = k_rope
        v_ref[:, h * MLA_V_DIM:(h + 1) * MLA_V_DIM] = kv_all[h][:, LANES:].astype(v_ref.dtype)


def _rope_tables(seq):
    half = MLA_ROPE_DIM // 2
    inv = ROPE_THETA ** (-jnp.arange(half, dtype=F32) / half)
    ang = jnp.arange(seq, dtype=F32)[:, None] * inv[None, :]
    cos, sin = jnp.cos(ang), jnp.sin(ang)
    zero = jnp.zeros_like(cos)
    return (jnp.concatenate([cos, zero, cos, zero], axis=1),
            jnp.concatenate([-sin, zero, sin, zero], axis=1))


def _mla_prep(pb2d, q_norm_g, kv_norm_g, w_uq, w_ukv, *, seq, tm=512):
    n_tok = pb2d.shape[0]
    dq = MLA_NOPE_DIM + MLA_ROPE_DIM
    xb_width = 512
    xb_start = (PB_CKV // xb_width) * xb_width
    assert PB_VALID <= xb_start + xb_width and PB_COLS % xb_width == 0
    xa_cols = -(-PB_CKV // LANES) * LANES
    ckv_lo = PB_CKV - xb_start
    kv_cols = -(-(ckv_lo + KV_LORA_RANK) // LANES) * LANES
    kr_lo = PB_KR - xb_start
    assert kr_lo % LANES + MLA_ROPE_DIM <= LANES
    half = MLA_ROPE_DIM // 2
    wq = w_uq.reshape(Q_LORA_RANK, MLA_HEADS, dq)
    gap = jnp.zeros(wq.shape[:2] + (LANES // 2 - half,), wq.dtype)
    wq = jnp.concatenate([wq[..., :MLA_NOPE_DIM], wq[..., MLA_NOPE_DIM:MLA_NOPE_DIM + half], gap,
                          wq[..., MLA_NOPE_DIM + half:], gap], axis=-1)
    wq = wq.reshape(Q_LORA_RANK, MLA_HEADS * MLA_QK_GROUP).astype(BF16)
    wkv = w_ukv.astype(BF16)
    cos_t, sin_t = _rope_tables(seq)
    tiles_per_seq = seq // tm
    pos = lambda i: (i % tiles_per_seq, 0)
    kern = functools.partial(_mla_prep_kernel, q_scale=math.log2(math.e) / math.sqrt(dq),
                             q_shift=PB_CQ, kv_cols=kv_cols, kv_shift=ckv_lo,
                             kr_group=kr_lo // LANES, kr_shift=kr_lo % LANES)
    qk_cols = MLA_HEADS * MLA_QK_GROUP
    const = lambda i: (0, 0)
    return pl.pallas_call(
        kern,
        grid=(n_tok // tm,),
        in_specs=[
            pl.BlockSpec((tm, xa_cols), lambda i: (i, 0)),
            pl.BlockSpec((tm, xb_width), lambda i: (i, xb_start // xb_width)),
            pl.BlockSpec((1, Q_LORA_RANK), const),
            pl.BlockSpec((1, KV_LORA_RANK), const),
            pl.BlockSpec((Q_LORA_RANK, qk_cols), const),
            pl.BlockSpec((KV_LORA_RANK, qk_cols), const),
            pl.BlockSpec((tm, LANES), pos),
            pl.BlockSpec((tm, LANES), pos),
        ],
        out_specs=[
            pl.BlockSpec((tm, qk_cols), lambda i: (i, 0)),
            pl.BlockSpec((tm, qk_cols), lambda i: (i, 0)),
            pl.BlockSpec((tm, MLA_HEADS * MLA_V_DIM), lambda i: (i, 0)),
        ],
        out_shape=[
            jax.ShapeDtypeStruct((n_tok, qk_cols), BF16),
            jax.ShapeDtypeStruct((n_tok, qk_cols), BF16),
            jax.ShapeDtypeStruct((n_tok, MLA_HEADS * MLA_V_DIM), BF16),
        ],
        compiler_params=_params(("arbitrary",)),
        name="mla_prep",
    )(pb2d, pb2d, q_norm_g.reshape(1, -1), kv_norm_g.reshape(1, -1), wq, wkv, cos_t, sin_t)


def _flash_kernel(q_ref, k_ref, v_ref, o_ref, m_ref, acc_ref, *, tq, heads):
    qi = pl.program_id(2)
    m_ref[...] = jnp.full_like(m_ref, NEG_BIG)
    acc_ref[...] = jnp.zeros_like(acc_ref)
    ones_col = (lax.broadcasted_iota(jnp.int32, (tq, LANES), 1) == 0).astype(BF16)
    on_or_below_diag = (lax.broadcasted_iota(jnp.int32, (tq, tq), 1)
                        <= lax.broadcasted_iota(jnp.int32, (tq, tq), 0))

    def key_block(j, masked):
        rows = pl.ds(pl.multiple_of(j * tq, tq), tq)
        hs = range(heads)
        s = [lax.dot_general(q_ref[0, :, h * MLA_QK_GROUP:(h + 1) * MLA_QK_GROUP],
                             k_ref[0, rows, h * MLA_QK_GROUP:(h + 1) * MLA_QK_GROUP],
                             (((1,), (1,)), ((), ())), preferred_element_type=F32) for h in hs]
        if masked:
            s = [jnp.where(on_or_below_diag, s_h, NEG_BIG) for s_h in s]
        m_prev = [m_ref[h] for h in hs]
        m_new = [jnp.maximum(m_prev[h], jnp.max(s[h], axis=-1, keepdims=True)) for h in hs]
        p = [jnp.exp2(s[h] - jnp.concatenate([m_new[h]] * (tq // LANES), axis=1)).astype(BF16) for h in hs]
        alpha = [jnp.exp2(m_prev[h] - m_new[h]) for h in hs]
        for h in hs:
            v_aug = jnp.concatenate([v_ref[0, rows, h * MLA_V_DIM:(h + 1) * MLA_V_DIM], ones_col], axis=1)
            pv = jnp.dot(p[h], v_aug, preferred_element_type=F32)
            acc_ref[h] = jnp.concatenate([alpha[h], alpha[h]], axis=1) * acc_ref[h] + pv
            m_ref[h] = m_new[h]

    def body(j, carry):
        key_block(j, False)
        return carry

    lax.fori_loop(0, qi, body, 0)
    key_block(qi, True)
    for h in range(heads):
        acc = acc_ref[h]
        o_ref[0, :, h * MLA_V_DIM:(h + 1) * MLA_V_DIM] = (
            acc[:, :MLA_V_DIM] / acc[:, MLA_V_DIM:MLA_V_DIM + 1]).astype(o_ref.dtype)


def _flash(q, k, v, *, tq=512, heads=8):
    bsz, seq, _ = q.shape
    kern = functools.partial(_flash_kernel, tq=tq, heads=heads)
    return pl.pallas_call(
        kern,
        grid=(bsz, MLA_HEADS // heads, seq // tq),
        in_specs=[
            pl.BlockSpec((1, tq, heads * MLA_QK_GROUP), lambda b, g, qi: (b, qi, g)),
            pl.BlockSpec((1, seq, heads * MLA_QK_GROUP), lambda b, g, qi: (b, 0, g)),
            pl.BlockSpec((1, seq, heads * MLA_V_DIM), lambda b, g, qi: (b, 0, g)),
        ],
        out_specs=pl.BlockSpec((1, tq, heads * MLA_V_DIM), lambda b, g, qi: (b, qi, g)),
        out_shape=jax.ShapeDtypeStruct((bsz, seq, MLA_HEADS * MLA_V_DIM), BF16),
        scratch_shapes=[
            pltpu.VMEM((heads, tq, LANES), F32),
            pltpu.VMEM((heads, tq, MLA_V_DIM + LANES), F32),
        ],
        compiler_params=_params(("arbitrary", "arbitrary", "arbitrary")),
        name="flash",
    )(q, k, v)


def _outproj_kernel(x_ref, hml_ref, hmla_ref, mod_ref, w_ref, g_ref, b_ref, o_ref, wb_ref, *,
                    sub, tiles_per_batch, alpha, row_chunk, norm_rows):
    i = pl.program_id(0)
    bidx = i // tiles_per_batch
    k_ml = hml_ref.shape[1]

    @pl.when(i == 0)
    def _():
        def body(r, carry):
            rows = pl.ds(pl.multiple_of(r * row_chunk, row_chunk), row_chunk)
            wb_ref[rows, :] = w_ref[rows, :].astype(BF16)
            return carry

        lax.fori_loop(0, w_ref.shape[0] // row_chunk, body, 0)

    gate = 1.0 + mod_ref[bidx, 3 * sub + 2:3 * sub + 3, :]
    g = g_ref[...]
    b_ln = b_ref[...]
    w_top = wb_ref[0:k_ml, :]
    w_bot = wb_ref[k_ml:, :]
    for r in range(x_ref.shape[0] // norm_rows):
        rows = slice(r * norm_rows, (r + 1) * norm_rows)
        y = (jnp.dot(hml_ref[rows, :], w_top, preferred_element_type=F32)
             + jnp.dot(hmla_ref[rows, :], w_bot, preferred_element_type=F32))
        o_ref[rows, :] = _layer_norm_rows(alpha * x_ref[rows, :] + gate * y, g, b_ln)


def _outproj(x2d, hml, hmla, mod, w_out, ln_g, ln_b, *, sub, seq, alpha, tm=512, row_chunk=128,
             norm_rows=128):
    n_tok, d = x2d.shape
    kern = functools.partial(_outproj_kernel, sub=sub, tiles_per_batch=seq // tm, alpha=alpha,
                             row_chunk=row_chunk, norm_rows=norm_rows)
    return pl.pallas_call(
        kern,
        grid=(n_tok // tm,),
        in_specs=[
            pl.BlockSpec((tm, d), lambda i: (i, 0)),
            pl.BlockSpec((tm, hml.shape[1]), lambda i: (i, 0)),
            pl.BlockSpec((tm, hmla.shape[1]), lambda i: (i, 0)),
            pl.BlockSpec(mod.shape, lambda i: (0, 0, 0)),
            pl.BlockSpec(w_out.shape, lambda i: (0, 0), pipeline_mode=pl.Buffered(1)),
            pl.BlockSpec((1, d), lambda i: (0, 0)),
            pl.BlockSpec((1, d), lambda i: (0, 0)),
        ],
        out_specs=pl.BlockSpec((tm, d), lambda i: (i, 0)),
        out_shape=jax.ShapeDtypeStruct((n_tok, d), F32),
        scratch_shapes=[pltpu.VMEM(w_out.shape, BF16)],
        compiler_params=_params(("arbitrary",)),
        name="outproj",
    )(x2d, hml, hmla, mod, w_out, ln_g.reshape(1, d), ln_b.reshape(1, d))


def kernel(x, c, w_ada, b_ada, ffn1_w1, ffn1_w3, ffn1_w2, ln1_g, ln1_b, w_in, conv_w, conv_b, b_igate, b_fgate, ml_norm_g, q_norm_g, w_uq, kv_norm_g, w_ukv, w_out, ln2_g, ln2_b, ffn2_w1, ffn2_w3, ffn2_w2, ln3_g, ln3_b):
    bsz, seq, d = x.shape
    depth = w_ada.shape[0]
    alpha = (2.0 * depth) ** 0.25
    h2d = x.reshape(bsz * seq, d)
    for l in range(depth):
        mod = _adaln(c, w_ada[l], b_ada[l]).reshape(bsz, N_SUBLAYERS * 3, d)
        h2d = _ffn(h2d, mod, ffn1_w1[l], ffn1_w3[l], ffn1_w2[l], ln1_g[l], ln1_b[l],
                   sub=0, seq=seq, alpha=alpha)
        assert w_in.shape[2] == D_IN
        pa, pb = _inproj(h2d, mod, w_in[l].T, sub=1, seq=seq)
        hml = _mlstm(pa.reshape(bsz, seq, PA_COLS), pb.reshape(bsz, seq, PB_COLS), conv_w[l], conv_b[l],
                     b_igate[l], b_fgate[l], ml_norm_g[l])
        q, k, v = _mla_prep(pb, q_norm_g[l], kv_norm_g[l], w_uq[l], w_ukv[l], seq=seq)
        hmla = _flash(q.reshape(bsz, seq, -1), k.reshape(bsz, seq, -1), v.reshape(bsz, seq, -1))
        h2d = _outproj(h2d, hml.reshape(bsz * seq, -1), hmla.reshape(bsz * seq, -1), mod,
                       w_out[l], ln2_g[l], ln2_b[l], sub=1, seq=seq, alpha=alpha)
        h2d = _ffn(h2d, mod, ffn2_w1[l], ffn2_w3[l], ffn2_w2[l], ln3_g[l], ln3_b[l],
                   sub=2, seq=seq, alpha=alpha)
    return h2d.reshape(bsz, seq, d)
```
